```python
import jax, jax.numpy as jnp
from jax import lax
import numpy as np

D_MODEL = 1024
BATCH = 8
SEQ = 2048
DEPTH = 4

PLE_DIM = 256
D_MIX = D_MODEL
CONV_WIDTH = D_MIX // 4
CONV_GROUPS = 4
CONV_KERNEL = 31
SSD_WIDTH = D_MIX // 2
SSD_HEADDIM = 64
SSD_HEADS = SSD_WIDTH // SSD_HEADDIM
SSD_NGROUPS = 2
SSD_STATE = 128
SSD_CONV = 4
SSD_CHUNK = 128
MLA_HEADS = 4
MLA_NOPE_DIM = 64
MLA_ROPE_DIM = 32
MLA_V_DIM = D_MIX // 4 // MLA_HEADS
MLA_WIDTH = MLA_HEADS * MLA_V_DIM
MLA_Q_RANK = D_MODEL // 4
MLA_KV_RANK = D_MODEL // 8
ROPE_BASE = 10000.0
Q_BLOCK = 128
D_FF = 7 * D_MODEL // 2
N_EXPERTS = 8
TOP_K = 2
MOE_BLOCK = 128
N_DENSE = (DEPTH + 1) // 2
N_MOE = DEPTH // 2
CONV_IN = 2 * CONV_WIDTH
SSD_XBC = SSD_WIDTH + 2 * SSD_NGROUPS * SSD_STATE
IN_COLS = CONV_IN + SSD_WIDTH + SSD_XBC + SSD_HEADS + MLA_Q_RANK + MLA_KV_RANK + MLA_ROPE_DIM
RMS_EPS = 1e-6
LN_EPS = 1e-5

kernel_name = 'hybrid_conv_ssd_mla_moe_ple_trunk'


def rms_norm(x, g):
    xf = x.astype(jnp.float32)
    y = xf * lax.rsqrt(jnp.mean(xf * xf, axis=-1, keepdims=True) + RMS_EPS)
    return (y * g.astype(jnp.float32)).astype(x.dtype)


def causal_depthwise_conv(x, w, b):
    k, c = w.shape
    y = lax.conv_general_dilated(x, w.astype(x.dtype)[:, None, :], window_strides=(1,),
                                 padding=((k - 1, 0),), dimension_numbers=('NWC', 'WIO', 'NWC'),
                                 feature_group_count=c)
    return y + b.astype(x.dtype)


def split_in(u):
    sizes = (CONV_IN, SSD_WIDTH, SSD_XBC, SSD_HEADS, MLA_Q_RANK, MLA_KV_RANK, MLA_ROPE_DIM)
    points = np.cumsum(sizes)[:-1].tolist()
    return jnp.split(u, points, axis=-1)


def conformer_conv(u, dw_w, dw_b, ln_g, ln_b):
    val, gate = jnp.split(u, 2, axis=-1)
    h = causal_depthwise_conv(val * jax.nn.sigmoid(gate), dw_w, dw_b)
    b, s, c = h.shape
    hg = h.astype(jnp.float32).reshape(b, s, CONV_GROUPS, c // CONV_GROUPS)
    mu = jnp.mean(hg, axis=-1, keepdims=True)
    var = jnp.mean(jnp.square(hg - mu), axis=-1, keepdims=True)
    hn = ((hg - mu) * lax.rsqrt(var + LN_EPS)).reshape(b, s, c)
    hn = hn * ln_g.astype(jnp.float32) + ln_b.astype(jnp.float32)
    return jax.nn.silu(hn).astype(u.dtype)


def ssd_chunked(x, dt, a, bm, cm):
    b, s, h, pd = x.shape
    g, n = bm.shape[2], bm.shape[3]
    r = h // g
    c, L = s // SSD_CHUNK, SSD_CHUNK
    xd = (x * dt[..., None]).reshape(b, c, L, g, r, pd)
    adt = (dt * a).reshape(b, c, L, g, r).transpose(0, 3, 4, 1, 2)
    a_cs = jnp.cumsum(adt, axis=-1)
    bc = bm.reshape(b, c, L, g, n)
    cc = cm.reshape(b, c, L, g, n)
    causal = jnp.tril(jnp.ones((L, L), dtype=bool))
    seg = a_cs[..., :, None] - a_cs[..., None, :]
    decay = jnp.exp(jnp.where(causal, seg, -jnp.inf))
    cb = jnp.einsum('bclgn,bcsgn->bgcls', cc, bc)
    mix = cb[:, :, None] * decay
    y_diag = jnp.einsum('bgrcls,bcsgrp->bclgrp', mix, xd)
    decay_states = jnp.exp(a_cs[..., -1:] - a_cs)
    states = jnp.einsum('bclgn,bgrcl,bclgrp->cbgrpn', bc, decay_states, xd)
    chunk_decay = jnp.moveaxis(jnp.exp(a_cs[..., -1]), -1, 0)

    def step(carry, inp):
        st, dec = inp
        return carry * dec[..., None, None] + st, carry

    init = jnp.zeros((b, g, r, pd, n), jnp.float32)
    _, prev = lax.scan(step, init, (states, chunk_decay))
    y_off = jnp.einsum('bclgn,cbgrpn,bgrcl->bclgrp', cc, prev, jnp.exp(a_cs))
    return (y_diag + y_off).reshape(b, s, h, pd)


def ssd_mixer(z, xbc, dt_raw, conv_w, conv_b, dt_bias, a_log, d_skip, norm_g):
    b, s, _ = xbc.shape
    xbc = jax.nn.silu(causal_depthwise_conv(xbc, conv_w, conv_b)).astype(jnp.float32)
    gn = SSD_NGROUPS * SSD_STATE
    xs = xbc[..., :SSD_WIDTH].reshape(b, s, SSD_HEADS, SSD_HEADDIM)
    bm = xbc[..., SSD_WIDTH:SSD_WIDTH + gn].reshape(b, s, SSD_NGROUPS, SSD_STATE)
    cm = xbc[..., SSD_WIDTH + gn:].reshape(b, s, SSD_NGROUPS, SSD_STATE)
    dt = jax.nn.softplus(dt_raw.astype(jnp.float32) + dt_bias.astype(jnp.float32))
    a = -jnp.exp(a_log.astype(jnp.float32))
    y = ssd_chunked(xs, dt, a, bm, cm) + d_skip.astype(jnp.float32)[:, None] * xs
    yg = (y.reshape(b, s, SSD_WIDTH) * jax.nn.silu(z.astype(jnp.float32)))
    yg = yg.reshape(b, s, SSD_NGROUPS, SSD_WIDTH // SSD_NGROUPS)
    yg = yg * lax.rsqrt(jnp.mean(yg * yg, axis=-1, keepdims=True) + RMS_EPS)
    return (yg.reshape(b, s, SSD_WIDTH) * norm_g.astype(jnp.float32)).astype(z.dtype)


def rope_tables(positions, dtype):
    inv = ROPE_BASE ** (-jnp.arange(0, MLA_ROPE_DIM, 2, dtype=jnp.float32) / MLA_ROPE_DIM)
    ang = positions.astype(jnp.float32)[..., None] * inv
    return jnp.cos(ang).astype(dtype), jnp.sin(ang).astype(dtype)


def apply_rope(x, cos, sin):
    half = x.shape[-1] // 2
    x1, x2 = x[..., :half], x[..., half:]
    return jnp.concatenate([x1 * cos - x2 * sin, x1 * sin + x2 * cos], axis=-1)


def mla_mixer(c_q, c_kv, k_rope, q_norm_g, w_uq, kv_norm_g, w_ukv, cos, sin):
    b, s, _ = c_q.shape
    q = (rms_norm(c_q, q_norm_g) @ w_uq).reshape(b, s, MLA_HEADS, MLA_NOPE_DIM + MLA_ROPE_DIM)
    q_nope, q_pe = q[..., :MLA_NOPE_DIM], q[..., MLA_NOPE_DIM:]
    kv = (rms_norm(c_kv, kv_norm_g) @ w_ukv).reshape(b, s, MLA_HEADS, MLA_NOPE_DIM + MLA_V_DIM)
    k_nope, v = kv[..., :MLA_NOPE_DIM], kv[..., MLA_NOPE_DIM:]
    q_pe = apply_rope(q_pe, cos[:, :, None, :], sin[:, :, None, :])
    k_pe = apply_rope(k_rope, cos, sin)
    scale = (MLA_NOPE_DIM + MLA_ROPE_DIM) ** -0.5
    outs = []
    for blk in range(s // Q_BLOCK):
        q0, q1 = blk * Q_BLOCK, (blk + 1) * Q_BLOCK
        sc = (jnp.einsum('bqhd,bkhd->bhqk', q_nope[:, q0:q1], k_nope[:, :q1])
              + jnp.einsum('bqhd,bkd->bhqk', q_pe[:, q0:q1], k_pe[:, :q1])).astype(jnp.float32) * scale
        mask = (q0 + jnp.arange(Q_BLOCK))[:, None] >= jnp.arange(q1)[None, :]
        pr = jax.nn.softmax(jnp.where(mask, sc, -jnp.inf), axis=-1).astype(v.dtype)
        outs.append(jnp.einsum('bhqk,bkhd->bqhd', pr, v[:, :q1]))
    return jnp.concatenate(outs, axis=1).reshape(b, s, MLA_WIDTH)


def swiglu(h, wg, wu, wd):
    return (jax.nn.silu(h @ wg) * (h @ wu)) @ wd


def moe_top2(h, router, wg, wu, wd):
    b, s, d = h.shape
    t = b * s
    hf = h.reshape(t, d)
    logits = (hf @ router).astype(jnp.float32)
    top_val, top_idx = lax.top_k(logits, TOP_K)
    gates = jax.nn.softmax(top_val, axis=-1)
    e_flat = top_idx.reshape(-1)
    tok_flat = jnp.arange(t * TOP_K, dtype=jnp.int32) // TOP_K
    g_flat = gates.reshape(-1)
    order = jnp.argsort(e_flat)
    e_sorted = e_flat[order]
    counts = jnp.zeros((N_EXPERTS,), jnp.int32).at[e_flat].add(1)
    padded = ((counts + MOE_BLOCK - 1) // MOE_BLOCK) * MOE_BLOCK
    starts = jnp.cumsum(counts) - counts
    pends = jnp.cumsum(padded)
    pstarts = pends - padded
    dest = pstarts[e_sorted] + (jnp.arange(t * TOP_K, dtype=jnp.int32) - starts[e_sorted])
    n_blocks = -(-(t * TOP_K) // MOE_BLOCK) + N_EXPERTS
    n_slots = n_blocks * MOE_BLOCK
    slot_tok = jnp.full((n_slots,), t, jnp.int32).at[dest].set(tok_flat[order])
    slot_gate = jnp.zeros((n_slots,), jnp.float32).at[dest].set(g_flat[order])
    block_exp = jnp.minimum(jnp.searchsorted(pends, jnp.arange(n_blocks) * MOE_BLOCK, side='right'),
                            N_EXPERTS - 1)
    x_pad = jnp.concatenate([hf, jnp.zeros((1, d), hf.dtype)], axis=0)
    xb = x_pad[slot_tok].reshape(n_blocks, MOE_BLOCK, d)

    def expert_block(args):
        xblk, e = args
        return swiglu(xblk, wg[e], wu[e], wd[e])

    yb = lax.map(expert_block, (xb, block_exp))
    ys = yb.reshape(n_slots, d) * slot_gate[:, None].astype(h.dtype)
    y = jnp.zeros((t + 1, d), h.dtype).at[slot_tok].add(ys)[:t]
    return y.reshape(b, s, d)


def setup_inputs(seed: int = 0) -> dict:
    key = jax.random.key(seed)
    ks = jax.random.split(key, 40)
    f32 = jnp.float32

    def nrm(k, shape, scale):
        return jax.random.normal(k, shape, f32) * scale

    def gain(k, shape):
        return 1.0 + 0.05 * jax.random.normal(k, shape, f32)

    dt = jnp.exp(jax.random.uniform(ks[11], (DEPTH, SSD_HEADS), f32)
                 * (jnp.log(0.1) - jnp.log(0.001)) + jnp.log(0.001))
    return {
        'x': nrm(ks[0], (BATCH, SEQ, D_MODEL), 1.0),
        'p': nrm(ks[1], (DEPTH, BATCH, SEQ, PLE_DIM), 1.0),
        'positions': jnp.arange(SEQ, dtype=jnp.int32)[None, :]
                     + jax.random.randint(ks[2], (BATCH, 1), 0, 1024, dtype=jnp.int32),
        'attn_norm_g': gain(ks[3], (DEPTH, D_MODEL)),
        'w_in': nrm(ks[4], (DEPTH, D_MODEL, IN_COLS), D_MODEL ** -0.5),
        'conv_dw_w': nrm(ks[5], (DEPTH, CONV_KERNEL, CONV_WIDTH), CONV_KERNEL ** -0.5),
        'conv_dw_b': nrm(ks[6], (DEPTH, CONV_WIDTH), 0.02),
        'conv_ln_g': gain(ks[7], (DEPTH, CONV_WIDTH)),
        'conv_ln_b': nrm(ks[8], (DEPTH, CONV_WIDTH), 0.02),
        'ssd_conv_w': nrm(ks[9], (DEPTH, SSD_CONV, SSD_XBC), SSD_CONV ** -0.5),
        'ssd_conv_b': nrm(ks[10], (DEPTH, SSD_XBC), 0.02),
        'ssd_dt_bias': dt + jnp.log(-jnp.expm1(-dt)),
        'ssd_a_log': jnp.log(jax.random.uniform(ks[12], (DEPTH, SSD_HEADS), f32, 1.0, 16.0)),
        'ssd_d': gain(ks[13], (DEPTH, SSD_HEADS)),
        'ssd_norm_g': gain(ks[14], (DEPTH, SSD_WIDTH)),
        'mla_q_norm_g': gain(ks[15], (DEPTH, MLA_Q_RANK)),
        'mla_w_uq': nrm(ks[16], (DEPTH, MLA_Q_RANK, MLA_HEADS * (MLA_NOPE_DIM + MLA_ROPE_DIM)), MLA_Q_RANK ** -0.5),
        'mla_kv_norm_g': gain(ks[17], (DEPTH, MLA_KV_RANK)),
        'mla_w_ukv': nrm(ks[18], (DEPTH, MLA_KV_RANK, MLA_HEADS * (MLA_NOPE_DIM + MLA_V_DIM)), MLA_KV_RANK ** -0.5),
        'w_out': nrm(ks[19], (DEPTH, D_MIX, D_MODEL), D_MIX ** -0.5),
        'ffn_norm_g': gain(ks[20], (DEPTH, D_MODEL)),
        'dense_w_gate': nrm(ks[21], (N_DENSE, D_MODEL, D_FF), D_MODEL ** -0.5),
        'dense_w_up': nrm(ks[22], (N_DENSE, D_MODEL, D_FF), D_MODEL ** -0.5),
        'dense_w_down': nrm(ks[23], (N_DENSE, D_FF, D_MODEL), D_FF ** -0.5),
        'moe_router': nrm(ks[24], (N_MOE, D_MODEL, N_EXPERTS), D_MODEL ** -0.5),
        'moe_w_gate': nrm(ks[25], (N_MOE, N_EXPERTS, D_MODEL, D_FF), D_MODEL ** -0.5),
        'moe_w_up': nrm(ks[26], (N_MOE, N_EXPERTS, D_MODEL, D_FF), D_MODEL ** -0.5),
        'moe_w_down': nrm(ks[27], (N_MOE, N_EXPERTS, D_FF, D_MODEL), D_FF ** -0.5),
        'ple_norm_g': gain(ks[28], (DEPTH, D_MODEL)),
        'ple_w_gate': nrm(ks[29], (DEPTH, D_MODEL, D_MODEL), D_MODEL ** -0.5),
        'ple_w_proj': nrm(ks[30], (DEPTH, PLE_DIM, D_MODEL), PLE_DIM ** -0.5),
        'final_norm_g': gain(ks[31], (D_MODEL,)),
    }


def reference(x, p, positions, attn_norm_g, w_in, conv_dw_w, conv_dw_b, conv_ln_g, conv_ln_b,
              ssd_conv_w, ssd_conv_b, ssd_dt_bias, ssd_a_log, ssd_d, ssd_norm_g,
              mla_q_norm_g, mla_w_uq, mla_kv_norm_g, mla_w_ukv, w_out, ffn_norm_g,
              dense_w_gate, dense_w_up, dense_w_down, moe_router, moe_w_gate, moe_w_up, moe_w_down,
              ple_norm_g, ple_w_gate, ple_w_proj, final_norm_g):
    h = x
    cos, sin = rope_tables(positions, x.dtype)
    for i in range(DEPTH):
        u = rms_norm(h, attn_norm_g[i]) @ w_in[i]
        u_conv, z, xbc, dt_raw, c_q, c_kv, k_rope = split_in(u)
        y_conv = conformer_conv(u_conv, conv_dw_w[i], conv_dw_b[i], conv_ln_g[i], conv_ln_b[i])
        y_ssd = ssd_mixer(z, xbc, dt_raw, ssd_conv_w[i], ssd_conv_b[i], ssd_dt_bias[i],
                          ssd_a_log[i], ssd_d[i], ssd_norm_g[i])
        y_mla = mla_mixer(c_q, c_kv, k_rope, mla_q_norm_g[i], mla_w_uq[i], mla_kv_norm_g[i],
                          mla_w_ukv[i], cos, sin)
        h = h + jnp.concatenate([y_conv, y_ssd, y_mla], axis=-1) @ w_out[i]
        hn = rms_norm(h, ffn_norm_g[i])
        if i % 2 == 0:
            j = i // 2
            h = h + swiglu(hn, dense_w_gate[j], dense_w_up[j], dense_w_down[j])
        else:
            j = i // 2
            h = h + moe_top2(hn, moe_router[j], moe_w_gate[j], moe_w_up[j], moe_w_down[j])
        gate = jax.nn.sigmoid(rms_norm(h, ple_norm_g[i]) @ ple_w_gate[i])
        h = h + (p[i] @ ple_w_proj[i]) * gate
    return rms_norm(h, final_norm_g)
```

```python
import functools

import numpy as np
import jax
import jax.numpy as jnp
from jax import lax
from jax.experimental import pallas as pl
from jax.experimental.pallas import tpu as pltpu

F32 = jnp.float32
BF16 = jnp.bfloat16

D_MODEL = 1024
DEPTH = 4
PLE_DIM = 256
CONV_WIDTH = 256
CONV_GROUPS = 4
CONV_KERNEL = 31
SSD_WIDTH = 512
SSD_HEADDIM = 64
SSD_HEADS = 8
SSD_NGROUPS = 2
SSD_STATE = 128
SSD_CONV = 4
SSD_CHUNK = 128
MLA_HEADS = 4
MLA_NOPE = 64
MLA_ROPE = 32
MLA_V = 64
MLA_Q_RANK = 256
MLA_KV_RANK = 128
ROPE_BASE = 10000.0
D_FF = 3584
N_EXPERTS = 8
RMS_EPS = 1e-6
LN_EPS = 1e-5

LANES = 128
HEAD_PAD = 128
ROPE_LO = MLA_NOPE
VMEM_LIMIT = 48 * 1024 * 1024

C_CONV = 0
C_Z = 512
C_XBC = 1024
C_CQ = 2048
C_CKV = 2304
C_KR = 2432
C_DTR = 2560
IN_COLS_PAD = 2688


def _cparams(sem):
    return pltpu.CompilerParams(dimension_semantics=sem, vmem_limit_bytes=VMEM_LIMIT)


def _dot(a, b):
    return jnp.dot(a, b, preferred_element_type=F32)


def _dot_nt(a, b):
    return lax.dot_general(a, b, (((1,), (1,)), ((), ())), preferred_element_type=F32)


def _split3(a):
    a1 = a.astype(BF16)
    r1 = a - a1.astype(F32)
    a2 = r1.astype(BF16)
    a3 = (r1 - a2.astype(F32)).astype(BF16)
    return a1, a2, a3


def _dot_f32_lhs(a, m):
    a1, a2, a3 = _split3(a)
    return _dot(a1, m) + _dot(a2, m) + _dot(a3, m)


def _dot_f32_rhs(m, b):
    b1, b2, b3 = _split3(b)
    return _dot(m, b1) + _dot(m, b2) + _dot(m, b3)


def _rms(x, g, eps=RMS_EPS):
    return x * lax.rsqrt(jnp.mean(x * x, axis=-1, keepdims=True) + eps) * g


def _sigmoid(x):
    return 1.0 / (1.0 + jnp.exp(-x))


def _silu(x):
    return x * _sigmoid(x)


def _rope_kernel(pos_ref, inv_ref, c_ref, s_ref):
    ang = pos_ref[...] * inv_ref[...]
    lane = lax.broadcasted_iota(jnp.int32, ang.shape, 1)
    in_rope = (lane >= ROPE_LO) & (lane < ROPE_LO + MLA_ROPE)
    first_half = lane < ROPE_LO + MLA_ROPE // 2
    cos = jnp.cos(ang)
    sin = jnp.sin(ang)
    c_ref[...] = jnp.where(in_rope, cos, jnp.where(lane < ROPE_LO, 1.0, 0.0))
    s_ref[...] = jnp.where(in_rope, jnp.where(first_half, -sin, sin), 0.0)


def _rope_tables(pos128, inv128, tm):
    t = pos128.shape[0]
    return pl.pallas_call(
        _rope_kernel,
        grid=(t // tm,),
        in_specs=[pl.BlockSpec((tm, LANES), lambda i: (i, 0)),
                  pl.BlockSpec((1, LANES), lambda i: (0, 0))],
        out_specs=[pl.BlockSpec((tm, LANES), lambda i: (i, 0))] * 2,
        out_shape=[jax.ShapeDtypeStruct((t, LANES), F32)] * 2,
        compiler_params=_cparams(("parallel",)),
        name="rope_tables",
    )(pos128, inv128)


def _inproj_kernel(h_ref, g_ref, w_ref, c_ref, s_ref, gq_ref, wq_ref, gkv_ref, wkv_ref,
                   oconv_ref, oz_ref, oxbc_ref, odt_ref, oq_ref, ok_ref, ov_ref):
    xn = _rms(h_ref[...], g_ref[...]).astype(BF16)
    oconv_ref[...] = _dot(xn, w_ref[:, C_CONV:C_Z])
    oz_ref[...] = _dot(xn, w_ref[:, C_Z:C_XBC])
    oxbc_ref[...] = _dot(xn, w_ref[:, C_XBC:C_CQ])
    cq = _dot(xn, w_ref[:, C_CQ:C_CKV])
    ckv = _dot(xn, w_ref[:, C_CKV:C_KR])
    kr = _dot(xn, w_ref[:, C_KR:C_DTR])
    dtr = _dot(xn, w_ref[:, C_DTR:IN_COLS_PAD])
    odt_ref[...] = dtr
    c = c_ref[...]
    s = s_ref[...]
    c4 = jnp.concatenate([c] * MLA_HEADS, axis=1)
    s4 = jnp.concatenate([s] * MLA_HEADS, axis=1)
    qq = _dot(_rms(cq, gq_ref[...]).astype(BF16), wq_ref[...])
    nq = MLA_HEADS * HEAD_PAD
    scale = (MLA_NOPE + MLA_ROPE) ** -0.5
    oq_ref[...] = ((qq[:, :nq] * c4 + qq[:, nq:] * s4) * scale).astype(BF16)
    kv = _dot(_rms(ckv, gkv_ref[...]).astype(BF16), wkv_ref[...])
    kpe = kr * c + dtr * s
    ok_ref[...] = (kv[:, :nq] + jnp.concatenate([kpe] * MLA_HEADS, axis=1)).astype(BF16)
    ov_ref[...] = kv[:, nq:].astype(BF16)


def _inproj(h, g, w, ctab, stab, gq, wq, gkv, wkv, tm):
    t = h.shape[0]
    row = lambda n: pl.BlockSpec((tm, n), lambda i: (i, 0))
    full = lambda a: pl.BlockSpec(a.shape, lambda i: (0,) * a.ndim)
    nq = MLA_HEADS * HEAD_PAD
    widths = (512, 512, 1024, LANES, nq, nq, MLA_HEADS * MLA_V)
    dtypes = (F32, F32, F32, F32, BF16, BF16, BF16)
    return pl.pallas_call(
        _inproj_kernel,
        grid=(t // tm,),
        in_specs=[row(D_MODEL), full(g), full(w), row(LANES), row(LANES), full(gq), full(wq), full(gkv), full(wkv)],
        out_specs=[row(n) for n in widths],
        out_shape=[jax.ShapeDtypeStruct((t, n), d) for n, d in zip(widths, dtypes)],
        compiler_params=_cparams(("parallel",)),
        name="inproj",
    )(h, g, w, ctab, stab, gq, wq, gkv, wkv)


CONV_HALO = 32
CONV_SUB = 64


def _conv_kernel(u_ref, w_ref, b_ref, lg_ref, lb_ref, gm_ref, o_ref, gbuf, *, tc):
    @pl.when(pl.program_id(1) == 0)
    def _():
        gbuf[0:CONV_HALO, :] = jnp.zeros((CONV_HALO, CONV_WIDTH), F32)

    u = u_ref[0]
    gbuf[CONV_HALO:CONV_HALO + tc, :] = u[:, :CONV_WIDTH] * _sigmoid(u[:, CONV_WIDTH:])
    gm = gm_ref[...]
    first = CONV_HALO - (CONV_KERNEL - 1)
    for r0 in range(0, tc, CONV_SUB):
        acc = jnp.broadcast_to(b_ref[...], (CONV_SUB, CONV_WIDTH))
        for j in range(CONV_KERNEL):
            acc = acc + w_ref[j:j + 1, :] * gbuf[first + j + r0:first + j + r0 + CONV_SUB, :]
        mu = _dot_f32_lhs(acc, gm)
        d = acc - mu
        var = _dot_f32_lhs(d * d, gm)
        hn = d * lax.rsqrt(var + LN_EPS) * lg_ref[...] + lb_ref[...]
        o_ref[0, r0:r0 + CONV_SUB, :] = _silu(hn).astype(BF16)
    gbuf[0:CONV_HALO, :] = gbuf[tc:tc + CONV_HALO, :]


def _conformer_conv(u, w, b, lg, lb, gm, tc):
    bsz, s, _ = u.shape
    full = lambda a: pl.BlockSpec(a.shape, lambda i, j: (0,) * a.ndim)
    return pl.pallas_call(
        functools.partial(_conv_kernel, tc=tc),
        grid=(bsz, s // tc),
        in_specs=[pl.BlockSpec((1, tc, 2 * CONV_WIDTH), lambda i, j: (i, j, 0)),
                  full(w), full(b), full(lg), full(lb), full(gm)],
        out_specs=pl.BlockSpec((1, tc, CONV_WIDTH), lambda i, j: (i, j, 0)),
        out_shape=jax.ShapeDtypeStruct((bsz, s, CONV_WIDTH), BF16),
        scratch_shapes=[pltpu.VMEM((CONV_HALO + tc, CONV_WIDTH), F32)],
        compiler_params=_cparams(("parallel", "arbitrary")),
        name="conformer_conv",
    )(u, w, b, lg, lb, gm)


SSD_HALO = 8
SSD_XBC = SSD_WIDTH + 2 * SSD_NGROUPS * SSD_STATE
GROUP_W = SSD_WIDTH // SSD_NGROUPS
HEADS_PER_GROUP = SSD_HEADS // SSD_NGROUPS


def _ssd_kernel(xbc_ref, z_ref, dt_ref, cw_ref, cb_ref, dtb_ref, alog_ref, dsk_ref, ng_ref, tril_ref, exp_ref,
                o_ref, cbuf, state):
    L = SSD_CHUNK

    @pl.when(pl.program_id(1) == 0)
    def _():
        cbuf[0:SSD_HALO, :] = jnp.zeros((SSD_HALO, SSD_XBC), F32)
        state[...] = jnp.zeros(state.shape, F32)

    cbuf[SSD_HALO:SSD_HALO + L, :] = xbc_ref[0]
    first = SSD_HALO - (SSD_CONV - 1)
    acc = jnp.broadcast_to(cb_ref[...], (L, SSD_XBC))
    for j in range(SSD_CONV):
        acc = acc + cw_ref[j:j + 1, :] * cbuf[first + j:first + j + L, :]
    cbuf[0:SSD_HALO, :] = cbuf[L:L + SSD_HALO, :]
    xc = _silu(acc)
    xs = xc[:, :SSD_WIDTH]
    bm = xc[:, SSD_WIDTH:SSD_WIDTH + SSD_NGROUPS * SSD_STATE]
    cm = xc[:, SSD_WIDTH + SSD_NGROUPS * SSD_STATE:]

    lane = lax.broadcasted_iota(jnp.int32, (1, LANES), 1)
    v = dt_ref[0] + dtb_ref[...]
    dt = jnp.maximum(v, 0.0) + jnp.log1p(jnp.exp(-jnp.abs(v)))
    a = jnp.where(lane < SSD_HEADS, -jnp.exp(alog_ref[...]), 0.0)
    cs = _dot_f32_rhs(tril_ref[...], dt * a)
    cs_t = cs.T
    cs_last = cs[L - 1:L, :]
    emat = exp_ref[...]
    dt_e = _dot_f32_lhs(dt, emat)
    ecs_e = _dot_f32_lhs(jnp.exp(cs), emat)
    ds_e = _dot_f32_lhs(jnp.exp(cs_last - cs), emat)
    cd_e = _dot_f32_lhs(jnp.broadcast_to(jnp.exp(cs_last), (8, LANES)), emat)[0:1, :]

    xd = xs * dt_e
    xdb = xd.astype(BF16)
    xds = (xd * ds_e).astype(BF16)
    rows = lax.broadcasted_iota(jnp.int32, (L, L), 0)
    cols = lax.broadcasted_iota(jnp.int32, (L, L), 1)
    causal = rows >= cols
    ys = []
    for g in range(SSD_NGROUPS):
        cmg = cm[:, g * SSD_STATE:(g + 1) * SSD_STATE].astype(BF16)
        bmg = bm[:, g * SSD_STATE:(g + 1) * SSD_STATE]
        cbm = _dot_nt(cmg, bmg.astype(BF16))
        yd = []
        for r in range(HEADS_PER_GROUP):
            h = g * HEADS_PER_GROUP + r
            seg = cs[:, h:h + 1] - cs_t[h:h + 1, :]
            dec = jnp.exp(jnp.where(causal, seg, -jnp.inf))
            mix = (cbm * dec).astype(BF16)
            yd.append(_dot(mix, xdb[:, h * SSD_HEADDIM:(h + 1) * SSD_HEADDIM]))
        gs = slice(g * GROUP_W, (g + 1) * GROUP_W)
        prev = state[g]
        y_off = _dot(cmg, prev.astype(BF16)) * ecs_e[:, gs]
        st_new = _dot(bmg.T.astype(BF16), xds[:, gs])
        state[g] = prev * cd_e[:, gs] + st_new
        ys.append(jnp.concatenate(yd, axis=1) + y_off)
    y = jnp.concatenate(ys, axis=1) + dsk_ref[...] * xs
    yg = y * _silu(z_ref[0])
    outs = []
    for g in range(SSD_NGROUPS):
        ygg = yg[:, g * GROUP_W:(g + 1) * GROUP_W]
        outs.append(ygg * lax.rsqrt(jnp.mean(ygg * ygg, axis=-1, keepdims=True) + RMS_EPS))
    o_ref[0] = (jnp.concatenate(outs, axis=1) * ng_ref[...]).astype(BF16)


def _ssd(xbc, z, dtm, cw, cb, dtb, alog, dsk, ng, tril, emat):
    bsz, s, _ = xbc.shape
    L = SSD_CHUNK
    full = lambda a: pl.BlockSpec(a.shape, lambda i, j: (0,) * a.ndim)
    blk = lambda n: pl.BlockSpec((1, L, n), lambda i, j: (i, j, 0))
    return pl.pallas_call(
        _ssd_kernel,
        grid=(bsz, s // L),
        in_specs=[blk(SSD_XBC), blk(SSD_WIDTH), blk(LANES),
                  full(cw), full(cb), full(dtb), full(alog), full(dsk), full(ng), full(tril), full(emat)],
        out_specs=blk(SSD_WIDTH),
        out_shape=jax.ShapeDtypeStruct((bsz, s, SSD_WIDTH), BF16),
        scratch_shapes=[pltpu.VMEM((SSD_HALO + L, SSD_XBC), F32),
                        pltpu.VMEM((SSD_NGROUPS, SSD_STATE, GROUP_W), F32)],
        compiler_params=_cparams(("parallel", "arbitrary")),
        name="ssd",
    )(xbc, z, dtm, cw, cb, dtb, alog, dsk, ng, tril, emat)


def _attn_kernel(q_ref, k_ref, v_ref, o_ref, *, tq):
    i = pl.program_id(1)
    q = q_ref[0]
    rows = lax.broadcasted_iota(jnp.int32, (tq, tq), 0)
    cols = lax.broadcasted_iota(jnp.int32, (tq, tq), 1)
    causal = rows >= cols
    outs = []
    for h in range(MLA_HEADS):
        qh = q[:, h * HEAD_PAD:(h + 1) * HEAD_PAD]

        def block(j, carry, masked, h=h, qh=qh):
            m, l, acc = carry
            start = pl.multiple_of(j * tq, tq)
            kj = k_ref[0, pl.ds(start, tq), h * HEAD_PAD:(h + 1) * HEAD_PAD]
            vj = v_ref[0, pl.ds(start, tq), h * MLA_V:(h + 1) * MLA_V]
            sc = _dot_nt(qh, kj)
            if masked:
                sc = jnp.where(causal, sc, -jnp.inf)
            m_new = jnp.maximum(m, jnp.max(sc, axis=-1, keepdims=True))
            p = jnp.exp(sc - m_new)
            alpha = jnp.exp(m - m_new)
            l = alpha * l + jnp.sum(p, axis=-1, keepdims=True)
            acc = alpha * acc + _dot(p.astype(BF16), vj)
            return m_new, l, acc

        init = (jnp.full((tq, 1), -jnp.inf, F32), jnp.zeros((tq, 1), F32), jnp.zeros((tq, MLA_V), F32))
        carry = lax.fori_loop(0, i, functools.partial(block, masked=False), init)
        _, l, acc = block(i, carry, True)
        outs.append(acc / l)
    o_ref[0] = jnp.concatenate(outs, axis=1).astype(BF16)


def _attention(q, k, v, tq):
    bsz, s, _ = q.shape
    nq = MLA_HEADS * HEAD_PAD
    nv = MLA_HEADS * MLA_V
    return pl.pallas_call(
        functools.partial(_attn_kernel, tq=tq),
        grid=(bsz, s // tq),
        in_specs=[pl.BlockSpec((1, tq, nq), lambda b, i: (b, i, 0)),
                  pl.BlockSpec((1, s, nq), lambda b, i: (b, 0, 0)),
                  pl.BlockSpec((1, s, nv), lambda b, i: (b, 0, 0))],
        out_specs=pl.BlockSpec((1, tq, nv), lambda b, i: (b, i, 0)),
        out_shape=jax.ShapeDtypeStruct((bsz, s, nv), BF16),
        compiler_params=_cparams(("parallel", "parallel")),
        name="mla_attention",
    )(q, k, v)


M_IDX0, M_IDX1, M_GATE0, M_GATE1, M_RANK0, M_RANK1 = range(6)


def _outproj_body(yc_ref, ys_ref, ym_ref, h_ref, w_ref, g_ref):
    acc = _dot(yc_ref[...], w_ref[0:CONV_WIDTH, :])
    acc = acc + _dot(ys_ref[...], w_ref[CONV_WIDTH:CONV_WIDTH + SSD_WIDTH, :])
    acc = acc + _dot(ym_ref[...], w_ref[CONV_WIDTH + SSD_WIDTH:, :])
    h1 = h_ref[...] + acc
    return h1, _rms(h1, g_ref[...])


def _outproj_dense_kernel(yc_ref, ys_ref, ym_ref, h_ref, w_ref, g_ref, h1_ref, hn_ref):
    h1, hn = _outproj_body(yc_ref, ys_ref, ym_ref, h_ref, w_ref, g_ref)
    h1_ref[...] = h1
    hn_ref[...] = hn.astype(BF16)


def _outproj_moe_kernel(yc_ref, ys_ref, ym_ref, h_ref, w_ref, g_ref, r_ref, lt_ref,
                        h1_ref, hn_ref, meta_ref, cnt_ref):
    @pl.when(pl.program_id(0) == 0)
    def _():
        cnt_ref[...] = jnp.zeros(cnt_ref.shape, F32)

    h1, hn = _outproj_body(yc_ref, ys_ref, ym_ref, h_ref, w_ref, g_ref)
    h1_ref[...] = h1
    hn_ref[...] = hn
    logits = _dot(hn.astype(BF16), r_ref[...])
    lane = lax.broadcasted_iota(jnp.int32, logits.shape, 1)
    lm = jnp.where(lane < N_EXPERTS, logits, -jnp.inf)
    m1 = jnp.max(lm, axis=-1, keepdims=True)
    i1 = jnp.min(jnp.where(lm == m1, lane, LANES), axis=-1, keepdims=True)
    lm2 = jnp.where(lane == i1, -jnp.inf, lm)
    m2 = jnp.max(lm2, axis=-1, keepdims=True)
    i2 = jnp.min(jnp.where(lm2 == m2, lane, LANES), axis=-1, keepdims=True)
    e = jnp.exp(m2 - m1)
    g1 = 1.0 / (1.0 + e)
    g2 = e / (1.0 + e)
    onehot = jnp.where((lane == i1) | (lane == i2), 1.0, 0.0)
    before = _dot(lt_ref[...], onehot.astype(BF16)) + cnt_ref[0:1, :]
    r1 = jnp.sum(jnp.where(lane == i1, before, 0.0), axis=-1, keepdims=True)
    r2 = jnp.sum(jnp.where(lane == i2, before, 0.0), axis=-1, keepdims=True)
    cnt_ref[...] = cnt_ref[...] + jnp.sum(onehot, axis=0, keepdims=True)
    meta = jnp.where(lane == M_IDX0, i1.astype(F32), 0.0)
    meta = jnp.where(lane == M_IDX1, i2.astype(F32), meta)
    meta = jnp.where(lane == M_GATE0, g1, meta)
    meta = jnp.where(lane == M_GATE1, g2, meta)
    meta = jnp.where(lane == M_RANK0, r1, meta)
    meta = jnp.where(lane == M_RANK1, r2, meta)
    meta_ref[...] = meta


def _outproj(yc, ys, ym, h, w, g, tm, router=None, ltri=None):
    t = h.shape[0]
    row = lambda n: pl.BlockSpec((tm, n), lambda i: (i, 0))
    full = lambda a: pl.BlockSpec(a.shape, lambda i: (0,) * a.ndim)
    in_specs = [row(CONV_WIDTH), row(SSD_WIDTH), row(MLA_HEADS * MLA_V), row(D_MODEL), full(w), full(g)]
    if router is None:
        return pl.pallas_call(
            _outproj_dense_kernel,
            grid=(t // tm,),
            in_specs=in_specs,
            out_specs=[row(D_MODEL), row(D_MODEL)],
            out_shape=[jax.ShapeDtypeStruct((t, D_MODEL), F32), jax.ShapeDtypeStruct((t, D_MODEL), BF16)],
            compiler_params=_cparams(("parallel",)),
            name="outproj_dense",
        )(yc, ys, ym, h, w, g)
    return pl.pallas_call(
        _outproj_moe_kernel,
        grid=(t // tm,),
        in_specs=in_specs + [full(router), full(ltri)],
        out_specs=[row(D_MODEL), row(D_MODEL), row(LANES), pl.BlockSpec((8, LANES), lambda i: (0, 0))],
        out_shape=[jax.ShapeDtypeStruct((t, D_MODEL), F32), jax.ShapeDtypeStruct((t, D_MODEL), F32),
                   jax.ShapeDtypeStruct((t, LANES), F32), jax.ShapeDtypeStruct((8, LANES), F32)],
        compiler_params=_cparams(("arbitrary",)),
        name="outproj_moe",
    )(yc, ys, ym, h, w, g, router, ltri)


def _ffn_kernel(hn_ref, h1_ref, wg_ref, wu_ref, wd_ref, o_ref):
    @pl.when(pl.program_id(1) == 0)
    def _():
        o_ref[...] = h1_ref[...]

    x = hn_ref[...]
    mid = (_silu(_dot(x, wg_ref[...])) * _dot(x, wu_ref[...])).astype(BF16)
    o_ref[...] += _dot(mid, wd_ref[...])


def _ffn_dense(hn, h1, wg, wu, wd, tm, fc):
    t = hn.shape[0]
    return pl.pallas_call(
        _ffn_kernel,
        grid=(t // tm, D_FF // fc),
        in_specs=[pl.BlockSpec((tm, D_MODEL), lambda i, f: (i, 0)),
                  pl.BlockSpec((tm, D_MODEL), lambda i, f: (i, 0)),
                  pl.BlockSpec((D_MODEL, fc), lambda i, f: (0, f)),
                  pl.BlockSpec((D_MODEL, fc), lambda i, f: (0, f)),
                  pl.BlockSpec((fc, D_MODEL), lambda i, f: (f, 0))],
        out_specs=pl.BlockSpec((tm, D_MODEL), lambda i, f: (i, 0)),
        out_shape=jax.ShapeDtypeStruct((t, D_MODEL), F32),
        compiler_params=_cparams(("parallel", "arbitrary")),
        name="ffn_dense",
    )(hn, h1, wg, wu, wd)


def _row_copy(src_ref, src_row, dst_ref, dst_row, sem):
    return pltpu.make_async_copy(src_ref.at[pl.ds(src_row, 1)], dst_ref.at[pl.ds(dst_row, 1)], sem)


def _dispatch_kernel(dest_ref, x_ref, zeros_ref, xs_ref, sem, *, tm):
    del zeros_ref

    def issue(r, c):
        _row_copy(x_ref, r, xs_ref, dest_ref[0, 0, 2 * r], sem).start()
        _row_copy(x_ref, r, xs_ref, dest_ref[0, 0, 2 * r + 1], sem).start()
        return c

    lax.fori_loop(0, tm, issue, 0)

    def drain(r, c):
        _row_copy(x_ref, 0, xs_ref, 0, sem).wait()
        return c

    lax.fori_loop(0, 2 * tm, drain, 0)


def _dispatch(dest3, hn, n_slots, tm):
    t = hn.shape[0]
    zeros = jnp.zeros((n_slots, D_MODEL), F32)
    return pl.pallas_call(
        functools.partial(_dispatch_kernel, tm=tm),
        grid=(t // tm,),
        in_specs=[pl.BlockSpec((1, 1, 2 * tm), lambda i: (i, 0, 0), memory_space=pltpu.SMEM),
                  pl.BlockSpec((tm, D_MODEL), lambda i: (i, 0)),
                  pl.BlockSpec(memory_space=pl.ANY)],
        out_specs=pl.BlockSpec(memory_space=pl.ANY),
        out_shape=jax.ShapeDtypeStruct((n_slots, D_MODEL), F32),
        scratch_shapes=[pltpu.SemaphoreType.DMA],
        input_output_aliases={2: 0},
        compiler_params=_cparams(("arbitrary",)),
        name="moe_dispatch",
    )(dest3, hn, zeros)


def _moe_ffn_kernel(texp_ref, nused_ref, x_ref, wg_ref, wu_ref, wd_ref, o_ref):
    del texp_ref

    @pl.when(pl.program_id(1) == 0)
    def _():
        o_ref[...] = jnp.zeros(o_ref.shape, F32)

    @pl.when(pl.program_id(0) < nused_ref[0])
    def _():
        x = x_ref[...].astype(BF16)
        mid = (_silu(_dot(x, wg_ref[0])) * _dot(x, wu_ref[0])).astype(BF16)
        o_ref[...] += _dot(mid, wd_ref[0])


def _moe_ffn(tile_exp, n_used, xs, wg, wu, wd, tm, fc):
    n_slots = xs.shape[0]
    nf = D_FF // fc

    def fsel(i, f, nu):
        return jnp.where(i < nu[0], f, nf - 1)

    grid_spec = pltpu.PrefetchScalarGridSpec(
        num_scalar_prefetch=2,
        grid=(n_slots // tm, nf),
        in_specs=[pl.BlockSpec((tm, D_MODEL), lambda i, f, te, nu: (i, 0)),
                  pl.BlockSpec((1, D_MODEL, fc), lambda i, f, te, nu: (te[i], 0, fsel(i, f, nu))),
                  pl.BlockSpec((1, D_MODEL, fc), lambda i, f, te, nu: (te[i], 0, fsel(i, f, nu))),
                  pl.BlockSpec((1, fc, D_MODEL), lambda i, f, te, nu: (te[i], fsel(i, f, nu), 0))],
        out_specs=pl.BlockSpec((tm, D_MODEL), lambda i, f, te, nu: (i, 0)),
    )
    return pl.pallas_call(
        _moe_ffn_kernel,
        grid_spec=grid_spec,
        out_shape=jax.ShapeDtypeStruct((n_slots, D_MODEL), F32),
        compiler_params=_cparams(("parallel", "arbitrary")),
        name="moe_ffn",
    )(tile_exp, n_used, xs, wg, wu, wd)


def _combine_kernel(dest_ref, h1_ref, meta_ref, ye_ref, o_ref, ya, yb, sem, *, tm):
    def issue(r, c):
        _row_copy(ye_ref, dest_ref[0, 0, 2 * r], ya, r, sem).start()
        _row_copy(ye_ref, dest_ref[0, 0, 2 * r + 1], yb, r, sem).start()
        return c

    lax.fori_loop(0, tm, issue, 0)

    def drain(r, c):
        _row_copy(ye_ref, 0, ya, 0, sem).wait()
        return c

    lax.fori_loop(0, 2 * tm, drain, 0)
    meta = meta_ref[...]
    g1 = meta[:, M_GATE0:M_GATE0 + 1]
    g2 = meta[:, M_GATE1:M_GATE1 + 1]
    o_ref[...] = h1_ref[...] + g1 * ya[...] + g2 * yb[...]


def _combine(dest3, h1, meta, ye, tm):
    t = h1.shape[0]
    return pl.pallas_call(
        functools.partial(_combine_kernel, tm=tm),
        grid=(t // tm,),
        in_specs=[pl.BlockSpec((1, 1, 2 * tm), lambda i: (i, 0, 0), memory_space=pltpu.SMEM),
                  pl.BlockSpec((tm, D_MODEL), lambda i: (i, 0)),
                  pl.BlockSpec((tm, LANES), lambda i: (i, 0)),
                  pl.BlockSpec(memory_space=pl.ANY)],
        out_specs=pl.BlockSpec((tm, D_MODEL), lambda i: (i, 0)),
        out_shape=jax.ShapeDtypeStruct((t, D_MODEL), F32),
        scratch_shapes=[pltpu.VMEM((tm, D_MODEL), F32), pltpu.VMEM((tm, D_MODEL), F32), pltpu.SemaphoreType.DMA],
        compiler_params=_cparams(("arbitrary",)),
        name="moe_combine",
    )(dest3, h1, meta, ye)


def _ple_kernel(h_ref, p_ref, g_ref, wg_ref, wp_ref, fg_ref, o_ref, *, final):
    h = h_ref[...]
    gate = _sigmoid(_dot(_rms(h, g_ref[...]).astype(BF16), wg_ref[...]))
    out = h + _dot(p_ref[...].astype(BF16), wp_ref[...]) * gate
    if final:
        out = _rms(out, fg_ref[...])
    o_ref[...] = out


def _ple(h, p, g, wg, wp, fg, tm, final):
    t = h.shape[0]
    row = lambda n: pl.BlockSpec((tm, n), lambda i: (i, 0))
    full = lambda a: pl.BlockSpec(a.shape, lambda i: (0,) * a.ndim)
    return pl.pallas_call(
        functools.partial(_ple_kernel, final=final),
        grid=(t // tm,),
        in_specs=[row(D_MODEL), row(PLE_DIM), full(g), full(wg), full(wp), full(fg)],
        out_specs=row(D_MODEL),
        out_shape=jax.ShapeDtypeStruct((t, D_MODEL), F32),
        compiler_params=_cparams(("parallel",)),
        name="ple",
    )(h, p, g, wg, wp, fg)


def _swap_halves(w):
    half = w.shape[-1] // 2
    return jnp.concatenate([w[..., half:], w[..., :half]], axis=-1)


def _pad_cols(w, left, total):
    return jnp.pad(w, ((0, 0), (left, total - left - w.shape[1])))


def _arrange_w_in(w):
    sizes = (512, 512, SSD_XBC, SSD_HEADS, MLA_Q_RANK, MLA_KV_RANK, MLA_ROPE)
    pts = np.cumsum(sizes)[:-1].tolist()
    w_conv, w_z, w_xbc, w_dt, w_cq, w_ckv, w_kr = jnp.split(w, pts, axis=1)
    seg_kr = _pad_cols(w_kr, ROPE_LO, LANES)
    seg_dtr = _pad_cols(w_dt, 0, LANES) + _pad_cols(_swap_halves(w_kr), ROPE_LO, LANES)
    return jnp.concatenate([w_conv, w_z, w_xbc, w_cq, w_ckv, seg_kr, seg_dtr], axis=1).astype(BF16)


def _arrange_w_uq(w):
    main, rot = [], []
    for h in range(MLA_HEADS):
        wh = w[:, h * (MLA_NOPE + MLA_ROPE):(h + 1) * (MLA_NOPE + MLA_ROPE)]
        main.append(_pad_cols(wh, 0, HEAD_PAD))
        rot.append(_pad_cols(_swap_halves(wh[:, MLA_NOPE:]), ROPE_LO, HEAD_PAD))
    return jnp.concatenate(main + rot, axis=1).astype(BF16)


def _arrange_w_ukv(w):
    ks, vs = [], []
    for h in range(MLA_HEADS):
        wh = w[:, h * (MLA_NOPE + MLA_V):(h + 1) * (MLA_NOPE + MLA_V)]
        ks.append(_pad_cols(wh[:, :MLA_NOPE], 0, HEAD_PAD))
        vs.append(wh[:, MLA_NOPE:])
    return jnp.concatenate(ks + vs, axis=1).astype(BF16)


def _row(v, width=None):
    v = v.reshape(1, -1).astype(F32)
    if width is not None:
        v = jnp.pad(v, ((0, 0), (0, width - v.shape[1])))
    return v


def _pick(n, prefs):
    for c in prefs:
        if n % c == 0:
            return c
    return n


def kernel(x, p, positions, attn_norm_g, w_in, conv_dw_w, conv_dw_b, conv_ln_g, conv_ln_b, ssd_conv_w, ssd_conv_b, ssd_dt_bias, ssd_a_log, ssd_d, ssd_norm_g, mla_q_norm_g, mla_w_uq, mla_kv_norm_g, mla_w_ukv, w_out, ffn_norm_g, dense_w_gate, dense_w_up, dense_w_down, moe_router, moe_w_gate, moe_w_up, moe_w_down, ple_norm_g, ple_w_gate, ple_w_proj, final_norm_g):
    bsz, s, _ = x.shape
    t = bsz * s
    tm_row = _pick(t, (512, 256, 128))
    tm_ffn = _pick(t, (1024, 512, 256, 128))
    fc = 512
    tc = _pick(s, (256, 128))
    tm_moe = _pick(t, (512, 256, 128))
    tm_tok = _pick(t, (256, 128))
    n_slots = 2 * t + N_EXPERTS * tm_moe

    inv = ROPE_BASE ** (-jnp.arange(0, MLA_ROPE, 2, dtype=F32) / MLA_ROPE)
    inv128 = _pad_cols(jnp.concatenate([inv, inv])[None, :], ROPE_LO, LANES)
    pos128 = jnp.broadcast_to(positions.astype(F32).reshape(t, 1), (t, LANES))
    ctab, stab = _rope_tables(pos128, inv128, tm_row)
    grp = np.arange(CONV_WIDTH) // (CONV_WIDTH // CONV_GROUPS)
    gmean = jnp.asarray((grp[:, None] == grp[None, :]) / (CONV_WIDTH // CONV_GROUPS), BF16)
    tril = jnp.asarray(np.tril(np.ones((SSD_CHUNK, SSD_CHUNK))), BF16)
    hd = np.arange(SSD_WIDTH) // SSD_HEADDIM
    emat = jnp.asarray(np.arange(LANES)[:, None] == hd[None, :], BF16)
    ltri = jnp.asarray(np.tril(np.ones((tm_row, tm_row)), -1), BF16)

    h = x.reshape(t, D_MODEL)
    for i in range(DEPTH):
        u_conv, z, xbc, dtm, q, k, v = _inproj(
            h, _row(attn_norm_g[i]), _arrange_w_in(w_in[i]), ctab, stab,
            _row(mla_q_norm_g[i]), _arrange_w_uq(mla_w_uq[i]),
            _row(mla_kv_norm_g[i]), _arrange_w_ukv(mla_w_ukv[i]), tm_row)
        y_conv = _conformer_conv(
            u_conv.reshape(bsz, s, -1), jnp.pad(conv_dw_w[i], ((0, 1), (0, 0))), _row(conv_dw_b[i]),
            _row(conv_ln_g[i]), _row(conv_ln_b[i]), gmean, tc)
        y_ssd = _ssd(
            xbc.reshape(bsz, s, -1), z.reshape(bsz, s, -1), dtm.reshape(bsz, s, -1),
            jnp.pad(ssd_conv_w[i], ((0, 8 - SSD_CONV), (0, 0))), _row(ssd_conv_b[i]),
            _row(ssd_dt_bias[i], LANES), _row(ssd_a_log[i], LANES),
            _row(jnp.repeat(ssd_d[i], SSD_HEADDIM)), _row(ssd_norm_g[i]), tril, emat)
        y_mla = _attention(q.reshape(bsz, s, -1), k.reshape(bsz, s, -1), v.reshape(bsz, s, -1), tc)
        yc, ys, ym = y_conv.reshape(t, -1), y_ssd.reshape(t, -1), y_mla.reshape(t, -1)
        wo = w_out[i].astype(BF16)
        j = i // 2
        if i % 2 == 0:
            h1, hn = _outproj(yc, ys, ym, h, wo, _row(ffn_norm_g[i]), tm_row)
            h2 = _ffn_dense(hn, h1, dense_w_gate[j].astype(BF16), dense_w_up[j].astype(BF16),
                            dense_w_down[j].astype(BF16), tm_ffn, fc)
        else:
            router = _pad_cols(moe_router[j], 0, LANES).astype(BF16)
            h1, hn, meta, cnt = _outproj(yc, ys, ym, h, wo, _row(ffn_norm_g[i]), tm_row, router, ltri)
            counts = cnt[0, :N_EXPERTS].astype(jnp.int32)
            padded = ((counts + tm_moe - 1) // tm_moe) * tm_moe
            pends = jnp.cumsum(padded)
            pstarts = pends - padded
            idx = meta[:, M_IDX0:M_IDX1 + 1].astype(jnp.int32)
            rank = meta[:, M_RANK0:M_RANK1 + 1].astype(jnp.int32)
            dest3 = (pstarts[idx] + rank).reshape(t // tm_tok, 1, 2 * tm_tok)
            n_tiles = n_slots // tm_moe
            n_used = (pends[-1] // tm_moe).astype(jnp.int32)
            tile_exp = jnp.minimum(
                jnp.searchsorted(pends, jnp.arange(n_tiles, dtype=jnp.int32) * tm_moe, side='right'),
                N_EXPERTS - 1).astype(jnp.int32)
            tile_exp = jnp.where(jnp.arange(n_tiles) < n_used, tile_exp, tile_exp[jnp.maximum(n_used - 1, 0)])
            xs = _dispatch(dest3, hn, n_slots, tm_tok)
            ye = _moe_ffn(tile_exp, n_used.reshape(1), xs, moe_w_gate[j].astype(BF16), moe_w_up[j].astype(BF16),
                          moe_w_down[j].astype(BF16), tm_moe, fc)
            h2 = _combine(dest3, h1, meta, ye, tm_tok)
        h = _ple(h2, p[i].reshape(t, PLE_DIM), _row(ple_norm_g[i]), ple_w_gate[i].astype(BF16),
                 ple_w_proj[i].astype(BF16), _row(final_norm_g), tm_row, final=(i == DEPTH - 1))
    return h.reshape(bsz, s, D_MODEL)
```

```python
import functools

import numpy as np
import jax
import jax.numpy as jnp
from jax import lax
from jax.experimental import pallas as pl
from jax.experimental.pallas import tpu as pltpu

F32 = jnp.float32
BF16 = jnp.bfloat16

D_MODEL = 1024
DEPTH = 4
PLE_DIM = 256
CONV_WIDTH = 256
CONV_GROUPS = 4
CONV_KERNEL = 31
SSD_WIDTH = 512
SSD_HEADDIM = 64
SSD_HEADS = 8
SSD_NGROUPS = 2
SSD_STATE = 128
SSD_CONV = 4
SSD_CHUNK = 128
MLA_HEADS = 4
MLA_NOPE = 64
MLA_ROPE = 32
MLA_V = 64
MLA_Q_RANK = 256
MLA_KV_RANK = 128
ROPE_BASE = 10000.0
D_FF = 3584
N_EXPERTS = 8
RMS_EPS = 1e-6
LN_EPS = 1e-5

LANES = 128
HEAD_PAD = 128
ROPE_LO = MLA_NOPE
VMEM_LIMIT = 48 * 1024 * 1024

C_CONV = 0
C_Z = 512
C_XBC = 1024
C_CQ = 2048
C_CKV = 2304
C_KR = 2432
C_DTR = 2560
IN_COLS_PAD = 2688


def _cparams(sem):
    return pltpu.CompilerParams(dimension_semantics=sem, vmem_limit_bytes=VMEM_LIMIT)


def _dot(a, b):
    return jnp.dot(a, b, preferred_element_type=F32)


def _dot_nt(a, b):
    return lax.dot_general(a, b, (((1,), (1,)), ((), ())), preferred_element_type=F32)


def _split3(a):
    a1 = a.astype(BF16)
    r1 = a - a1.astype(F32)
    a2 = r1.astype(BF16)
    a3 = (r1 - a2.astype(F32)).astype(BF16)
    return a1, a2, a3


def _dot_f32_lhs(a, m):
    a1, a2, a3 = _split3(a)
    return _dot(a1, m) + _dot(a2, m) + _dot(a3, m)


def _dot_f32_rhs(m, b):
    b1, b2, b3 = _split3(b)
    return _dot(m, b1) + _dot(m, b2) + _dot(m, b3)


def _rms(x, g, eps=RMS_EPS):
    return x * lax.rsqrt(jnp.mean(x * x, axis=-1, keepdims=True) + eps) * g


def _sigmoid(x):
    return 1.0 / (1.0 + jnp.exp(-x))


def _silu(x):
    return x * _sigmoid(x)


def _rope_kernel(pos_ref, inv_ref, c_ref, s_ref):
    ang = pos_ref[...] * inv_ref[...]
    lane = lax.broadcasted_iota(jnp.int32, ang.shape, 1)
    in_rope = (lane >= ROPE_LO) & (lane < ROPE_LO + MLA_ROPE)
    first_half = lane < ROPE_LO + MLA_ROPE // 2
    cos = jnp.cos(ang)
    sin = jnp.sin(ang)
    c_ref[...] = jnp.where(in_rope, cos, jnp.where(lane < ROPE_LO, 1.0, 0.0))
    s_ref[...] = jnp.where(in_rope, jnp.where(first_half, -sin, sin), 0.0)


def _rope_tables(pos128, inv128, tm):
    t = pos128.shape[0]
    return pl.pallas_call(
        _rope_kernel,
        grid=(t // tm,),
        in_specs=[pl.BlockSpec((tm, LANES), lambda i: (i, 0)),
                  pl.BlockSpec((1, LANES), lambda i: (0, 0))],
        out_specs=[pl.BlockSpec((tm, LANES), lambda i: (i, 0))] * 2,
        out_shape=[jax.ShapeDtypeStruct((t, LANES), F32)] * 2,
        compiler_params=_cparams(("parallel",)),
        name="rope_tables",
    )(pos128, inv128)


def _inproj_kernel(h_ref, g_ref, w_ref, c_ref, s_ref, gq_ref, wq_ref, gkv_ref, wk_ref, wvt_ref,
                   oconv_ref, oz_ref, oxbc_ref, odt_ref, oq_ref, ok_ref, ovt_ref):
    xn = _rms(h_ref[...], g_ref[...]).astype(BF16)
    oconv_ref[...] = _dot(xn, w_ref[:, C_CONV:C_Z])
    oz_ref[...] = _dot(xn, w_ref[:, C_Z:C_XBC])
    oxbc_ref[...] = _dot(xn, w_ref[:, C_XBC:C_CQ])
    cq = _dot(xn, w_ref[:, C_CQ:C_CKV])
    ckv = _dot(xn, w_ref[:, C_CKV:C_KR])
    kr = _dot(xn, w_ref[:, C_KR:C_DTR])
    dtr = _dot(xn, w_ref[:, C_DTR:IN_COLS_PAD])
    odt_ref[...] = dtr
    c = c_ref[...]
    s = s_ref[...]
    c4 = jnp.concatenate([c] * MLA_HEADS, axis=1)
    s4 = jnp.concatenate([s] * MLA_HEADS, axis=1)
    qq = _dot(_rms(cq, gq_ref[...]).astype(BF16), wq_ref[...])
    nq = MLA_HEADS * HEAD_PAD
    scale = (MLA_NOPE + MLA_ROPE) ** -0.5 * np.log2(np.e)
    oq_ref[...] = ((qq[:, :nq] * c4 + qq[:, nq:] * s4) * scale).astype(BF16)
    ckvn = _rms(ckv, gkv_ref[...]).astype(BF16)
    kpe = kr * c + dtr * s
    ok_ref[...] = (_dot(ckvn, wk_ref[...]) + jnp.concatenate([kpe] * MLA_HEADS, axis=1)).astype(BF16)
    ovt_ref[...] = _dot_nt(wvt_ref[...], ckvn).astype(BF16)


def _inproj(h, g, w, ctab, stab, gq, wq, gkv, wk, wvt, tm):
    t = h.shape[0]
    row = lambda n: pl.BlockSpec((tm, n), lambda i: (i, 0))
    full = lambda a: pl.BlockSpec(a.shape, lambda i: (0,) * a.ndim)
    nq = MLA_HEADS * HEAD_PAD
    nv = MLA_HEADS * MLA_V
    widths = (512, 512, 1024, LANES, nq, nq)
    dtypes = (F32, F32, F32, F32, BF16, BF16)
    return pl.pallas_call(
        _inproj_kernel,
        grid=(t // tm,),
        in_specs=[row(D_MODEL), full(g), full(w), row(LANES), row(LANES), full(gq), full(wq), full(gkv),
                  full(wk), full(wvt)],
        out_specs=[row(n) for n in widths] + [pl.BlockSpec((nv, tm), lambda i: (0, i))],
        out_shape=[jax.ShapeDtypeStruct((t, n), d) for n, d in zip(widths, dtypes)]
        + [jax.ShapeDtypeStruct((nv, t), BF16)],
        compiler_params=_cparams(("parallel",)),
        name="inproj",
    )(h, g, w, ctab, stab, gq, wq, gkv, wk, wvt)


CONV_HALO = 32
CONV_SUB = 64


def _conv_kernel(u_ref, w_ref, b_ref, lg_ref, lb_ref, gm_ref, o_ref, gbuf, *, tc):
    @pl.when(pl.program_id(1) == 0)
    def _():
        gbuf[0:CONV_HALO, :] = jnp.zeros((CONV_HALO, CONV_WIDTH), F32)

    u = u_ref[0]
    gbuf[CONV_HALO:CONV_HALO + tc, :] = u[:, :CONV_WIDTH] * _sigmoid(u[:, CONV_WIDTH:])
    gm = gm_ref[...]
    first = CONV_HALO - (CONV_KERNEL - 1)
    for r0 in range(0, tc, CONV_SUB):
        acc = jnp.broadcast_to(b_ref[...], (CONV_SUB, CONV_WIDTH))
        for j in range(CONV_KERNEL):
            acc = acc + w_ref[j:j + 1, :] * gbuf[first + j + r0:first + j + r0 + CONV_SUB, :]
        mu = _dot_f32_lhs(acc, gm)
        d = acc - mu
        var = _dot_f32_lhs(d * d, gm)
        hn = d * lax.rsqrt(var + LN_EPS) * lg_ref[...] + lb_ref[...]
        o_ref[0, r0:r0 + CONV_SUB, :] = _silu(hn).astype(BF16)
    gbuf[0:CONV_HALO, :] = gbuf[tc:tc + CONV_HALO, :]


def _conformer_conv(u, w, b, lg, lb, gm, tc):
    bsz, s, _ = u.shape
    full = lambda a: pl.BlockSpec(a.shape, lambda i, j: (0,) * a.ndim)
    return pl.pallas_call(
        functools.partial(_conv_kernel, tc=tc),
        grid=(bsz, s // tc),
        in_specs=[pl.BlockSpec((1, tc, 2 * CONV_WIDTH), lambda i, j: (i, j, 0)),
                  full(w), full(b), full(lg), full(lb), full(gm)],
        out_specs=pl.BlockSpec((1, tc, CONV_WIDTH), lambda i, j: (i, j, 0)),
        out_shape=jax.ShapeDtypeStruct((bsz, s, CONV_WIDTH), BF16),
        scratch_shapes=[pltpu.VMEM((CONV_HALO + tc, CONV_WIDTH), F32)],
        compiler_params=_cparams(("parallel", "arbitrary")),
        name="conformer_conv",
    )(u, w, b, lg, lb, gm)


SSD_HALO = 8
SSD_XBC = SSD_WIDTH + 2 * SSD_NGROUPS * SSD_STATE
GROUP_W = SSD_WIDTH // SSD_NGROUPS
HEADS_PER_GROUP = SSD_HEADS // SSD_NGROUPS


def _ssd_kernel(xbc_ref, z_ref, dt_ref, cw_ref, cb_ref, dtb_ref, alog_ref, dsk_ref, ng_ref, tril_ref, exp_ref,
                o_ref, cbuf, state):
    L = SSD_CHUNK

    @pl.when(pl.program_id(1) == 0)
    def _():
        cbuf[0:SSD_HALO, :] = jnp.zeros((SSD_HALO, SSD_XBC), F32)
        state[...] = jnp.zeros(state.shape, F32)

    cbuf[SSD_HALO:SSD_HALO + L, :] = xbc_ref[0]
    first = SSD_HALO - (SSD_CONV - 1)
    acc = jnp.broadcast_to(cb_ref[...], (L, SSD_XBC))
    for j in range(SSD_CONV):
        acc = acc + cw_ref[j:j + 1, :] * cbuf[first + j:first + j + L, :]
    cbuf[0:SSD_HALO, :] = cbuf[L:L + SSD_HALO, :]
    xc = _silu(acc)
    xs = xc[:, :SSD_WIDTH]
    bm = xc[:, SSD_WIDTH:SSD_WIDTH + SSD_NGROUPS * SSD_STATE]
    cm = xc[:, SSD_WIDTH + SSD_NGROUPS * SSD_STATE:]

    lane = lax.broadcasted_iota(jnp.int32, (1, LANES), 1)
    v = dt_ref[0] + dtb_ref[...]
    dt = jnp.maximum(v, 0.0) + jnp.log1p(jnp.exp(-jnp.abs(v)))
    a = jnp.where(lane < SSD_HEADS, -jnp.exp(alog_ref[...]), 0.0)
    cs = _dot_f32_rhs(tril_ref[...], dt * a)
    cs_t = cs.T
    cs_last = cs[L - 1:L, :]
    emat = exp_ref[...]
    dt_e = _dot_f32_lhs(dt, emat)
    ecs_e = _dot_f32_lhs(jnp.exp(cs), emat)
    ds_e = _dot_f32_lhs(jnp.exp(cs_last - cs), emat)
    cd_e = _dot_f32_lhs(jnp.broadcast_to(jnp.exp(cs_last), (8, LANES)), emat)[0:1, :]

    xd = xs * dt_e
    xdb = xd.astype(BF16)
    xds = (xd * ds_e).astype(BF16)
    rows = lax.broadcasted_iota(jnp.int32, (L, L), 0)
    cols = lax.broadcasted_iota(jnp.int32, (L, L), 1)
    causal = rows >= cols
    ys = []
    for g in range(SSD_NGROUPS):
        cmg = cm[:, g * SSD_STATE:(g + 1) * SSD_STATE].astype(BF16)
        bmg = bm[:, g * SSD_STATE:(g + 1) * SSD_STATE]
        cbm = _dot_nt(cmg, bmg.astype(BF16))
        yd = []
        for r in range(HEADS_PER_GROUP):
            h = g * HEADS_PER_GROUP + r
            seg = cs[:, h:h + 1] - cs_t[h:h + 1, :]
            dec = jnp.exp(jnp.where(causal, seg, -jnp.inf))
            mix = (cbm * dec).astype(BF16)
            yd.append(_dot(mix, xdb[:, h * SSD_HEADDIM:(h + 1) * SSD_HEADDIM]))
        gs = slice(g * GROUP_W, (g + 1) * GROUP_W)
        prev = state[g]
        y_off = _dot(cmg, prev.astype(BF16)) * ecs_e[:, gs]
        st_new = _dot(bmg.T.astype(BF16), xds[:, gs])
        state[g] = prev * cd_e[:, gs] + st_new
        ys.append(jnp.concatenate(yd, axis=1) + y_off)
    y = jnp.concatenate(ys, axis=1) + dsk_ref[...] * xs
    yg = y * _silu(z_ref[0])
    outs = []
    for g in range(SSD_NGROUPS):
        ygg = yg[:, g * GROUP_W:(g + 1) * GROUP_W]
        outs.append(ygg * lax.rsqrt(jnp.mean(ygg * ygg, axis=-1, keepdims=True) + RMS_EPS))
    o_ref[0] = (jnp.concatenate(outs, axis=1) * ng_ref[...]).astype(BF16)


def _ssd(xbc, z, dtm, cw, cb, dtb, alog, dsk, ng, tril, emat):
    bsz, s, _ = xbc.shape
    L = SSD_CHUNK
    full = lambda a: pl.BlockSpec(a.shape, lambda i, j: (0,) * a.ndim)
    blk = lambda n: pl.BlockSpec((1, L, n), lambda i, j: (i, j, 0))
    return pl.pallas_call(
        _ssd_kernel,
        grid=(bsz, s // L),
        in_specs=[blk(SSD_XBC), blk(SSD_WIDTH), blk(LANES),
                  full(cw), full(cb), full(dtb), full(alog), full(dsk), full(ng), full(tril), full(emat)],
        out_specs=blk(SSD_WIDTH),
        out_shape=jax.ShapeDtypeStruct((bsz, s, SSD_WIDTH), BF16),
        scratch_shapes=[pltpu.VMEM((SSD_HALO + L, SSD_XBC), F32),
                        pltpu.VMEM((SSD_NGROUPS, SSD_STATE, GROUP_W), F32)],
        compiler_params=_cparams(("parallel", "arbitrary")),
        name="ssd",
    )(xbc, z, dtm, cw, cb, dtb, alog, dsk, ng, tril, emat)


def _attn_kernel(q_ref, k_ref, vt_ref, o_ref, m_sc, l_sc, acc_sc, st0, st1, *, tq):
    i = pl.program_id(1)
    krow = lax.broadcasted_iota(jnp.int32, (tq, tq), 0)
    qcol = lax.broadcasted_iota(jnp.int32, (tq, tq), 1)
    causal = krow <= qcol
    ones = jnp.ones((16, tq), BF16)
    m_sc[...] = jnp.full(m_sc.shape, -jnp.inf, F32)
    l_sc[...] = jnp.zeros(l_sc.shape, F32)
    acc_sc[...] = jnp.zeros(acc_sc.shape, F32)

    def scores(j, st_ref):
        start = pl.multiple_of(j * tq, tq)
        for h in range(MLA_HEADS):
            hs = slice(h * HEAD_PAD, (h + 1) * HEAD_PAD)
            st_ref[h] = _dot_nt(k_ref[pl.ds(start, tq), hs], q_ref[:, hs])

    def update(j, st_ref, masked):
        start = pl.multiple_of(j * tq, tq)
        for h in range(MLA_HEADS):
            vs = slice(h * MLA_V, (h + 1) * MLA_V)
            st = st_ref[h]
            if masked:
                st = jnp.where(causal, st, -jnp.inf)
            m = m_sc[h:h + 1, :]
            m_new = jnp.maximum(m, jnp.max(st, axis=0, keepdims=True))
            p = jnp.exp2(st - m_new).astype(BF16)
            alpha = jnp.exp2(m - m_new)
            m_sc[h:h + 1, :] = m_new
            lhs = jnp.concatenate([vt_ref[vs, pl.ds(start, tq)], ones], axis=0)
            pv = _dot(lhs, p)
            l_sc[h:h + 1, :] = alpha * l_sc[h:h + 1, :] + pv[MLA_V:MLA_V + 1, :]
            acc_sc[vs, :] = alpha * acc_sc[vs, :] + pv[:MLA_V, :]

    def pair(jp, c):
        j0 = 2 * jp
        scores(j0 + 1, st1)
        update(j0, st0, False)
        scores(j0 + 2, st0)
        update(j0 + 1, st1, False)
        return c

    scores(0, st0)
    npairs = i // 2
    lax.fori_loop(0, npairs, pair, 0)

    @pl.when(i == 2 * npairs)
    def _():
        update(i, st0, True)

    @pl.when(i != 2 * npairs)
    def _():
        scores(i, st1)
        update(i - 1, st0, False)
        update(i, st1, True)

    outs = [acc_sc[h * MLA_V:(h + 1) * MLA_V, :] / l_sc[h:h + 1, :] for h in range(MLA_HEADS)]
    o_ref[...] = jnp.concatenate(outs, axis=0).T.astype(BF16)


def _attention(q, k, vt, bsz, tq):
    t = q.shape[0]
    s = t // bsz
    nblk = s // tq
    nq = MLA_HEADS * HEAD_PAD
    nv = MLA_HEADS * MLA_V
    return pl.pallas_call(
        functools.partial(_attn_kernel, tq=tq),
        grid=(bsz, nblk),
        in_specs=[pl.BlockSpec((tq, nq), lambda b, i: (b * nblk + i, 0)),
                  pl.BlockSpec((s, nq), lambda b, i: (b, 0)),
                  pl.BlockSpec((nv, s), lambda b, i: (0, b))],
        out_specs=pl.BlockSpec((tq, nv), lambda b, i: (b * nblk + i, 0)),
        out_shape=jax.ShapeDtypeStruct((t, nv), BF16),
        scratch_shapes=[pltpu.VMEM((8, tq), F32), pltpu.VMEM((8, tq), F32), pltpu.VMEM((nv, tq), F32),
                        pltpu.VMEM((MLA_HEADS, tq, tq), F32), pltpu.VMEM((MLA_HEADS, tq, tq), F32)],
        compiler_params=_cparams(("parallel", "parallel")),
        name="mla_attention",
    )(q, k, vt)


M_IDX0, M_IDX1, M_GATE0, M_GATE1, M_RANK0, M_RANK1 = range(6)


def _outproj_body(yc_ref, ys_ref, ym_ref, h_ref, w_ref, g_ref):
    acc = _dot(yc_ref[...], w_ref[0:CONV_WIDTH, :])
    acc = acc + _dot(ys_ref[...], w_ref[CONV_WIDTH:CONV_WIDTH + SSD_WIDTH, :])
    acc = acc + _dot(ym_ref[...], w_ref[CONV_WIDTH + SSD_WIDTH:, :])
    h1 = h_ref[...] + acc
    return h1, _rms(h1, g_ref[...])


def _outproj_dense_kernel(yc_ref, ys_ref, ym_ref, h_ref, w_ref, g_ref, h1_ref, hn_ref):
    h1, hn = _outproj_body(yc_ref, ys_ref, ym_ref, h_ref, w_ref, g_ref)
    h1_ref[...] = h1
    hn_ref[...] = hn.astype(BF16)


def _outproj_moe_kernel(yc_ref, ys_ref, ym_ref, h_ref, w_ref, g_ref, r_ref, lt_ref,
                        h1_ref, hn_ref, meta_ref, cnt_ref):
    @pl.when(pl.program_id(0) == 0)
    def _():
        cnt_ref[...] = jnp.zeros(cnt_ref.shape, F32)

    h1, hn = _outproj_body(yc_ref, ys_ref, ym_ref, h_ref, w_ref, g_ref)
    h1_ref[...] = h1
    hn_ref[...] = hn
    logits = _dot(hn.astype(BF16), r_ref[...])
    lane = lax.broadcasted_iota(jnp.int32, logits.shape, 1)
    lm = jnp.where(lane < N_EXPERTS, logits, -jnp.inf)
    m1 = jnp.max(lm, axis=-1, keepdims=True)
    i1 = jnp.min(jnp.where(lm == m1, lane, LANES), axis=-1, keepdims=True)
    lm2 = jnp.where(lane == i1, -jnp.inf, lm)
    m2 = jnp.max(lm2, axis=-1, keepdims=True)
    i2 = jnp.min(jnp.where(lm2 == m2, lane, LANES), axis=-1, keepdims=True)
    e = jnp.exp(m2 - m1)
    g1 = 1.0 / (1.0 + e)
    g2 = e / (1.0 + e)
    onehot = jnp.where((lane == i1) | (lane == i2), 1.0, 0.0)
    before = _dot(lt_ref[...], onehot.astype(BF16)) + cnt_ref[0:1, :]
    r1 = jnp.sum(jnp.where(lane == i1, before, 0.0), axis=-1, keepdims=True)
    r2 = jnp.sum(jnp.where(lane == i2, before, 0.0), axis=-1, keepdims=True)
    cnt_ref[...] = cnt_ref[...] + jnp.sum(onehot, axis=0, keepdims=True)
    meta = jnp.where(lane == M_IDX0, i1.astype(F32), 0.0)
    meta = jnp.where(lane == M_IDX1, i2.astype(F32), meta)
    meta = jnp.where(lane == M_GATE0, g1, meta)
    meta = jnp.where(lane == M_GATE1, g2, meta)
    meta = jnp.where(lane == M_RANK0, r1, meta)
    meta = jnp.where(lane == M_RANK1, r2, meta)
    meta_ref[...] = meta


def _outproj(yc, ys, ym, h, w, g, tm, router=None, ltri=None):
    t = h.shape[0]
    row = lambda n: pl.BlockSpec((tm, n), lambda i: (i, 0))
    full = lambda a: pl.BlockSpec(a.shape, lambda i: (0,) * a.ndim)
    in_specs = [row(CONV_WIDTH), row(SSD_WIDTH), row(MLA_HEADS * MLA_V), row(D_MODEL), full(w), full(g)]
    if router is None:
        return pl.pallas_call(
            _outproj_dense_kernel,
            grid=(t // tm,),
            in_specs=in_specs,
            out_specs=[row(D_MODEL), row(D_MODEL)],
            out_shape=[jax.ShapeDtypeStruct((t, D_MODEL), F32), jax.ShapeDtypeStruct((t, D_MODEL), BF16)],
            compiler_params=_cparams(("parallel",)),
            name="outproj_dense",
        )(yc, ys, ym, h, w, g)
    return pl.pallas_call(
        _outproj_moe_kernel,
        grid=(t // tm,),
        in_specs=in_specs + [full(router), full(ltri)],
        out_specs=[row(D_MODEL), row(D_MODEL), row(LANES), pl.BlockSpec((8, LANES), lambda i: (0, 0))],
        out_shape=[jax.ShapeDtypeStruct((t, D_MODEL), F32), jax.ShapeDtypeStruct((t, D_MODEL), F32),
                   jax.ShapeDtypeStruct((t, LANES), F32), jax.ShapeDtypeStruct((8, LANES), F32)],
        compiler_params=_cparams(("arbitrary",)),
        name="outproj_moe",
    )(yc, ys, ym, h, w, g, router, ltri)


def _ffn_kernel(hn_ref, h1_ref, wg_ref, wu_ref, wd_ref, o_ref):
    @pl.when(pl.program_id(1) == 0)
    def _():
        o_ref[...] = h1_ref[...]

    x = hn_ref[...]
    mid = (_silu(_dot(x, wg_ref[...])) * _dot(x, wu_ref[...])).astype(BF16)
    o_ref[...] += _dot(mid, wd_ref[...])


def _ffn_dense(hn, h1, wg, wu, wd, tm, fc):
    t = hn.shape[0]
    return pl.pallas_call(
        _ffn_kernel,
        grid=(t // tm, D_FF // fc),
        in_specs=[pl.BlockSpec((tm, D_MODEL), lambda i, f: (i, 0)),
                  pl.BlockSpec((tm, D_MODEL), lambda i, f: (i, 0)),
                  pl.BlockSpec((D_MODEL, fc), lambda i, f: (0, f)),
                  pl.BlockSpec((D_MODEL, fc), lambda i, f: (0, f)),
                  pl.BlockSpec((fc, D_MODEL), lambda i, f: (f, 0))],
        out_specs=pl.BlockSpec((tm, D_MODEL), lambda i, f: (i, 0)),
        out_shape=jax.ShapeDtypeStruct((t, D_MODEL), F32),
        compiler_params=_cparams(("parallel", "arbitrary")),
        name="ffn_dense",
    )(hn, h1, wg, wu, wd)


def _row_copy(src_ref, src_row, dst_ref, dst_row, sem):
    return pltpu.make_async_copy(src_ref.at[pl.ds(src_row, 1)], dst_ref.at[pl.ds(dst_row, 1)], sem)


def _dispatch_kernel(dest_ref, x_ref, zeros_ref, xs_ref, sem, *, tm):
    del zeros_ref

    def issue(r, c):
        _row_copy(x_ref, r, xs_ref, dest_ref[0, 0, 2 * r], sem).start()
        _row_copy(x_ref, r, xs_ref, dest_ref[0, 0, 2 * r + 1], sem).start()
        return c

    lax.fori_loop(0, tm, issue, 0)

    def drain(r, c):
        _row_copy(x_ref, 0, xs_ref, 0, sem).wait()
        return c

    lax.fori_loop(0, 2 * tm, drain, 0)


def _dispatch(dest3, hn, n_slots, tm):
    t = hn.shape[0]
    zeros = jnp.zeros((n_slots, D_MODEL), F32)
    return pl.pallas_call(
        functools.partial(_dispatch_kernel, tm=tm),
        grid=(t // tm,),
        in_specs=[pl.BlockSpec((1, 1, 2 * tm), lambda i: (i, 0, 0), memory_space=pltpu.SMEM),
                  pl.BlockSpec((tm, D_MODEL), lambda i: (i, 0)),
                  pl.BlockSpec(memory_space=pl.ANY)],
        out_specs=pl.BlockSpec(memory_space=pl.ANY),
        out_shape=jax.ShapeDtypeStruct((n_slots, D_MODEL), F32),
        scratch_shapes=[pltpu.SemaphoreType.DMA],
        input_output_aliases={2: 0},
        compiler_params=_cparams(("arbitrary",)),
        name="moe_dispatch",
    )(dest3, hn, zeros)


def _moe_ffn_kernel(texp_ref, nused_ref, x_ref, wg_ref, wu_ref, wd_ref, o_ref):
    del texp_ref

    @pl.when(pl.program_id(1) == 0)
    def _():
        o_ref[...] = jnp.zeros(o_ref.shape, F32)

    @pl.when(pl.program_id(0) < nused_ref[0])
    def _():
        x = x_ref[...].astype(BF16)
        mid = (_silu(_dot(x, wg_ref[0])) * _dot(x, wu_ref[0])).astype(BF16)
        o_ref[...] += _dot(mid, wd_ref[0])


def _moe_ffn(tile_exp, n_used, xs, wg, wu, wd, tm, fc):
    n_slots = xs.shape[0]
    nf = D_FF // fc

    def fsel(i, f, nu):
        return jnp.where(i < nu[0], f, nf - 1)

    grid_spec = pltpu.PrefetchScalarGridSpec(
        num_scalar_prefetch=2,
        grid=(n_slots // tm, nf),
        in_specs=[pl.BlockSpec((tm, D_MODEL), lambda i, f, te, nu: (i, 0)),
                  pl.BlockSpec((1, D_MODEL, fc), lambda i, f, te, nu: (te[i], 0, fsel(i, f, nu))),
                  pl.BlockSpec((1, D_MODEL, fc), lambda i, f, te, nu: (te[i], 0, fsel(i, f, nu))),
                  pl.BlockSpec((1, fc, D_MODEL), lambda i, f, te, nu: (te[i], fsel(i, f, nu), 0))],
        out_specs=pl.BlockSpec((tm, D_MODEL), lambda i, f, te, nu: (i, 0)),
    )
    return pl.pallas_call(
        _moe_ffn_kernel,
        grid_spec=grid_spec,
        out_shape=jax.ShapeDtypeStruct((n_slots, D_MODEL), F32),
        compiler_params=_cparams(("parallel", "arbitrary")),
        name="moe_ffn",
    )(tile_exp, n_used, xs, wg, wu, wd)


def _combine_kernel(dest_ref, h1_ref, meta_ref, ye_ref, o_ref, ya, yb, sem, *, tm):
    def issue(r, c):
        _row_copy(ye_ref, dest_ref[0, 0, 2 * r], ya, r, sem).start()
        _row_copy(ye_ref, dest_ref[0, 0, 2 * r + 1], yb, r, sem).start()
        return c

    lax.fori_loop(0, tm, issue, 0)

    def drain(r, c):
        _row_copy(ye_ref, 0, ya, 0, sem).wait()
        return c

    lax.fori_loop(0, 2 * tm, drain, 0)
    meta = meta_ref[...]
    g1 = meta[:, M_GATE0:M_GATE0 + 1]
    g2 = meta[:, M_GATE1:M_GATE1 + 1]
    o_ref[...] = h1_ref[...] + g1 * ya[...] + g2 * yb[...]


def _combine(dest3, h1, meta, ye, tm):
    t = h1.shape[0]
    return pl.pallas_call(
        functools.partial(_combine_kernel, tm=tm),
        grid=(t // tm,),
        in_specs=[pl.BlockSpec((1, 1, 2 * tm), lambda i: (i, 0, 0), memory_space=pltpu.SMEM),
                  pl.BlockSpec((tm, D_MODEL), lambda i: (i, 0)),
                  pl.BlockSpec((tm, LANES), lambda i: (i, 0)),
                  pl.BlockSpec(memory_space=pl.ANY)],
        out_specs=pl.BlockSpec((tm, D_MODEL), lambda i: (i, 0)),
        out_shape=jax.ShapeDtypeStruct((t, D_MODEL), F32),
        scratch_shapes=[pltpu.VMEM((tm, D_MODEL), F32), pltpu.VMEM((tm, D_MODEL), F32), pltpu.SemaphoreType.DMA],
        compiler_params=_cparams(("arbitrary",)),
        name="moe_combine",
    )(dest3, h1, meta, ye)


def _ple_kernel(h_ref, p_ref, g_ref, wg_ref, wp_ref, fg_ref, o_ref, *, final):
    h = h_ref[...]
    gate = _sigmoid(_dot(_rms(h, g_ref[...]).astype(BF16), wg_ref[...]))
    out = h + _dot(p_ref[...].astype(BF16), wp_ref[...]) * gate
    if final:
        out = _rms(out, fg_ref[...])
    o_ref[...] = out


def _ple(h, p, g, wg, wp, fg, tm, final):
    t = h.shape[0]
    row = lambda n: pl.BlockSpec((tm, n), lambda i: (i, 0))
    full = lambda a: pl.BlockSpec(a.shape, lambda i: (0,) * a.ndim)
    return pl.pallas_call(
        functools.partial(_ple_kernel, final=final),
        grid=(t // tm,),
        in_specs=[row(D_MODEL), row(PLE_DIM), full(g), full(wg), full(wp), full(fg)],
        out_specs=row(D_MODEL),
        out_shape=jax.ShapeDtypeStruct((t, D_MODEL), F32),
        compiler_params=_cparams(("parallel",)),
        name="ple",
    )(h, p, g, wg, wp, fg)


def _swap_halves(w):
    half = w.shape[-1] // 2
    return jnp.concatenate([w[..., half:], w[..., :half]], axis=-1)


def _pad_cols(w, left, total):
    return jnp.pad(w, ((0, 0), (left, total - left - w.shape[1])))


def _arrange_w_in(w):
    sizes = (512, 512, SSD_XBC, SSD_HEADS, MLA_Q_RANK, MLA_KV_RANK, MLA_ROPE)
    pts = np.cumsum(sizes)[:-1].tolist()
    w_conv, w_z, w_xbc, w_dt, w_cq, w_ckv, w_kr = jnp.split(w, pts, axis=1)
    seg_kr = _pad_cols(w_kr, ROPE_LO, LANES)
    seg_dtr = _pad_cols(w_dt, 0, LANES) + _pad_cols(_swap_halves(w_kr), ROPE_LO, LANES)
    return jnp.concatenate([w_conv, w_z, w_xbc, w_cq, w_ckv, seg_kr, seg_dtr], axis=1).astype(BF16)


def _arrange_w_uq(w):
    main, rot = [], []
    for h in range(MLA_HEADS):
        wh = w[:, h * (MLA_NOPE + MLA_ROPE):(h + 1) * (MLA_NOPE + MLA_ROPE)]
        main.append(_pad_cols(wh, 0, HEAD_PAD))
        rot.append(_pad_cols(_swap_halves(wh[:, MLA_NOPE:]), ROPE_LO, HEAD_PAD))
    return jnp.concatenate(main + rot, axis=1).astype(BF16)


def _arrange_w_ukv(w):
    ks, vs = [], []
    for h in range(MLA_HEADS):
        wh = w[:, h * (MLA_NOPE + MLA_V):(h + 1) * (MLA_NOPE + MLA_V)]
        ks.append(_pad_cols(wh[:, :MLA_NOPE], 0, HEAD_PAD))
        vs.append(wh[:, MLA_NOPE:])
    return jnp.concatenate(ks, axis=1).astype(BF16), jnp.concatenate(vs, axis=1).T.astype(BF16)


def _row(v, width=None):
    v = v.reshape(1, -1).astype(F32)
    if width is not None:
        v = jnp.pad(v, ((0, 0), (0, width - v.shape[1])))
    return v


def _pick(n, prefs):
    for c in prefs:
        if n % c == 0:
            return c
    return n


def kernel(x, p, positions, attn_norm_g, w_in, conv_dw_w, conv_dw_b, conv_ln_g, conv_ln_b, ssd_conv_w, ssd_conv_b, ssd_dt_bias, ssd_a_log, ssd_d, ssd_norm_g, mla_q_norm_g, mla_w_uq, mla_kv_norm_g, mla_w_ukv, w_out, ffn_norm_g, dense_w_gate, dense_w_up, dense_w_down, moe_router, moe_w_gate, moe_w_up, moe_w_down, ple_norm_g, ple_w_gate, ple_w_proj, final_norm_g):
    bsz, s, _ = x.shape
    t = bsz * s
    tm_row = _pick(t, (512, 256, 128))
    tm_ffn = _pick(t, (1024, 512, 256, 128))
    fc = 512
    tc = _pick(s, (256, 128))
    tm_moe = _pick(t, (512, 256, 128))
    tm_tok = _pick(t, (256, 128))
    n_slots = 2 * t + N_EXPERTS * tm_moe

    inv = ROPE_BASE ** (-jnp.arange(0, MLA_ROPE, 2, dtype=F32) / MLA_ROPE)
    inv128 = _pad_cols(jnp.concatenate([inv, inv])[None, :], ROPE_LO, LANES)
    pos128 = jnp.broadcast_to(positions.astype(F32).reshape(t, 1), (t, LANES))
    ctab, stab = _rope_tables(pos128, inv128, tm_row)
    grp = np.arange(CONV_WIDTH) // (CONV_WIDTH // CONV_GROUPS)
    gmean = jnp.asarray((grp[:, None] == grp[None, :]) / (CONV_WIDTH // CONV_GROUPS), BF16)
    tril = jnp.asarray(np.tril(np.ones((SSD_CHUNK, SSD_CHUNK))), BF16)
    hd = np.arange(SSD_WIDTH) // SSD_HEADDIM
    emat = jnp.asarray(np.arange(LANES)[:, None] == hd[None, :], BF16)
    ltri = jnp.asarray(np.tril(np.ones((tm_row, tm_row)), -1), BF16)

    h = x.reshape(t, D_MODEL)
    for i in range(DEPTH):
        u_conv, z, xbc, dtm, q, k, vt = _inproj(
            h, _row(attn_norm_g[i]), _arrange_w_in(w_in[i]), ctab, stab,
            _row(mla_q_norm_g[i]), _arrange_w_uq(mla_w_uq[i]),
            _row(mla_kv_norm_g[i]), *_arrange_w_ukv(mla_w_ukv[i]), tm_row)
        y_conv = _conformer_conv(
            u_conv.reshape(bsz, s, -1), jnp.pad(conv_dw_w[i], ((0, 1), (0, 0))), _row(conv_dw_b[i]),
            _row(conv_ln_g[i]), _row(conv_ln_b[i]), gmean, tc)
        y_ssd = _ssd(
            xbc.reshape(bsz, s, -1), z.reshape(bsz, s, -1), dtm.reshape(bsz, s, -1),
            jnp.pad(ssd_conv_w[i], ((0, 8 - SSD_CONV), (0, 0))), _row(ssd_conv_b[i]),
            _row(ssd_dt_bias[i], LANES), _row(ssd_a_log[i], LANES),
            _row(jnp.repeat(ssd_d[i], SSD_HEADDIM)), _row(ssd_norm_g[i]), tril, emat)
        ym = _attention(q, k, vt, bsz, tc)
        yc, ys = y_conv.reshape(t, -1), y_ssd.reshape(t, -1)
        wo = w_out[i].astype(BF16)
        j = i // 2
        if i % 2 == 0:
            h1, hn = _outproj(yc, ys, ym, h, wo, _row(ffn_norm_g[i]), tm_row)
            h2 = _ffn_dense(hn, h1, dense_w_gate[j].astype(BF16), dense_w_up[j].astype(BF16),
                            dense_w_down[j].astype(BF16), tm_ffn, fc)
        else:
            router = _pad_cols(moe_router[j], 0, LANES).astype(BF16)
            h1, hn, meta, cnt = _outproj(yc, ys, ym, h, wo, _row(ffn_norm_g[i]), tm_row, router, ltri)
            counts = cnt[0, :N_EXPERTS].astype(jnp.int32)
            padded = ((counts + tm_moe - 1) // tm_moe) * tm_moe
            pends = jnp.cumsum(padded)
            pstarts = pends - padded
            idx = meta[:, M_IDX0:M_IDX1 + 1].astype(jnp.int32)
            rank = meta[:, M_RANK0:M_RANK1 + 1].astype(jnp.int32)
            dest3 = (pstarts[idx] + rank).reshape(t // tm_tok, 1, 2 * tm_tok)
            n_tiles = n_slots // tm_moe
            n_used = (pends[-1] // tm_moe).astype(jnp.int32)
            tile_start = jnp.arange(n_tiles, dtype=jnp.int32) * tm_moe
            tile_exp = jnp.minimum(jnp.sum(pends[None, :] <= tile_start[:, None], axis=1), N_EXPERTS - 1)
            tile_exp = tile_exp.astype(jnp.int32)
            tile_exp = jnp.where(jnp.arange(n_tiles) < n_used, tile_exp, tile_exp[jnp.maximum(n_used - 1, 0)])
            xs = _dispatch(dest3, hn, n_slots, tm_tok)
            ye = _moe_ffn(tile_exp, n_used.reshape(1), xs, moe_w_gate[j].astype(BF16), moe_w_up[j].astype(BF16),
                          moe_w_down[j].astype(BF16), tm_moe, fc)
            h2 = _combine(dest3, h1, meta, ye, tm_tok)
        h = _ple(h2, p[i].reshape(t, PLE_DIM), _row(ple_norm_g[i]), ple_w_gate[i].astype(BF16),
                 ple_w_proj[i].astype(BF16), _row(final_norm_g), tm_row, final=(i == DEPTH - 1))
    return h.reshape(bsz, s, D_MODEL)
```

```python
import functools

import numpy as np
import jax
import jax.numpy as jnp
from jax import lax
from jax.experimental import pallas as pl
from jax.experimental.pallas import tpu as pltpu

F32 = jnp.float32
BF16 = jnp.bfloat16

D_MODEL = 1024
DEPTH = 4
PLE_DIM = 256
CONV_WIDTH = 256
CONV_GROUPS = 4
CONV_KERNEL = 31
SSD_WIDTH = 512
SSD_HEADDIM = 64
SSD_HEADS = 8
SSD_NGROUPS = 2
SSD_STATE = 128
SSD_CONV = 4
SSD_CHUNK = 128
MLA_HEADS = 4
MLA_NOPE = 64
MLA_ROPE = 32
MLA_V = 64
MLA_Q_RANK = 256
MLA_KV_RANK = 128
ROPE_BASE = 10000.0
D_FF = 3584
N_EXPERTS = 8
RMS_EPS = 1e-6
LN_EPS = 1e-5

LANES = 128
HEAD_PAD = 128
ROPE_LO = MLA_NOPE
VMEM_LIMIT = 48 * 1024 * 1024

C_CONV = 0
C_Z = 512
C_XBC = 1024
C_CQ = 2048
C_CKV = 2304
C_KR = 2432
C_DTR = 2560
IN_COLS_PAD = 2688


def _cparams(sem):
    return pltpu.CompilerParams(dimension_semantics=sem, vmem_limit_bytes=VMEM_LIMIT)


def _dot(a, b):
    return jnp.dot(a, b, preferred_element_type=F32)


def _dot_nt(a, b):
    return lax.dot_general(a, b, (((1,), (1,)), ((), ())), preferred_element_type=F32)


def _split3(a):
    a1 = a.astype(BF16)
    r1 = a - a1.astype(F32)
    a2 = r1.astype(BF16)
    a3 = (r1 - a2.astype(F32)).astype(BF16)
    return a1, a2, a3


def _dot_f32_lhs(a, m):
    a1, a2, a3 = _split3(a)
    return _dot(a1, m) + _dot(a2, m) + _dot(a3, m)


def _dot_f32_rhs(m, b):
    b1, b2, b3 = _split3(b)
    return _dot(m, b1) + _dot(m, b2) + _dot(m, b3)


def _rms(x, g, eps=RMS_EPS):
    return x * lax.rsqrt(jnp.mean(x * x, axis=-1, keepdims=True) + eps) * g


def _sigmoid(x):
    return 1.0 / (1.0 + jnp.exp(-x))


def _silu(x):
    return x * _sigmoid(x)


def _rope_kernel(pos_ref, inv_ref, c_ref, s_ref):
    ang = pos_ref[...] * inv_ref[...]
    lane = lax.broadcasted_iota(jnp.int32, ang.shape, 1)
    in_rope = (lane >= ROPE_LO) & (lane < ROPE_LO + MLA_ROPE)
    first_half = lane < ROPE_LO + MLA_ROPE // 2
    cos = jnp.cos(ang)
    sin = jnp.sin(ang)
    c_ref[...] = jnp.where(in_rope, cos, jnp.where(lane < ROPE_LO, 1.0, 0.0))
    s_ref[...] = jnp.where(in_rope, jnp.where(first_half, -sin, sin), 0.0)


def _rope_tables(pos128, inv128, tm):
    t = pos128.shape[0]
    return pl.pallas_call(
        _rope_kernel,
        grid=(t // tm,),
        in_specs=[pl.BlockSpec((tm, LANES), lambda i: (i, 0)),
                  pl.BlockSpec((1, LANES), lambda i: (0, 0))],
        out_specs=[pl.BlockSpec((tm, LANES), lambda i: (i, 0))] * 2,
        out_shape=[jax.ShapeDtypeStruct((t, LANES), F32)] * 2,
        compiler_params=_cparams(("parallel",)),
        name="rope_tables",
    )(pos128, inv128)


def _inproj_kernel(h_ref, g_ref, w_ref, c_ref, s_ref, gq_ref, wq_ref, gkv_ref, wk_ref, wvt_ref,
                   oconv_ref, oz_ref, oxbc_ref, odt_ref, oq_ref, ok_ref, ovt_ref):
    xn = _rms(h_ref[...], g_ref[...]).astype(BF16)
    oconv_ref[...] = _dot(xn, w_ref[:, C_CONV:C_Z])
    oz_ref[...] = _dot(xn, w_ref[:, C_Z:C_XBC])
    oxbc_ref[...] = _dot(xn, w_ref[:, C_XBC:C_CQ])
    cq = _dot(xn, w_ref[:, C_CQ:C_CKV])
    ckv = _dot(xn, w_ref[:, C_CKV:C_KR])
    kr = _dot(xn, w_ref[:, C_KR:C_DTR])
    dtr = _dot(xn, w_ref[:, C_DTR:IN_COLS_PAD])
    odt_ref[...] = dtr
    c = c_ref[...]
    s = s_ref[...]
    c4 = jnp.concatenate([c] * MLA_HEADS, axis=1)
    s4 = jnp.concatenate([s] * MLA_HEADS, axis=1)
    qq = _dot(_rms(cq, gq_ref[...]).astype(BF16), wq_ref[...])
    nq = MLA_HEADS * HEAD_PAD
    scale = (MLA_NOPE + MLA_ROPE) ** -0.5 * np.log2(np.e)
    oq_ref[...] = ((qq[:, :nq] * c4 + qq[:, nq:] * s4) * scale).astype(BF16)
    ckvn = _rms(ckv, gkv_ref[...]).astype(BF16)
    kpe = kr * c + dtr * s
    ok_ref[...] = (_dot(ckvn, wk_ref[...]) + jnp.concatenate([kpe] * MLA_HEADS, axis=1)).astype(BF16)
    ovt_ref[...] = _dot_nt(wvt_ref[...], ckvn).astype(BF16)


def _inproj(h, g, w, ctab, stab, gq, wq, gkv, wk, wvt, tm):
    t = h.shape[0]
    row = lambda n: pl.BlockSpec((tm, n), lambda i: (i, 0))
    full = lambda a: pl.BlockSpec(a.shape, lambda i: (0,) * a.ndim)
    nq = MLA_HEADS * HEAD_PAD
    nv = MLA_HEADS * MLA_V
    widths = (512, 512, 1024, LANES, nq, nq)
    dtypes = (F32, F32, F32, F32, BF16, BF16)
    return pl.pallas_call(
        _inproj_kernel,
        grid=(t // tm,),
        in_specs=[row(D_MODEL), full(g), full(w), row(LANES), row(LANES), full(gq), full(wq), full(gkv),
                  full(wk), full(wvt)],
        out_specs=[row(n) for n in widths] + [pl.BlockSpec((nv, tm), lambda i: (0, i))],
        out_shape=[jax.ShapeDtypeStruct((t, n), d) for n, d in zip(widths, dtypes)]
        + [jax.ShapeDtypeStruct((nv, t), BF16)],
        compiler_params=_cparams(("parallel",)),
        name="inproj",
    )(h, g, w, ctab, stab, gq, wq, gkv, wk, wvt)


CONV_HALO = 32
CONV_SUB = 64


def _conv_kernel(u_ref, w_ref, b_ref, lg_ref, lb_ref, gm_ref, o_ref, gbuf, *, tc):
    @pl.when(pl.program_id(1) == 0)
    def _():
        gbuf[0:CONV_HALO, :] = jnp.zeros((CONV_HALO, CONV_WIDTH), F32)

    u = u_ref[0]
    gbuf[CONV_HALO:CONV_HALO + tc, :] = u[:, :CONV_WIDTH] * _sigmoid(u[:, CONV_WIDTH:])
    gm = gm_ref[...]
    first = CONV_HALO - (CONV_KERNEL - 1)
    for r0 in range(0, tc, CONV_SUB):
        acc = jnp.broadcast_to(b_ref[...], (CONV_SUB, CONV_WIDTH))
        for j in range(CONV_KERNEL):
            acc = acc + w_ref[j:j + 1, :] * gbuf[first + j + r0:first + j + r0 + CONV_SUB, :]
        mu = _dot_f32_lhs(acc, gm)
        d = acc - mu
        var = _dot_f32_lhs(d * d, gm)
        hn = d * lax.rsqrt(var + LN_EPS) * lg_ref[...] + lb_ref[...]
        o_ref[0, r0:r0 + CONV_SUB, :] = _silu(hn).astype(BF16)
    gbuf[0:CONV_HALO, :] = gbuf[tc:tc + CONV_HALO, :]


def _conformer_conv(u, w, b, lg, lb, gm, tc):
    bsz, s, _ = u.shape
    full = lambda a: pl.BlockSpec(a.shape, lambda i, j: (0,) * a.ndim)
    return pl.pallas_call(
        functools.partial(_conv_kernel, tc=tc),
        grid=(bsz, s // tc),
        in_specs=[pl.BlockSpec((1, tc, 2 * CONV_WIDTH), lambda i, j: (i, j, 0)),
                  full(w), full(b), full(lg), full(lb), full(gm)],
        out_specs=pl.BlockSpec((1, tc, CONV_WIDTH), lambda i, j: (i, j, 0)),
        out_shape=jax.ShapeDtypeStruct((bsz, s, CONV_WIDTH), BF16),
        scratch_shapes=[pltpu.VMEM((CONV_HALO + tc, CONV_WIDTH), F32)],
        compiler_params=_cparams(("parallel", "arbitrary")),
        name="conformer_conv",
    )(u, w, b, lg, lb, gm)


SSD_HALO = 8
SSD_XBC = SSD_WIDTH + 2 * SSD_NGROUPS * SSD_STATE
GROUP_W = SSD_WIDTH // SSD_NGROUPS
HEADS_PER_GROUP = SSD_HEADS // SSD_NGROUPS


def _ssd_kernel(xbc_ref, z_ref, dt_ref, cw_ref, cb_ref, dtb_ref, alog_ref, dsk_ref, ng_ref, tril_ref, exp_ref,
                o_ref, cbuf, state):
    L = SSD_CHUNK

    @pl.when(pl.program_id(1) == 0)
    def _():
        cbuf[0:SSD_HALO, :] = jnp.zeros((SSD_HALO, SSD_XBC), F32)
        state[...] = jnp.zeros(state.shape, F32)

    cbuf[SSD_HALO:SSD_HALO + L, :] = xbc_ref[0]
    first = SSD_HALO - (SSD_CONV - 1)
    acc = jnp.broadcast_to(cb_ref[...], (L, SSD_XBC))
    for j in range(SSD_CONV):
        acc = acc + cw_ref[j:j + 1, :] * cbuf[first + j:first + j + L, :]
    cbuf[0:SSD_HALO, :] = cbuf[L:L + SSD_HALO, :]
    xc = _silu(acc)
    xs = xc[:, :SSD_WIDTH]
    bm = xc[:, SSD_WIDTH:SSD_WIDTH + SSD_NGROUPS * SSD_STATE]
    cm = xc[:, SSD_WIDTH + SSD_NGROUPS * SSD_STATE:]

    lane = lax.broadcasted_iota(jnp.int32, (1, LANES), 1)
    v = dt_ref[0] + dtb_ref[...]
    dt = jnp.maximum(v, 0.0) + jnp.log1p(jnp.exp(-jnp.abs(v)))
    a = jnp.where(lane < SSD_HEADS, -jnp.exp(alog_ref[...]), 0.0)
    cs = _dot_f32_rhs(tril_ref[...], dt * a)
    cs_t = cs.T
    cs_last = cs[L - 1:L, :]
    emat = exp_ref[...]
    dt_e = _dot_f32_lhs(dt, emat)
    ecs_e = _dot_f32_lhs(jnp.exp(cs), emat)
    ds_e = _dot_f32_lhs(jnp.exp(cs_last - cs), emat)
    cd_e = _dot_f32_lhs(jnp.broadcast_to(jnp.exp(cs_last), (8, LANES)), emat)[0:1, :]

    xd = xs * dt_e
    xdb = xd.astype(BF16)
    xds = (xd * ds_e).astype(BF16)
    rows = lax.broadcasted_iota(jnp.int32, (L, L), 0)
    cols = lax.broadcasted_iota(jnp.int32, (L, L), 1)
    causal = rows >= cols
    ys = []
    for g in range(SSD_NGROUPS):
        cmg = cm[:, g * SSD_STATE:(g + 1) * SSD_STATE].astype(BF16)
        bmg = bm[:, g * SSD_STATE:(g + 1) * SSD_STATE]
        cbm = _dot_nt(cmg, bmg.astype(BF16))
        yd = []
        for r in range(HEADS_PER_GROUP):
            h = g * HEADS_PER_GROUP + r
            seg = cs[:, h:h + 1] - cs_t[h:h + 1, :]
            dec = jnp.exp(jnp.where(causal, seg, -jnp.inf))
            mix = (cbm * dec).astype(BF16)
            yd.append(_dot(mix, xdb[:, h * SSD_HEADDIM:(h + 1) * SSD_HEADDIM]))
        gs = slice(g * GROUP_W, (g + 1) * GROUP_W)
        prev = state[g]
        y_off = _dot(cmg, prev.astype(BF16)) * ecs_e[:, gs]
        st_new = _dot(bmg.T.astype(BF16), xds[:, gs])
        state[g] = prev * cd_e[:, gs] + st_new
        ys.append(jnp.concatenate(yd, axis=1) + y_off)
    y = jnp.concatenate(ys, axis=1) + dsk_ref[...] * xs
    yg = y * _silu(z_ref[0])
    outs = []
    for g in range(SSD_NGROUPS):
        ygg = yg[:, g * GROUP_W:(g + 1) * GROUP_W]
        outs.append(ygg * lax.rsqrt(jnp.mean(ygg * ygg, axis=-1, keepdims=True) + RMS_EPS))
    o_ref[0] = (jnp.concatenate(outs, axis=1) * ng_ref[...]).astype(BF16)


def _ssd(xbc, z, dtm, cw, cb, dtb, alog, dsk, ng, tril, emat):
    bsz, s, _ = xbc.shape
    L = SSD_CHUNK
    full = lambda a: pl.BlockSpec(a.shape, lambda i, j: (0,) * a.ndim)
    blk = lambda n: pl.BlockSpec((1, L, n), lambda i, j: (i, j, 0))
    return pl.pallas_call(
        _ssd_kernel,
        grid=(bsz, s // L),
        in_specs=[blk(SSD_XBC), blk(SSD_WIDTH), blk(LANES),
                  full(cw), full(cb), full(dtb), full(alog), full(dsk), full(ng), full(tril), full(emat)],
        out_specs=blk(SSD_WIDTH),
        out_shape=jax.ShapeDtypeStruct((bsz, s, SSD_WIDTH), BF16),
        scratch_shapes=[pltpu.VMEM((SSD_HALO + L, SSD_XBC), F32),
                        pltpu.VMEM((SSD_NGROUPS, SSD_STATE, GROUP_W), F32)],
        compiler_params=_cparams(("parallel", "arbitrary")),
        name="ssd",
    )(xbc, z, dtm, cw, cb, dtb, alog, dsk, ng, tril, emat)


def _attn_kernel(q_ref, k_ref, vt_ref, o_ref, m_sc, l_sc, acc_sc, st0, st1, *, tq):
    i = pl.program_id(1)
    krow = lax.broadcasted_iota(jnp.int32, (tq, tq), 0)
    qcol = lax.broadcasted_iota(jnp.int32, (tq, tq), 1)
    causal = krow <= qcol
    ones = jnp.ones((16, tq), BF16)
    m_sc[...] = jnp.full(m_sc.shape, -jnp.inf, F32)
    l_sc[...] = jnp.zeros(l_sc.shape, F32)
    acc_sc[...] = jnp.zeros(acc_sc.shape, F32)

    def scores(j, st_ref):
        start = pl.multiple_of(j * tq, tq)
        for h in range(MLA_HEADS):
            hs = slice(h * HEAD_PAD, (h + 1) * HEAD_PAD)
            st_ref[h] = _dot_nt(k_ref[pl.ds(start, tq), hs], q_ref[:, hs])

    def update(j, st_ref, masked):
        start = pl.multiple_of(j * tq, tq)
        for h in range(MLA_HEADS):
            vs = slice(h * MLA_V, (h + 1) * MLA_V)
            st = st_ref[h]
            if masked:
                st = jnp.where(causal, st, -jnp.inf)
            m = m_sc[h:h + 1, :]
            m_new = jnp.maximum(m, jnp.max(st, axis=0, keepdims=True))
            p = jnp.exp2(st - m_new).astype(BF16)
            alpha = jnp.exp2(m - m_new)
            m_sc[h:h + 1, :] = m_new
            lhs = jnp.concatenate([vt_ref[vs, pl.ds(start, tq)], ones], axis=0)
            pv = _dot(lhs, p)
            l_sc[h:h + 1, :] = alpha * l_sc[h:h + 1, :] + pv[MLA_V:MLA_V + 1, :]
            acc_sc[vs, :] = alpha * acc_sc[vs, :] + pv[:MLA_V, :]

    def pair(jp, c):
        j0 = 2 * jp
        scores(j0 + 1, st1)
        update(j0, st0, False)
        scores(j0 + 2, st0)
        update(j0 + 1, st1, False)
        return c

    scores(0, st0)
    npairs = i // 2
    lax.fori_loop(0, npairs, pair, 0)

    @pl.when(i == 2 * npairs)
    def _():
        update(i, st0, True)

    @pl.when(i != 2 * npairs)
    def _():
        scores(i, st1)
        update(i - 1, st0, False)
        update(i, st1, True)

    outs = [acc_sc[h * MLA_V:(h + 1) * MLA_V, :] / l_sc[h:h + 1, :] for h in range(MLA_HEADS)]
    o_ref[...] = jnp.concatenate(outs, axis=0).T.astype(BF16)


def _attention(q, k, vt, bsz, tq):
    t = q.shape[0]
    s = t // bsz
    nblk = s // tq
    nq = MLA_HEADS * HEAD_PAD
    nv = MLA_HEADS * MLA_V
    return pl.pallas_call(
        functools.partial(_attn_kernel, tq=tq),
        grid=(bsz, nblk),
        in_specs=[pl.BlockSpec((tq, nq), lambda b, i: (b * nblk + i, 0)),
                  pl.BlockSpec((s, nq), lambda b, i: (b, 0)),
                  pl.BlockSpec((nv, s), lambda b, i: (0, b))],
        out_specs=pl.BlockSpec((tq, nv), lambda b, i: (b * nblk + i, 0)),
        out_shape=jax.ShapeDtypeStruct((t, nv), BF16),
        scratch_shapes=[pltpu.VMEM((8, tq), F32), pltpu.VMEM((8, tq), F32), pltpu.VMEM((nv, tq), F32),
                        pltpu.VMEM((MLA_HEADS, tq, tq), F32), pltpu.VMEM((MLA_HEADS, tq, tq), F32)],
        compiler_params=_cparams(("parallel", "parallel")),
        name="mla_attention",
    )(q, k, vt)


M_IDX0, M_IDX1, M_GATE0, M_GATE1 = range(4)


def _outproj_body(yc_ref, ys_ref, ym_ref, h_ref, w_ref, g_ref):
    acc = _dot(yc_ref[...], w_ref[0:CONV_WIDTH, :])
    acc = acc + _dot(ys_ref[...], w_ref[CONV_WIDTH:CONV_WIDTH + SSD_WIDTH, :])
    acc = acc + _dot(ym_ref[...], w_ref[CONV_WIDTH + SSD_WIDTH:, :])
    h1 = h_ref[...] + acc
    return h1, _rms(h1, g_ref[...])


def _outproj_dense_kernel(yc_ref, ys_ref, ym_ref, h_ref, w_ref, g_ref, h1_ref, hn_ref):
    h1, hn = _outproj_body(yc_ref, ys_ref, ym_ref, h_ref, w_ref, g_ref)
    h1_ref[...] = h1
    hn_ref[...] = hn.astype(BF16)


def _outproj_moe_kernel(yc_ref, ys_ref, ym_ref, h_ref, w_ref, g_ref, r_ref,
                        h1_ref, hn_ref, meta_ref, tcnt_ref, cnt_ref):
    @pl.when(pl.program_id(0) == 0)
    def _():
        cnt_ref[...] = jnp.zeros(cnt_ref.shape, F32)

    h1, hn = _outproj_body(yc_ref, ys_ref, ym_ref, h_ref, w_ref, g_ref)
    h1_ref[...] = h1
    hnb = hn.astype(BF16)
    hn_ref[...] = hnb
    logits = _dot(hnb, r_ref[...])
    lane = lax.broadcasted_iota(jnp.int32, logits.shape, 1)
    lm = jnp.where(lane < N_EXPERTS, logits, -jnp.inf)
    m1 = jnp.max(lm, axis=-1, keepdims=True)
    i1 = jnp.min(jnp.where(lm == m1, lane, LANES), axis=-1, keepdims=True)
    lm2 = jnp.where(lane == i1, -jnp.inf, lm)
    m2 = jnp.max(lm2, axis=-1, keepdims=True)
    i2 = jnp.min(jnp.where(lm2 == m2, lane, LANES), axis=-1, keepdims=True)
    e = jnp.exp(m2 - m1)
    g1 = 1.0 / (1.0 + e)
    g2 = e / (1.0 + e)
    onehot = jnp.where((lane == i1) | (lane == i2), 1.0, 0.0)
    tcnt_ref[0] = cnt_ref[...]
    cnt_ref[...] = cnt_ref[...] + jnp.sum(onehot, axis=0, keepdims=True)
    meta = jnp.where(lane == M_IDX0, i1.astype(F32), 0.0)
    meta = jnp.where(lane == M_IDX1, i2.astype(F32), meta)
    meta = jnp.where(lane == M_GATE0, g1, meta)
    meta = jnp.where(lane == M_GATE1, g2, meta)
    meta_ref[...] = meta


def _outproj(yc, ys, ym, h, w, g, tm, router=None):
    t = h.shape[0]
    row = lambda n: pl.BlockSpec((tm, n), lambda i: (i, 0))
    full = lambda a: pl.BlockSpec(a.shape, lambda i: (0,) * a.ndim)
    in_specs = [row(CONV_WIDTH), row(SSD_WIDTH), row(MLA_HEADS * MLA_V), row(D_MODEL), full(w), full(g)]
    if router is None:
        return pl.pallas_call(
            _outproj_dense_kernel,
            grid=(t // tm,),
            in_specs=in_specs,
            out_specs=[row(D_MODEL), row(D_MODEL)],
            out_shape=[jax.ShapeDtypeStruct((t, D_MODEL), F32), jax.ShapeDtypeStruct((t, D_MODEL), BF16)],
            compiler_params=_cparams(("parallel",)),
            name="outproj_dense",
        )(yc, ys, ym, h, w, g)
    return pl.pallas_call(
        _outproj_moe_kernel,
        grid=(t // tm,),
        in_specs=in_specs + [full(router)],
        out_specs=[row(D_MODEL), row(D_MODEL), row(LANES),
                   pl.BlockSpec((1, 8, LANES), lambda i: (i, 0, 0)), pl.BlockSpec((8, LANES), lambda i: (0, 0))],
        out_shape=[jax.ShapeDtypeStruct((t, D_MODEL), F32), jax.ShapeDtypeStruct((t, D_MODEL), BF16),
                   jax.ShapeDtypeStruct((t, LANES), F32), jax.ShapeDtypeStruct((t // tm, 8, LANES), F32),
                   jax.ShapeDtypeStruct((8, LANES), F32)],
        compiler_params=_cparams(("arbitrary",)),
        name="outproj_moe",
    )(yc, ys, ym, h, w, g, router)


def _ffn_kernel(hn_ref, h1_ref, wg_ref, wu_ref, wd_ref, o_ref):
    @pl.when(pl.program_id(1) == 0)
    def _():
        o_ref[...] = h1_ref[...]

    x = hn_ref[...]
    mid = (_silu(_dot(x, wg_ref[...])) * _dot(x, wu_ref[...])).astype(BF16)
    o_ref[...] += _dot(mid, wd_ref[...])


def _ffn_dense(hn, h1, wg, wu, wd, tm, fc):
    t = hn.shape[0]
    return pl.pallas_call(
        _ffn_kernel,
        grid=(t // tm, D_FF // fc),
        in_specs=[pl.BlockSpec((tm, D_MODEL), lambda i, f: (i, 0)),
                  pl.BlockSpec((tm, D_MODEL), lambda i, f: (i, 0)),
                  pl.BlockSpec((D_MODEL, fc), lambda i, f: (0, f)),
                  pl.BlockSpec((D_MODEL, fc), lambda i, f: (0, f)),
                  pl.BlockSpec((fc, D_MODEL), lambda i, f: (f, 0))],
        out_specs=pl.BlockSpec((tm, D_MODEL), lambda i, f: (i, 0)),
        out_shape=jax.ShapeDtypeStruct((t, D_MODEL), F32),
        compiler_params=_cparams(("parallel", "arbitrary")),
        name="ffn_dense",
    )(hn, h1, wg, wu, wd)


ROW_TILES = D_MODEL // LANES


def _rows_to_tiles(ref_view, val):
    for c in range(ROW_TILES):
        ref_view[:, c, :] = val[:, c * LANES:(c + 1) * LANES]


def _tiles_to_rows(ref_view):
    return jnp.concatenate([ref_view[:, c, :] for c in range(ROW_TILES)], axis=1)


def _run_bits(tm):
    return [1 << b for b in range(tm.bit_length() - 1, -1, -1)]


def _run_copies(n, src_ref, src0, dst_ref, dst0, sem, tm, wait):
    off = 0
    for b in _run_bits(tm):
        part = n & b

        @pl.when(part != 0)
        def _(off=off, b=b):
            cp = pltpu.make_async_copy(src_ref.at[pl.ds(src0 + off, b)], dst_ref.at[pl.ds(dst0 + off, b)], sem)
            if wait:
                cp.wait()
            else:
                cp.start()

        off = off + part


def _sorted_positions(meta, ltri, upper):
    lane = lax.broadcasted_iota(jnp.int32, meta.shape, 1)
    i1 = meta[:, M_IDX0:M_IDX0 + 1].astype(jnp.int32)
    i2 = meta[:, M_IDX1:M_IDX1 + 1].astype(jnp.int32)
    onehot = jnp.where((lane == i1) | (lane == i2), 1.0, 0.0)
    before = _dot(ltri, onehot.astype(BF16))
    n = jnp.broadcast_to(jnp.sum(onehot, axis=0, keepdims=True), (8, LANES))
    loff = _dot_f32_lhs(n, upper)[0:1, :]
    pos = before + loff
    q1 = jnp.sum(jnp.where(lane == i1, pos, 0.0), axis=-1, keepdims=True)
    q2 = jnp.sum(jnp.where(lane == i2, pos, 0.0), axis=-1, keepdims=True)
    return q1, q2


def _dispatch_kernel(n_ref, lo_ref, g_ref, x_ref, meta_ref, ltri_ref, up_ref, zeros_ref, xs_ref, gs_ref,
                     xbuf, sems, *, tm):
    del zeros_ref
    i = pl.program_id(0)
    nt = pl.num_programs(0)
    slot = i % 2
    meta = meta_ref[...]
    q1, q2 = _sorted_positions(meta, ltri_ref[...], up_ref[...])
    lane = lax.broadcasted_iota(jnp.int32, meta.shape, 1)
    qmat = jnp.where(lane == 0, q1, jnp.where(lane == 1, q2, 0.0))
    qt = qmat.T
    srow = lax.broadcasted_iota(jnp.int32, (2 * tm, tm), 0).astype(F32)
    p1 = jnp.where(srow == qt[0:1, :], 1.0, 0.0).astype(BF16)
    p2 = jnp.where(srow == qt[1:2, :], 1.0, 0.0).astype(BF16)
    g1 = jnp.broadcast_to(meta[:, M_GATE0:M_GATE0 + 1], (tm, LANES))
    g2 = jnp.broadcast_to(meta[:, M_GATE1:M_GATE1 + 1], (tm, LANES))
    gs_ref[...] = _dot_f32_rhs(p1, g1) + _dot_f32_rhs(p2, g2)
    _rows_to_tiles(xbuf.at[slot], _dot(p1 + p2, x_ref[...]))

    def runs(tile, sl, wait):
        for e in range(N_EXPERTS):
            k = tile * N_EXPERTS + e
            _run_copies(n_ref[k], xbuf.at[sl], lo_ref[k], xs_ref, g_ref[k], sems.at[sl], tm, wait)

    runs(i, slot, False)

    @pl.when(i > 0)
    def _():
        runs(i - 1, 1 - slot, True)

    @pl.when(i == nt - 1)
    def _():
        runs(i, slot, True)


def _dispatch(n_run, lo_run, g_run, hn, meta, ltri, upper, n_slots, tm):
    t = hn.shape[0]
    zeros = jnp.zeros((n_slots, ROW_TILES, LANES), F32)
    full = lambda a: pl.BlockSpec(a.shape, lambda i, *_: (0,) * a.ndim)
    grid_spec = pltpu.PrefetchScalarGridSpec(
        num_scalar_prefetch=3,
        grid=(t // tm,),
        in_specs=[pl.BlockSpec((tm, D_MODEL), lambda i, *_: (i, 0)),
                  pl.BlockSpec((tm, LANES), lambda i, *_: (i, 0)),
                  full(ltri), full(upper),
                  pl.BlockSpec(memory_space=pl.ANY)],
        out_specs=[pl.BlockSpec(memory_space=pl.ANY), pl.BlockSpec((2 * tm, LANES), lambda i, *_: (i, 0))],
        scratch_shapes=[pltpu.VMEM((2, 2 * tm, ROW_TILES, LANES), F32), pltpu.SemaphoreType.DMA((2,))],
    )
    return pl.pallas_call(
        functools.partial(_dispatch_kernel, tm=tm),
        grid_spec=grid_spec,
        out_shape=[jax.ShapeDtypeStruct((n_slots, ROW_TILES, LANES), F32),
                   jax.ShapeDtypeStruct((2 * t, LANES), F32)],
        input_output_aliases={7: 0},
        compiler_params=_cparams(("arbitrary",)),
        name="moe_dispatch",
    )(n_run, lo_run, g_run, hn, meta, ltri, upper, zeros)


def _moe_ffn_kernel(texp_ref, nused_ref, x_ref, wg_ref, wu_ref, wd_ref, o_ref, xb, acc):
    del texp_ref
    f = pl.program_id(1)

    @pl.when(f == 0)
    def _():
        xb[...] = _tiles_to_rows(x_ref).astype(BF16)
        acc[...] = jnp.zeros(acc.shape, F32)

    @pl.when(pl.program_id(0) < nused_ref[0])
    def _():
        x = xb[...]
        mid = (_silu(_dot(x, wg_ref[0])) * _dot(x, wu_ref[0])).astype(BF16)
        acc[...] += _dot(mid, wd_ref[0])

    @pl.when(f == pl.num_programs(1) - 1)
    def _():
        _rows_to_tiles(o_ref, acc[...])


def _moe_ffn(tile_exp, n_used, xs, wg, wu, wd, tm, fc):
    n_slots = xs.shape[0]
    nf = D_FF // fc

    def fsel(i, f, nu):
        return jnp.where(i < nu[0], f, nf - 1)

    grid_spec = pltpu.PrefetchScalarGridSpec(
        num_scalar_prefetch=2,
        grid=(n_slots // tm, nf),
        in_specs=[pl.BlockSpec((tm, ROW_TILES, LANES), lambda i, f, te, nu: (i, 0, 0)),
                  pl.BlockSpec((1, D_MODEL, fc), lambda i, f, te, nu: (te[i], 0, fsel(i, f, nu))),
                  pl.BlockSpec((1, D_MODEL, fc), lambda i, f, te, nu: (te[i], 0, fsel(i, f, nu))),
                  pl.BlockSpec((1, fc, D_MODEL), lambda i, f, te, nu: (te[i], fsel(i, f, nu), 0))],
        out_specs=pl.BlockSpec((tm, ROW_TILES, LANES), lambda i, f, te, nu: (i, 0, 0)),
        scratch_shapes=[pltpu.VMEM((tm, D_MODEL), BF16), pltpu.VMEM((tm, D_MODEL), F32)],
    )
    return pl.pallas_call(
        _moe_ffn_kernel,
        grid_spec=grid_spec,
        out_shape=jax.ShapeDtypeStruct((n_slots, ROW_TILES, LANES), F32),
        compiler_params=_cparams(("parallel", "arbitrary")),
        name="moe_ffn",
    )(tile_exp, n_used, xs, wg, wu, wd)


def _combine_kernel(n_ref, lo_ref, g_ref, h1_ref, meta_ref, gs_ref, ltri_ref, up_ref, ye_ref, o_ref,
                    ybuf, sems, *, tm):
    i = pl.program_id(0)
    nt = pl.num_programs(0)
    slot = i % 2

    def runs(tile, sl, wait):
        for e in range(N_EXPERTS):
            k = tile * N_EXPERTS + e
            _run_copies(n_ref[k], ye_ref, g_ref[k], ybuf.at[sl], lo_ref[k], sems.at[sl], tm, wait)

    @pl.when(i == 0)
    def _():
        runs(i, slot, False)

    @pl.when(i + 1 < nt)
    def _():
        runs(i + 1, 1 - slot, False)

    q1, q2 = _sorted_positions(meta_ref[...], ltri_ref[...], up_ref[...])
    scol = lax.broadcasted_iota(jnp.int32, (tm, 2 * tm), 1).astype(F32)
    sel = jnp.where((scol == q1) | (scol == q2), 1.0, 0.0).astype(BF16)
    runs(i, slot, True)
    y = _tiles_to_rows(ybuf.at[slot]) * gs_ref[:, 0:1]
    hi = y.astype(BF16)
    lo = (y - hi.astype(F32)).astype(BF16)
    o_ref[...] = h1_ref[...] + _dot(sel, hi) + _dot(sel, lo)


def _combine(n_run, lo_run, g_run, h1, meta, gs, ltri, upper, ye, tm):
    t = h1.shape[0]
    full = lambda a: pl.BlockSpec(a.shape, lambda i, *_: (0,) * a.ndim)
    grid_spec = pltpu.PrefetchScalarGridSpec(
        num_scalar_prefetch=3,
        grid=(t // tm,),
        in_specs=[pl.BlockSpec((tm, D_MODEL), lambda i, *_: (i, 0)),
                  pl.BlockSpec((tm, LANES), lambda i, *_: (i, 0)),
                  pl.BlockSpec((2 * tm, LANES), lambda i, *_: (i, 0)),
                  full(ltri), full(upper),
                  pl.BlockSpec(memory_space=pl.ANY)],
        out_specs=pl.BlockSpec((tm, D_MODEL), lambda i, *_: (i, 0)),
        scratch_shapes=[pltpu.VMEM((2, 2 * tm, ROW_TILES, LANES), F32), pltpu.SemaphoreType.DMA((2,))],
    )
    return pl.pallas_call(
        functools.partial(_combine_kernel, tm=tm),
        grid_spec=grid_spec,
        out_shape=jax.ShapeDtypeStruct((t, D_MODEL), F32),
        compiler_params=_cparams(("arbitrary",)),
        name="moe_combine",
    )(n_run, lo_run, g_run, h1, meta, gs, ltri, upper, ye)


def _ple_kernel(h_ref, p_ref, g_ref, wg_ref, wp_ref, fg_ref, o_ref, *, final):
    h = h_ref[...]
    gate = _sigmoid(_dot(_rms(h, g_ref[...]).astype(BF16), wg_ref[...]))
    out = h + _dot(p_ref[...].astype(BF16), wp_ref[...]) * gate
    if final:
        out = _rms(out, fg_ref[...])
    o_ref[...] = out


def _ple(h, p, g, wg, wp, fg, tm, final):
    t = h.shape[0]
    row = lambda n: pl.BlockSpec((tm, n), lambda i: (i, 0))
    full = lambda a: pl.BlockSpec(a.shape, lambda i: (0,) * a.ndim)
    return pl.pallas_call(
        functools.partial(_ple_kernel, final=final),
        grid=(t // tm,),
        in_specs=[row(D_MODEL), row(PLE_DIM), full(g), full(wg), full(wp), full(fg)],
        out_specs=row(D_MODEL),
        out_shape=jax.ShapeDtypeStruct((t, D_MODEL), F32),
        compiler_params=_cparams(("parallel",)),
        name="ple",
    )(h, p, g, wg, wp, fg)


def _swap_halves(w):
    half = w.shape[-1] // 2
    return jnp.concatenate([w[..., half:], w[..., :half]], axis=-1)


def _pad_cols(w, left, total):
    return jnp.pad(w, ((0, 0), (left, total - left - w.shape[1])))


def _arrange_w_in(w):
    sizes = (512, 512, SSD_XBC, SSD_HEADS, MLA_Q_RANK, MLA_KV_RANK, MLA_ROPE)
    pts = np.cumsum(sizes)[:-1].tolist()
    w_conv, w_z, w_xbc, w_dt, w_cq, w_ckv, w_kr = jnp.split(w, pts, axis=1)
    seg_kr = _pad_cols(w_kr, ROPE_LO, LANES)
    seg_dtr = _pad_cols(w_dt, 0, LANES) + _pad_cols(_swap_halves(w_kr), ROPE_LO, LANES)
    return jnp.concatenate([w_conv, w_z, w_xbc, w_cq, w_ckv, seg_kr, seg_dtr], axis=1).astype(BF16)


def _arrange_w_uq(w):
    main, rot = [], []
    for h in range(MLA_HEADS):
        wh = w[:, h * (MLA_NOPE + MLA_ROPE):(h + 1) * (MLA_NOPE + MLA_ROPE)]
        main.append(_pad_cols(wh, 0, HEAD_PAD))
        rot.append(_pad_cols(_swap_halves(wh[:, MLA_NOPE:]), ROPE_LO, HEAD_PAD))
    return jnp.concatenate(main + rot, axis=1).astype(BF16)


def _arrange_w_ukv(w):
    ks, vs = [], []
    for h in range(MLA_HEADS):
        wh = w[:, h * (MLA_NOPE + MLA_V):(h + 1) * (MLA_NOPE + MLA_V)]
        ks.append(_pad_cols(wh[:, :MLA_NOPE], 0, HEAD_PAD))
        vs.append(wh[:, MLA_NOPE:])
    return jnp.concatenate(ks, axis=1).astype(BF16), jnp.concatenate(vs, axis=1).T.astype(BF16)


def _row(v, width=None):
    v = v.reshape(1, -1).astype(F32)
    if width is not None:
        v = jnp.pad(v, ((0, 0), (0, width - v.shape[1])))
    return v


def _pick(n, prefs):
    for c in prefs:
        if n % c == 0:
            return c
    return n


def kernel(x, p, positions, attn_norm_g, w_in, conv_dw_w, conv_dw_b, conv_ln_g, conv_ln_b, ssd_conv_w, ssd_conv_b, ssd_dt_bias, ssd_a_log, ssd_d, ssd_norm_g, mla_q_norm_g, mla_w_uq, mla_kv_norm_g, mla_w_ukv, w_out, ffn_norm_g, dense_w_gate, dense_w_up, dense_w_down, moe_router, moe_w_gate, moe_w_up, moe_w_down, ple_norm_g, ple_w_gate, ple_w_proj, final_norm_g):
    bsz, s, _ = x.shape
    t = bsz * s
    tm_row = _pick(t, (512, 256, 128))
    tm_ffn = _pick(t, (1024, 512, 256, 128))
    fc = 512
    tc = _pick(s, (256, 128))
    tm_moe = _pick(t, (512, 256, 128))
    tm_tok = _pick(t, (256, 128))
    n_slots = 2 * t + N_EXPERTS * tm_moe

    inv = ROPE_BASE ** (-jnp.arange(0, MLA_ROPE, 2, dtype=F32) / MLA_ROPE)
    inv128 = _pad_cols(jnp.concatenate([inv, inv])[None, :], ROPE_LO, LANES)
    pos128 = jnp.broadcast_to(positions.astype(F32).reshape(t, 1), (t, LANES))
    ctab, stab = _rope_tables(pos128, inv128, tm_row)
    grp = np.arange(CONV_WIDTH) // (CONV_WIDTH // CONV_GROUPS)
    gmean = jnp.asarray((grp[:, None] == grp[None, :]) / (CONV_WIDTH // CONV_GROUPS), BF16)
    tril = jnp.asarray(np.tril(np.ones((SSD_CHUNK, SSD_CHUNK))), BF16)
    hd = np.arange(SSD_WIDTH) // SSD_HEADDIM
    emat = jnp.asarray(np.arange(LANES)[:, None] == hd[None, :], BF16)
    ltri = jnp.asarray(np.tril(np.ones((tm_tok, tm_tok)), -1), BF16)
    upper = jnp.asarray(np.triu(np.ones((LANES, LANES)), 1), BF16)

    h = x.reshape(t, D_MODEL)
    for i in range(DEPTH):
        u_conv, z, xbc, dtm, q, k, vt = _inproj(
            h, _row(attn_norm_g[i]), _arrange_w_in(w_in[i]), ctab, stab,
            _row(mla_q_norm_g[i]), _arrange_w_uq(mla_w_uq[i]),
            _row(mla_kv_norm_g[i]), *_arrange_w_ukv(mla_w_ukv[i]), tm_row)
        y_conv = _conformer_conv(
            u_conv.reshape(bsz, s, -1), jnp.pad(conv_dw_w[i], ((0, 1), (0, 0))), _row(conv_dw_b[i]),
            _row(conv_ln_g[i]), _row(conv_ln_b[i]), gmean, tc)
        y_ssd = _ssd(
            xbc.reshape(bsz, s, -1), z.reshape(bsz, s, -1), dtm.reshape(bsz, s, -1),
            jnp.pad(ssd_conv_w[i], ((0, 8 - SSD_CONV), (0, 0))), _row(ssd_conv_b[i]),
            _row(ssd_dt_bias[i], LANES), _row(ssd_a_log[i], LANES),
            _row(jnp.repeat(ssd_d[i], SSD_HEADDIM)), _row(ssd_norm_g[i]), tril, emat)
        ym = _attention(q, k, vt, bsz, tc)
        yc, ys = y_conv.reshape(t, -1), y_ssd.reshape(t, -1)
        wo = w_out[i].astype(BF16)
        j = i // 2
        if i % 2 == 0:
            h1, hn = _outproj(yc, ys, ym, h, wo, _row(ffn_norm_g[i]), tm_row)
            h2 = _ffn_dense(hn, h1, dense_w_gate[j].astype(BF16), dense_w_up[j].astype(BF16),
                            dense_w_down[j].astype(BF16), tm_ffn, fc)
        else:
            router = _pad_cols(moe_router[j], 0, LANES).astype(BF16)
            h1, hn, meta, tcnt, cnt = _outproj(yc, ys, ym, h, wo, _row(ffn_norm_g[i]), tm_tok, router)
            counts = cnt[0, :N_EXPERTS].astype(jnp.int32)
            padded = ((counts + tm_moe - 1) // tm_moe) * tm_moe
            pends = jnp.cumsum(padded)
            pstarts = pends - padded
            before = tcnt[:, 0, :N_EXPERTS].astype(jnp.int32)
            n_run = jnp.concatenate([before[1:], counts[None, :]], axis=0) - before
            lo_run = jnp.cumsum(n_run, axis=1) - n_run
            g_run = pstarts[None, :] + before
            runs = (n_run.reshape(-1), lo_run.reshape(-1), g_run.reshape(-1))
            n_tiles = n_slots // tm_moe
            n_used = (pends[-1] // tm_moe).astype(jnp.int32)
            tile_start = jnp.arange(n_tiles, dtype=jnp.int32) * tm_moe
            tile_exp = jnp.minimum(jnp.sum(pends[None, :] <= tile_start[:, None], axis=1), N_EXPERTS - 1)
            tile_exp = tile_exp.astype(jnp.int32)
            tile_exp = jnp.where(jnp.arange(n_tiles) < n_used, tile_exp, tile_exp[jnp.maximum(n_used - 1, 0)])
            xs, gs = _dispatch(*runs, hn, meta, ltri, upper, n_slots, tm_tok)
            ye = _moe_ffn(tile_exp, n_used.reshape(1), xs, moe_w_gate[j].astype(BF16), moe_w_up[j].astype(BF16),
                          moe_w_down[j].astype(BF16), tm_moe, fc)
            h2 = _combine(*runs, h1, meta, gs, ltri, upper, ye, tm_tok)
        h = _ple(h2, p[i].reshape(t, PLE_DIM), _row(ple_norm_g[i]), ple_w_gate[i].astype(BF16),
                 ple_w_proj[i].astype(BF16), _row(final_norm_g), tm_row, final=(i == DEPTH - 1))
    return h.reshape(bsz, s, D_MODEL)
```

```python
import functools

import numpy as np
import jax
import jax.numpy as jnp
from jax import lax
from jax.experimental import pallas as pl
from jax.experimental.pallas import tpu as pltpu

F32 = jnp.float32
BF16 = jnp.bfloat16

D_MODEL = 1024
DEPTH = 4
PLE_DIM = 256
CONV_WIDTH = 256
CONV_GROUPS = 4
CONV_KERNEL = 31
SSD_WIDTH = 512
SSD_HEADDIM = 64
SSD_HEADS = 8
SSD_NGROUPS = 2
SSD_STATE = 128
SSD_CONV = 4
SSD_CHUNK = 128
MLA_HEADS = 4
MLA_NOPE = 64
MLA_ROPE = 32
MLA_V = 64
MLA_Q_RANK = 256
MLA_KV_RANK = 128
ROPE_BASE = 10000.0
D_FF = 3584
N_EXPERTS = 8
RMS_EPS = 1e-6
LN_EPS = 1e-5

LANES = 128
HEAD_PAD = 128
ROPE_LO = MLA_NOPE
VMEM_LIMIT = 48 * 1024 * 1024

C_CONV = 0
C_Z = 512
C_XBC = 1024
C_CQ = 2048
C_CKV = 2304
C_KR = 2432
C_DTR = 2560
IN_COLS_PAD = 2688


def _cparams(sem):
    return pltpu.CompilerParams(dimension_semantics=sem, vmem_limit_bytes=VMEM_LIMIT)


def _dot(a, b):
    return jnp.dot(a, b, preferred_element_type=F32)


def _dot_nt(a, b):
    return lax.dot_general(a, b, (((1,), (1,)), ((), ())), preferred_element_type=F32)


def _split3(a):
    a1 = a.astype(BF16)
    r1 = a - a1.astype(F32)
    a2 = r1.astype(BF16)
    a3 = (r1 - a2.astype(F32)).astype(BF16)
    return a1, a2, a3


def _dot_f32_lhs(a, m):
    a1, a2, a3 = _split3(a)
    return _dot(a1, m) + _dot(a2, m) + _dot(a3, m)


def _dot_f32_rhs(m, b):
    b1, b2, b3 = _split3(b)
    return _dot(m, b1) + _dot(m, b2) + _dot(m, b3)


def _rms(x, g, eps=RMS_EPS):
    return x * lax.rsqrt(jnp.mean(x * x, axis=-1, keepdims=True) + eps) * g


def _sigmoid(x):
    return 1.0 / (1.0 + jnp.exp(-x))


def _silu(x):
    return x * _sigmoid(x)


def _rope_kernel(pos_ref, inv_ref, c_ref, s_ref):
    ang = pos_ref[...] * inv_ref[...]
    lane = lax.broadcasted_iota(jnp.int32, ang.shape, 1)
    in_rope = (lane >= ROPE_LO) & (lane < ROPE_LO + MLA_ROPE)
    first_half = lane < ROPE_LO + MLA_ROPE // 2
    cos = jnp.cos(ang)
    sin = jnp.sin(ang)
    c_ref[...] = jnp.where(in_rope, cos, jnp.where(lane < ROPE_LO, 1.0, 0.0))
    s_ref[...] = jnp.where(in_rope, jnp.where(first_half, -sin, sin), 0.0)


def _rope_tables(pos128, inv128, tm):
    t = pos128.shape[0]
    return pl.pallas_call(
        _rope_kernel,
        grid=(t // tm,),
        in_specs=[pl.BlockSpec((tm, LANES), lambda i: (i, 0)),
                  pl.BlockSpec((1, LANES), lambda i: (0, 0))],
        out_specs=[pl.BlockSpec((tm, LANES), lambda i: (i, 0))] * 2,
        out_shape=[jax.ShapeDtypeStruct((t, LANES), F32)] * 2,
        compiler_params=_cparams(("parallel",)),
        name="rope_tables",
    )(pos128, inv128)


def _inproj_kernel(h_ref, g_ref, w_ref, c_ref, s_ref, gq_ref, wq_ref, gkv_ref, wk_ref, wvt_ref,
                   oconv_ref, oz_ref, oxbc_ref, odt_ref, oq_ref, ok_ref, ovt_ref):
    xn = _rms(h_ref[...], g_ref[...]).astype(BF16)
    oconv_ref[...] = _dot(xn, w_ref[:, C_CONV:C_Z])
    oz_ref[...] = _dot(xn, w_ref[:, C_Z:C_XBC])
    oxbc_ref[...] = _dot(xn, w_ref[:, C_XBC:C_CQ])
    cq = _dot(xn, w_ref[:, C_CQ:C_CKV])
    ckv = _dot(xn, w_ref[:, C_CKV:C_KR])
    kr = _dot(xn, w_ref[:, C_KR:C_DTR])
    dtr = _dot(xn, w_ref[:, C_DTR:IN_COLS_PAD])
    odt_ref[...] = dtr
    c = c_ref[...]
    s = s_ref[...]
    c4 = jnp.concatenate([c] * MLA_HEADS, axis=1)
    s4 = jnp.concatenate([s] * MLA_HEADS, axis=1)
    qq = _dot(_rms(cq, gq_ref[...]).astype(BF16), wq_ref[...])
    nq = MLA_HEADS * HEAD_PAD
    scale = (MLA_NOPE + MLA_ROPE) ** -0.5 * np.log2(np.e)
    oq_ref[...] = ((qq[:, :nq] * c4 + qq[:, nq:] * s4) * scale).astype(BF16)
    ckvn = _rms(ckv, gkv_ref[...]).astype(BF16)
    kpe = kr * c + dtr * s
    ok_ref[...] = (_dot(ckvn, wk_ref[...]) + jnp.concatenate([kpe] * MLA_HEADS, axis=1)).astype(BF16)
    ovt_ref[...] = _dot_nt(wvt_ref[...], ckvn).astype(BF16)


def _inproj(h, g, w, ctab, stab, gq, wq, gkv, wk, wvt, tm):
    t = h.shape[0]
    row = lambda n: pl.BlockSpec((tm, n), lambda i: (i, 0))
    full = lambda a: pl.BlockSpec(a.shape, lambda i: (0,) * a.ndim)
    nq = MLA_HEADS * HEAD_PAD
    nv = MLA_HEADS * MLA_V
    widths = (512, 512, 1024, LANES, nq, nq)
    dtypes = (F32, F32, F32, F32, BF16, BF16)
    return pl.pallas_call(
        _inproj_kernel,
        grid=(t // tm,),
        in_specs=[row(D_MODEL), full(g), full(w), row(LANES), row(LANES), full(gq), full(wq), full(gkv),
                  full(wk), full(wvt)],
        out_specs=[row(n) for n in widths] + [pl.BlockSpec((nv, tm), lambda i: (0, i))],
        out_shape=[jax.ShapeDtypeStruct((t, n), d) for n, d in zip(widths, dtypes)]
        + [jax.ShapeDtypeStruct((nv, t), BF16)],
        compiler_params=_cparams(("parallel",)),
        name="inproj",
    )(h, g, w, ctab, stab, gq, wq, gkv, wk, wvt)


CONV_HALO = 32
CONV_SUB = 64


def _conv_kernel(u_ref, w_ref, b_ref, lg_ref, lb_ref, gm_ref, o_ref, gbuf, shifted, *, tc):
    @pl.when(pl.program_id(1) == 0)
    def _():
        gbuf[0:CONV_HALO, :] = jnp.zeros((CONV_HALO, CONV_WIDTH), F32)

    u = u_ref[0]
    gbuf[CONV_HALO:CONV_HALO + tc, :] = u[:, :CONV_WIDTH] * _sigmoid(u[:, CONV_WIDTH:])
    gm = gm_ref[...]
    first = CONV_HALO - (CONV_KERNEL - 1)
    span = CONV_HALO + tc - 8
    for s in range(1, 8):
        shifted[s - 1, 0:span, :] = gbuf[s:s + span, :]
    for r0 in range(0, tc, CONV_SUB):
        acc = jnp.broadcast_to(b_ref[...], (CONV_SUB, CONV_WIDTH))
        for j in range(CONV_KERNEL):
            start = first + j + r0
            s, a = start % 8, start - start % 8
            assert a + CONV_SUB <= span or s == 0
            win = gbuf[a:a + CONV_SUB, :] if s == 0 else shifted[s - 1, a:a + CONV_SUB, :]
            acc = acc + w_ref[j:j + 1, :] * win
        mu = _dot_f32_lhs(acc, gm)
        d = acc - mu
        var = _dot_f32_lhs(d * d, gm)
        hn = d * lax.rsqrt(var + LN_EPS) * lg_ref[...] + lb_ref[...]
        o_ref[0, r0:r0 + CONV_SUB, :] = _silu(hn).astype(BF16)
    gbuf[0:CONV_HALO, :] = gbuf[tc:tc + CONV_HALO, :]


def _conformer_conv(u, w, b, lg, lb, gm, tc):
    bsz, s, _ = u.shape
    full = lambda a: pl.BlockSpec(a.shape, lambda i, j: (0,) * a.ndim)
    return pl.pallas_call(
        functools.partial(_conv_kernel, tc=tc),
        grid=(bsz, s // tc),
        in_specs=[pl.BlockSpec((1, tc, 2 * CONV_WIDTH), lambda i, j: (i, j, 0)),
                  full(w), full(b), full(lg), full(lb), full(gm)],
        out_specs=pl.BlockSpec((1, tc, CONV_WIDTH), lambda i, j: (i, j, 0)),
        out_shape=jax.ShapeDtypeStruct((bsz, s, CONV_WIDTH), BF16),
        scratch_shapes=[pltpu.VMEM((CONV_HALO + tc, CONV_WIDTH), F32),
                        pltpu.VMEM((7, CONV_HALO + tc, CONV_WIDTH), F32)],
        compiler_params=_cparams(("parallel", "arbitrary")),
        name="conformer_conv",
    )(u, w, b, lg, lb, gm)


SSD_HALO = 8
SSD_XBC = SSD_WIDTH + 2 * SSD_NGROUPS * SSD_STATE
GROUP_W = SSD_WIDTH // SSD_NGROUPS
HEADS_PER_GROUP = SSD_HEADS // SSD_NGROUPS


def _ssd_chunk(r0, cbuf, z_ref, dt_ref, cw_ref, cb_ref, dtb_ref, alog_ref, dsk_ref, ng_ref, tril_ref, exp_ref,
               o_ref, state):
    L = SSD_CHUNK
    first = SSD_HALO - (SSD_CONV - 1) + r0
    acc = jnp.broadcast_to(cb_ref[...], (L, SSD_XBC))
    for j in range(SSD_CONV):
        acc = acc + cw_ref[j:j + 1, :] * cbuf[first + j:first + j + L, :]
    xc = _silu(acc)
    xs = xc[:, :SSD_WIDTH]
    bm = xc[:, SSD_WIDTH:SSD_WIDTH + SSD_NGROUPS * SSD_STATE]
    cm = xc[:, SSD_WIDTH + SSD_NGROUPS * SSD_STATE:]

    lane = lax.broadcasted_iota(jnp.int32, (1, LANES), 1)
    v = dt_ref[0, r0:r0 + L, :] + dtb_ref[...]
    dt = jnp.maximum(v, 0.0) + jnp.log1p(jnp.exp(-jnp.abs(v)))
    a = jnp.where(lane < SSD_HEADS, -jnp.exp(alog_ref[...]), 0.0)
    cs = _dot_f32_rhs(tril_ref[...], dt * a)
    cs_t = cs.T
    cs_last = cs[L - 1:L, :]
    emat = exp_ref[...]
    dt_e = _dot_f32_lhs(dt, emat)
    ecs_e = _dot_f32_lhs(jnp.exp(cs), emat)
    ds_e = _dot_f32_lhs(jnp.exp(cs_last - cs), emat)
    cd_e = _dot_f32_lhs(jnp.broadcast_to(jnp.exp(cs_last), (8, LANES)), emat)[0:1, :]

    xd = xs * dt_e
    xdb = xd.astype(BF16)
    xds = (xd * ds_e).astype(BF16)
    rows = lax.broadcasted_iota(jnp.int32, (L, L), 0)
    cols = lax.broadcasted_iota(jnp.int32, (L, L), 1)
    causal = rows >= cols
    ys = []
    for g in range(SSD_NGROUPS):
        cmg = cm[:, g * SSD_STATE:(g + 1) * SSD_STATE].astype(BF16)
        bmg = bm[:, g * SSD_STATE:(g + 1) * SSD_STATE]
        cbm = _dot_nt(cmg, bmg.astype(BF16))
        yd = []
        for r in range(HEADS_PER_GROUP):
            h = g * HEADS_PER_GROUP + r
            seg = cs[:, h:h + 1] - cs_t[h:h + 1, :]
            dec = jnp.exp(jnp.where(causal, seg, -jnp.inf))
            mix = (cbm * dec).astype(BF16)
            yd.append(_dot(mix, xdb[:, h * SSD_HEADDIM:(h + 1) * SSD_HEADDIM]))
        gs = slice(g * GROUP_W, (g + 1) * GROUP_W)
        prev = state[g]
        y_off = _dot(cmg, prev.astype(BF16)) * ecs_e[:, gs]
        st_new = _dot(bmg.T.astype(BF16), xds[:, gs])
        state[g] = prev * cd_e[:, gs] + st_new
        ys.append(jnp.concatenate(yd, axis=1) + y_off)
    y = jnp.concatenate(ys, axis=1) + dsk_ref[...] * xs
    yg = y * _silu(z_ref[0, r0:r0 + L, :])
    outs = []
    for g in range(SSD_NGROUPS):
        ygg = yg[:, g * GROUP_W:(g + 1) * GROUP_W]
        outs.append(ygg * lax.rsqrt(jnp.mean(ygg * ygg, axis=-1, keepdims=True) + RMS_EPS))
    o_ref[0, r0:r0 + L, :] = (jnp.concatenate(outs, axis=1) * ng_ref[...]).astype(BF16)


def _ssd_kernel(xbc_ref, z_ref, dt_ref, cw_ref, cb_ref, dtb_ref, alog_ref, dsk_ref, ng_ref, tril_ref, exp_ref,
                o_ref, cbuf, state, *, rows):
    @pl.when(pl.program_id(1) == 0)
    def _():
        cbuf[0:SSD_HALO, :] = jnp.zeros((SSD_HALO, SSD_XBC), F32)
        state[...] = jnp.zeros(state.shape, F32)

    cbuf[SSD_HALO:SSD_HALO + rows, :] = xbc_ref[0]
    for r0 in range(0, rows, SSD_CHUNK):
        _ssd_chunk(r0, cbuf, z_ref, dt_ref, cw_ref, cb_ref, dtb_ref, alog_ref, dsk_ref, ng_ref, tril_ref, exp_ref,
                   o_ref, state)
    cbuf[0:SSD_HALO, :] = cbuf[rows:rows + SSD_HALO, :]


def _ssd(xbc, z, dtm, cw, cb, dtb, alog, dsk, ng, tril, emat, rows):
    bsz, s, _ = xbc.shape
    full = lambda a: pl.BlockSpec(a.shape, lambda i, j: (0,) * a.ndim)
    blk = lambda n: pl.BlockSpec((1, rows, n), lambda i, j: (i, j, 0))
    return pl.pallas_call(
        functools.partial(_ssd_kernel, rows=rows),
        grid=(bsz, s // rows),
        in_specs=[blk(SSD_XBC), blk(SSD_WIDTH), blk(LANES),
                  full(cw), full(cb), full(dtb), full(alog), full(dsk), full(ng), full(tril), full(emat)],
        out_specs=blk(SSD_WIDTH),
        out_shape=jax.ShapeDtypeStruct((bsz, s, SSD_WIDTH), BF16),
        scratch_shapes=[pltpu.VMEM((SSD_HALO + rows, SSD_XBC), F32),
                        pltpu.VMEM((SSD_NGROUPS, SSD_STATE, GROUP_W), F32)],
        compiler_params=_cparams(("parallel", "arbitrary")),
        name="ssd",
    )(xbc, z, dtm, cw, cb, dtb, alog, dsk, ng, tril, emat)


def _attn_kernel(q_ref, k_ref, vt_ref, o_ref, m_sc, l_sc, acc_sc, st0, st1, *, tq):
    i = pl.program_id(1)
    krow = lax.broadcasted_iota(jnp.int32, (tq, tq), 0)
    qcol = lax.broadcasted_iota(jnp.int32, (tq, tq), 1)
    causal = krow <= qcol
    ones = jnp.ones((16, tq), BF16)
    m_sc[...] = jnp.full(m_sc.shape, -jnp.inf, F32)
    l_sc[...] = jnp.zeros(l_sc.shape, F32)
    acc_sc[...] = jnp.zeros(acc_sc.shape, F32)

    def scores(j, st_ref):
        start = pl.multiple_of(j * tq, tq)
        for h in range(MLA_HEADS):
            hs = slice(h * HEAD_PAD, (h + 1) * HEAD_PAD)
            st_ref[h] = _dot_nt(k_ref[pl.ds(start, tq), hs], q_ref[:, hs])

    def update(j, st_ref, masked):
        start = pl.multiple_of(j * tq, tq)
        for h in range(MLA_HEADS):
            vs = slice(h * MLA_V, (h + 1) * MLA_V)
            st = st_ref[h]
            if masked:
                st = jnp.where(causal, st, -jnp.inf)
            m = m_sc[h:h + 1, :]
            m_new = jnp.maximum(m, jnp.max(st, axis=0, keepdims=True))
            p = jnp.exp2(st - m_new).astype(BF16)
            alpha = jnp.exp2(m - m_new)
            m_sc[h:h + 1, :] = m_new
            lhs = jnp.concatenate([vt_ref[vs, pl.ds(start, tq)], ones], axis=0)
            pv = _dot(lhs, p)
            l_sc[h:h + 1, :] = alpha * l_sc[h:h + 1, :] + pv[MLA_V:MLA_V + 1, :]
            acc_sc[vs, :] = alpha * acc_sc[vs, :] + pv[:MLA_V, :]

    def pair(jp, c):
        j0 = 2 * jp
        scores(j0 + 1, st1)
        update(j0, st0, False)
        scores(j0 + 2, st0)
        update(j0 + 1, st1, False)
        return c

    scores(0, st0)
    npairs = i // 2
    lax.fori_loop(0, npairs, pair, 0)

    @pl.when(i == 2 * npairs)
    def _():
        update(i, st0, True)

    @pl.when(i != 2 * npairs)
    def _():
        scores(i, st1)
        update(i - 1, st0, False)
        update(i, st1, True)

    outs = [acc_sc[h * MLA_V:(h + 1) * MLA_V, :] / l_sc[h:h + 1, :] for h in range(MLA_HEADS)]
    o_ref[...] = jnp.concatenate(outs, axis=0).T.astype(BF16)


def _attention(q, k, vt, bsz, tq):
    t = q.shape[0]
    s = t // bsz
    nblk = s // tq
    nq = MLA_HEADS * HEAD_PAD
    nv = MLA_HEADS * MLA_V
    return pl.pallas_call(
        functools.partial(_attn_kernel, tq=tq),
        grid=(bsz, nblk),
        in_specs=[pl.BlockSpec((tq, nq), lambda b, i: (b * nblk + i, 0)),
                  pl.BlockSpec((s, nq), lambda b, i: (b, 0)),
                  pl.BlockSpec((nv, s), lambda b, i: (0, b))],
        out_specs=pl.BlockSpec((tq, nv), lambda b, i: (b * nblk + i, 0)),
        out_shape=jax.ShapeDtypeStruct((t, nv), BF16),
        scratch_shapes=[pltpu.VMEM((8, tq), F32), pltpu.VMEM((8, tq), F32), pltpu.VMEM((nv, tq), F32),
                        pltpu.VMEM((MLA_HEADS, tq, tq), F32), pltpu.VMEM((MLA_HEADS, tq, tq), F32)],
        compiler_params=_cparams(("parallel", "parallel")),
        name="mla_attention",
    )(q, k, vt)


M_IDX0, M_IDX1, M_GATE0, M_GATE1 = range(4)


def _outproj_body(yc_ref, ys_ref, ym_ref, h_ref, w_ref, g_ref):
    acc = _dot(yc_ref[...], w_ref[0:CONV_WIDTH, :])
    acc = acc + _dot(ys_ref[...], w_ref[CONV_WIDTH:CONV_WIDTH + SSD_WIDTH, :])
    acc = acc + _dot(ym_ref[...], w_ref[CONV_WIDTH + SSD_WIDTH:, :])
    h1 = h_ref[...] + acc
    return h1, _rms(h1, g_ref[...])


def _outproj_dense_kernel(yc_ref, ys_ref, ym_ref, h_ref, w_ref, g_ref, h1_ref, hn_ref):
    h1, hn = _outproj_body(yc_ref, ys_ref, ym_ref, h_ref, w_ref, g_ref)
    h1_ref[...] = h1
    hn_ref[...] = hn.astype(BF16)


def _outproj_moe_kernel(yc_ref, ys_ref, ym_ref, h_ref, w_ref, g_ref, r_ref,
                        h1_ref, hn_ref, meta_ref, tcnt_ref, cnt_ref):
    @pl.when(pl.program_id(0) == 0)
    def _():
        cnt_ref[...] = jnp.zeros(cnt_ref.shape, F32)

    h1, hn = _outproj_body(yc_ref, ys_ref, ym_ref, h_ref, w_ref, g_ref)
    h1_ref[...] = h1
    hnb = hn.astype(BF16)
    hn_ref[...] = hnb
    logits = _dot(hnb, r_ref[...])
    lane = lax.broadcasted_iota(jnp.int32, logits.shape, 1)
    lm = jnp.where(lane < N_EXPERTS, logits, -jnp.inf)
    m1 = jnp.max(lm, axis=-1, keepdims=True)
    i1 = jnp.min(jnp.where(lm == m1, lane, LANES), axis=-1, keepdims=True)
    lm2 = jnp.where(lane == i1, -jnp.inf, lm)
    m2 = jnp.max(lm2, axis=-1, keepdims=True)
    i2 = jnp.min(jnp.where(lm2 == m2, lane, LANES), axis=-1, keepdims=True)
    e = jnp.exp(m2 - m1)
    g1 = 1.0 / (1.0 + e)
    g2 = e / (1.0 + e)
    onehot = jnp.where((lane == i1) | (lane == i2), 1.0, 0.0)
    tcnt_ref[0] = cnt_ref[...]
    cnt_ref[...] = cnt_ref[...] + jnp.sum(onehot, axis=0, keepdims=True)
    meta = jnp.where(lane == M_IDX0, i1.astype(F32), 0.0)
    meta = jnp.where(lane == M_IDX1, i2.astype(F32), meta)
    meta = jnp.where(lane == M_GATE0, g1, meta)
    meta = jnp.where(lane == M_GATE1, g2, meta)
    meta_ref[...] = meta


def _outproj(yc, ys, ym, h, w, g, tm, router=None):
    t = h.shape[0]
    row = lambda n: pl.BlockSpec((tm, n), lambda i: (i, 0))
    full = lambda a: pl.BlockSpec(a.shape, lambda i: (0,) * a.ndim)
    in_specs = [row(CONV_WIDTH), row(SSD_WIDTH), row(MLA_HEADS * MLA_V), row(D_MODEL), full(w), full(g)]
    if router is None:
        return pl.pallas_call(
            _outproj_dense_kernel,
            grid=(t // tm,),
            in_specs=in_specs,
            out_specs=[row(D_MODEL), row(D_MODEL)],
            out_shape=[jax.ShapeDtypeStruct((t, D_MODEL), F32), jax.ShapeDtypeStruct((t, D_MODEL), BF16)],
            compiler_params=_cparams(("parallel",)),
            name="outproj_dense",
        )(yc, ys, ym, h, w, g)
    return pl.pallas_call(
        _outproj_moe_kernel,
        grid=(t // tm,),
        in_specs=in_specs + [full(router)],
        out_specs=[row(D_MODEL), row(D_MODEL), row(LANES),
                   pl.BlockSpec((1, 8, LANES), lambda i: (i, 0, 0)), pl.BlockSpec((8, LANES), lambda i: (0, 0))],
        out_shape=[jax.ShapeDtypeStruct((t, D_MODEL), F32), jax.ShapeDtypeStruct((t, D_MODEL), BF16),
                   jax.ShapeDtypeStruct((t, LANES), F32), jax.ShapeDtypeStruct((t // tm, 8, LANES), F32),
                   jax.ShapeDtypeStruct((8, LANES), F32)],
        compiler_params=_cparams(("arbitrary",)),
        name="outproj_moe",
    )(yc, ys, ym, h, w, g, router)


def _ffn_kernel(hn_ref, h1_ref, wg_ref, wu_ref, wd_ref, o_ref):
    @pl.when(pl.program_id(1) == 0)
    def _():
        o_ref[...] = h1_ref[...]

    x = hn_ref[...]
    mid = (_silu(_dot(x, wg_ref[0].astype(BF16))) * _dot(x, wu_ref[0].astype(BF16))).astype(BF16)
    o_ref[...] += _dot(mid, wd_ref[0].astype(BF16))


def _ffn_dense(hn, h1, wg, wu, wd, layer, tm, fc):
    t = hn.shape[0]
    return pl.pallas_call(
        _ffn_kernel,
        grid=(t // tm, D_FF // fc),
        in_specs=[pl.BlockSpec((tm, D_MODEL), lambda i, f: (i, 0)),
                  pl.BlockSpec((tm, D_MODEL), lambda i, f: (i, 0)),
                  pl.BlockSpec((1, D_MODEL, fc), lambda i, f: (layer, 0, f)),
                  pl.BlockSpec((1, D_MODEL, fc), lambda i, f: (layer, 0, f)),
                  pl.BlockSpec((1, fc, D_MODEL), lambda i, f: (layer, f, 0))],
        out_specs=pl.BlockSpec((tm, D_MODEL), lambda i, f: (i, 0)),
        out_shape=jax.ShapeDtypeStruct((t, D_MODEL), F32),
        compiler_params=_cparams(("parallel", "arbitrary")),
        name="ffn_dense",
    )(hn, h1, wg, wu, wd)


ROW_TILES = D_MODEL // LANES


def _rows_to_tiles(val):
    blocks = jnp.stack([val[:, c * LANES:(c + 1) * LANES] for c in range(ROW_TILES)], axis=0)
    return pltpu.einshape("crl->rcl", blocks)


def _tiles_to_rows(val):
    blocks = pltpu.einshape("rcl->crl", val)
    return jnp.concatenate([blocks[c] for c in range(ROW_TILES)], axis=1)


def _run_bits(tm):
    return [1 << b for b in range(tm.bit_length() - 1, -1, -1)]


def _run_copies(n, src_ref, src0, dst_ref, dst0, sem, tm, wait):
    off = 0
    for b in _run_bits(tm):
        part = n & b

        @pl.when(part != 0)
        def _(off=off, b=b):
            cp = pltpu.make_async_copy(src_ref.at[pl.ds(src0 + off, b)], dst_ref.at[pl.ds(dst0 + off, b)], sem)
            if wait:
                cp.wait()
            else:
                cp.start()

        off = off + part


def _sorted_positions(meta, ltri, upper):
    lane = lax.broadcasted_iota(jnp.int32, meta.shape, 1)
    i1 = meta[:, M_IDX0:M_IDX0 + 1].astype(jnp.int32)
    i2 = meta[:, M_IDX1:M_IDX1 + 1].astype(jnp.int32)
    onehot = jnp.where((lane == i1) | (lane == i2), 1.0, 0.0)
    before = _dot(ltri, onehot.astype(BF16))
    n = jnp.broadcast_to(jnp.sum(onehot, axis=0, keepdims=True), (8, LANES))
    loff = _dot_f32_lhs(n, upper)[0:1, :]
    pos = before + loff
    q1 = jnp.sum(jnp.where(lane == i1, pos, 0.0), axis=-1, keepdims=True)
    q2 = jnp.sum(jnp.where(lane == i2, pos, 0.0), axis=-1, keepdims=True)
    return q1, q2


def _dispatch_kernel(n_ref, lo_ref, g_ref, x_ref, meta_ref, ltri_ref, up_ref, zeros_ref, xs_ref, gs_ref,
                     xbuf, sems, *, tm):
    del zeros_ref
    i = pl.program_id(0)
    nt = pl.num_programs(0)
    slot = i % 2
    meta = meta_ref[...]
    q1, q2 = _sorted_positions(meta, ltri_ref[...], up_ref[...])
    lane = lax.broadcasted_iota(jnp.int32, meta.shape, 1)
    qmat = jnp.where(lane == 0, q1, jnp.where(lane == 1, q2, 0.0))
    qt = qmat.T
    srow = lax.broadcasted_iota(jnp.int32, (2 * tm, tm), 0).astype(F32)
    p1 = jnp.where(srow == qt[0:1, :], 1.0, 0.0).astype(BF16)
    p2 = jnp.where(srow == qt[1:2, :], 1.0, 0.0).astype(BF16)
    g1 = jnp.broadcast_to(meta[:, M_GATE0:M_GATE0 + 1], (tm, LANES))
    g2 = jnp.broadcast_to(meta[:, M_GATE1:M_GATE1 + 1], (tm, LANES))
    gs_ref[...] = _dot_f32_rhs(p1, g1) + _dot_f32_rhs(p2, g2)
    xbuf[slot] = _rows_to_tiles(_dot(p1 + p2, x_ref[...]))

    def runs(tile, sl, wait):
        for e in range(N_EXPERTS):
            k = tile * N_EXPERTS + e
            _run_copies(n_ref[k], xbuf.at[sl], lo_ref[k], xs_ref, g_ref[k], sems.at[sl], tm, wait)

    runs(i, slot, False)

    @pl.when(i > 0)
    def _():
        runs(i - 1, 1 - slot, True)

    @pl.when(i == nt - 1)
    def _():
        runs(i, slot, True)


def _dispatch(n_run, lo_run, g_run, hn, meta, ltri, upper, n_slots, tm):
    t = hn.shape[0]
    zeros = jnp.zeros((n_slots, ROW_TILES, LANES), F32)
    full = lambda a: pl.BlockSpec(a.shape, lambda i, *_: (0,) * a.ndim)
    grid_spec = pltpu.PrefetchScalarGridSpec(
        num_scalar_prefetch=3,
        grid=(t // tm,),
        in_specs=[pl.BlockSpec((tm, D_MODEL), lambda i, *_: (i, 0)),
                  pl.BlockSpec((tm, LANES), lambda i, *_: (i, 0)),
                  full(ltri), full(upper),
                  pl.BlockSpec(memory_space=pl.ANY)],
        out_specs=[pl.BlockSpec(memory_space=pl.ANY), pl.BlockSpec((2 * tm, LANES), lambda i, *_: (i, 0))],
        scratch_shapes=[pltpu.VMEM((2, 2 * tm, ROW_TILES, LANES), F32), pltpu.SemaphoreType.DMA((2,))],
    )
    return pl.pallas_call(
        functools.partial(_dispatch_kernel, tm=tm),
        grid_spec=grid_spec,
        out_shape=[jax.ShapeDtypeStruct((n_slots, ROW_TILES, LANES), F32),
                   jax.ShapeDtypeStruct((2 * t, LANES), F32)],
        input_output_aliases={7: 0},
        compiler_params=_cparams(("arbitrary",)),
        name="moe_dispatch",
    )(n_run, lo_run, g_run, hn, meta, ltri, upper, zeros)


def _moe_ffn_kernel(texp_ref, nused_ref, x_ref, wg_ref, wu_ref, wd_ref, o_ref, xb, acc):
    del texp_ref
    f = pl.program_id(1)

    @pl.when(f == 0)
    def _():
        xb[...] = _tiles_to_rows(x_ref[...]).astype(BF16)
        acc[...] = jnp.zeros(acc.shape, F32)

    @pl.when(pl.program_id(0) < nused_ref[0])
    def _():
        x = xb[...]
        mid = (_silu(_dot(x, wg_ref[0, 0].astype(BF16))) * _dot(x, wu_ref[0, 0].astype(BF16))).astype(BF16)
        acc[...] += _dot(mid, wd_ref[0, 0].astype(BF16))

    @pl.when(f == pl.num_programs(1) - 1)
    def _():
        o_ref[...] = _rows_to_tiles(acc[...])


def _moe_ffn(tile_exp, n_used, xs, wg, wu, wd, layer, tm, fc):
    n_slots = xs.shape[0]
    nf = D_FF // fc

    def fsel(i, f, nu):
        return jnp.where(i < nu[0], f, nf - 1)

    grid_spec = pltpu.PrefetchScalarGridSpec(
        num_scalar_prefetch=2,
        grid=(n_slots // tm, nf),
        in_specs=[pl.BlockSpec((tm, ROW_TILES, LANES), lambda i, f, te, nu: (i, 0, 0)),
                  pl.BlockSpec((1, 1, D_MODEL, fc), lambda i, f, te, nu: (layer, te[i], 0, fsel(i, f, nu))),
                  pl.BlockSpec((1, 1, D_MODEL, fc), lambda i, f, te, nu: (layer, te[i], 0, fsel(i, f, nu))),
                  pl.BlockSpec((1, 1, fc, D_MODEL), lambda i, f, te, nu: (layer, te[i], fsel(i, f, nu), 0))],
        out_specs=pl.BlockSpec((tm, ROW_TILES, LANES), lambda i, f, te, nu: (i, 0, 0)),
        scratch_shapes=[pltpu.VMEM((tm, D_MODEL), BF16), pltpu.VMEM((tm, D_MODEL), F32)],
    )
    return pl.pallas_call(
        _moe_ffn_kernel,
        grid_spec=grid_spec,
        out_shape=jax.ShapeDtypeStruct((n_slots, ROW_TILES, LANES), F32),
        compiler_params=_cparams(("parallel", "arbitrary")),
        name="moe_ffn",
    )(tile_exp, n_used, xs, wg, wu, wd)


def _combine_kernel(n_ref, lo_ref, g_ref, h1_ref, meta_ref, gs_ref, ltri_ref, up_ref, ye_ref, o_ref,
                    ybuf, sems, *, tm):
    i = pl.program_id(0)
    nt = pl.num_programs(0)
    slot = i % 2

    def runs(tile, sl, wait):
        for e in range(N_EXPERTS):
            k = tile * N_EXPERTS + e
            _run_copies(n_ref[k], ye_ref, g_ref[k], ybuf.at[sl], lo_ref[k], sems.at[sl], tm, wait)

    @pl.when(i == 0)
    def _():
        runs(i, slot, False)

    @pl.when(i + 1 < nt)
    def _():
        runs(i + 1, 1 - slot, False)

    q1, q2 = _sorted_positions(meta_ref[...], ltri_ref[...], up_ref[...])
    scol = lax.broadcasted_iota(jnp.int32, (tm, 2 * tm), 1).astype(F32)
    sel = jnp.where((scol == q1) | (scol == q2), 1.0, 0.0).astype(BF16)
    runs(i, slot, True)
    y = _tiles_to_rows(ybuf[slot]) * gs_ref[:, 0:1]
    hi = y.astype(BF16)
    lo = (y - hi.astype(F32)).astype(BF16)
    o_ref[...] = h1_ref[...] + _dot(sel, hi) + _dot(sel, lo)


def _combine(n_run, lo_run, g_run, h1, meta, gs, ltri, upper, ye, tm):
    t = h1.shape[0]
    full = lambda a: pl.BlockSpec(a.shape, lambda i, *_: (0,) * a.ndim)
    grid_spec = pltpu.PrefetchScalarGridSpec(
        num_scalar_prefetch=3,
        grid=(t // tm,),
        in_specs=[pl.BlockSpec((tm, D_MODEL), lambda i, *_: (i, 0)),
                  pl.BlockSpec((tm, LANES), lambda i, *_: (i, 0)),
                  pl.BlockSpec((2 * tm, LANES), lambda i, *_: (i, 0)),
                  full(ltri), full(upper),
                  pl.BlockSpec(memory_space=pl.ANY)],
        out_specs=pl.BlockSpec((tm, D_MODEL), lambda i, *_: (i, 0)),
        scratch_shapes=[pltpu.VMEM((2, 2 * tm, ROW_TILES, LANES), F32), pltpu.SemaphoreType.DMA((2,))],
    )
    return pl.pallas_call(
        functools.partial(_combine_kernel, tm=tm),
        grid_spec=grid_spec,
        out_shape=jax.ShapeDtypeStruct((t, D_MODEL), F32),
        compiler_params=_cparams(("arbitrary",)),
        name="moe_combine",
    )(n_run, lo_run, g_run, h1, meta, gs, ltri, upper, ye)


def _ple_kernel(h_ref, p_ref, g_ref, wg_ref, wp_ref, fg_ref, o_ref, *, final):
    h = h_ref[...]
    gate = _sigmoid(_dot(_rms(h, g_ref[...]).astype(BF16), wg_ref[...]))
    out = h + _dot(p_ref[...].astype(BF16), wp_ref[...]) * gate
    if final:
        out = _rms(out, fg_ref[...])
    o_ref[...] = out


def _ple(h, p, g, wg, wp, fg, tm, final):
    t = h.shape[0]
    row = lambda n: pl.BlockSpec((tm, n), lambda i: (i, 0))
    full = lambda a: pl.BlockSpec(a.shape, lambda i: (0,) * a.ndim)
    return pl.pallas_call(
        functools.partial(_ple_kernel, final=final),
        grid=(t // tm,),
        in_specs=[row(D_MODEL), row(PLE_DIM), full(g), full(wg), full(wp), full(fg)],
        out_specs=row(D_MODEL),
        out_shape=jax.ShapeDtypeStruct((t, D_MODEL), F32),
        compiler_params=_cparams(("parallel",)),
        name="ple",
    )(h, p, g, wg, wp, fg)


def _swap_halves(w):
    half = w.shape[-1] // 2
    return jnp.concatenate([w[..., half:], w[..., :half]], axis=-1)


def _pad_cols(w, left, total):
    return jnp.pad(w, ((0, 0), (left, total - left - w.shape[1])))


def _arrange_w_in(w):
    sizes = (512, 512, SSD_XBC, SSD_HEADS, MLA_Q_RANK, MLA_KV_RANK, MLA_ROPE)
    pts = np.cumsum(sizes)[:-1].tolist()
    w_conv, w_z, w_xbc, w_dt, w_cq, w_ckv, w_kr = jnp.split(w, pts, axis=1)
    seg_kr = _pad_cols(w_kr, ROPE_LO, LANES)
    seg_dtr = _pad_cols(w_dt, 0, LANES) + _pad_cols(_swap_halves(w_kr), ROPE_LO, LANES)
    return jnp.concatenate([w_conv, w_z, w_xbc, w_cq, w_ckv, seg_kr, seg_dtr], axis=1).astype(BF16)


def _arrange_w_uq(w):
    main, rot = [], []
    for h in range(MLA_HEADS):
        wh = w[:, h * (MLA_NOPE + MLA_ROPE):(h + 1) * (MLA_NOPE + MLA_ROPE)]
        main.append(_pad_cols(wh, 0, HEAD_PAD))
        rot.append(_pad_cols(_swap_halves(wh[:, MLA_NOPE:]), ROPE_LO, HEAD_PAD))
    return jnp.concatenate(main + rot, axis=1).astype(BF16)


def _arrange_w_ukv(w):
    ks, vs = [], []
    for h in range(MLA_HEADS):
        wh = w[:, h * (MLA_NOPE + MLA_V):(h + 1) * (MLA_NOPE + MLA_V)]
        ks.append(_pad_cols(wh[:, :MLA_NOPE], 0, HEAD_PAD))
        vs.append(wh[:, MLA_NOPE:])
    return jnp.concatenate(ks, axis=1).astype(BF16), jnp.concatenate(vs, axis=1).T.astype(BF16)


def _row(v, width=None):
    v = v.reshape(1, -1).astype(F32)
    if width is not None:
        v = jnp.pad(v, ((0, 0), (0, width - v.shape[1])))
    return v


def _pick(n, prefs):
    for c in prefs:
        if n % c == 0:
            return c
    return n


def kernel(x, p, positions, attn_norm_g, w_in, conv_dw_w, conv_dw_b, conv_ln_g, conv_ln_b, ssd_conv_w, ssd_conv_b, ssd_dt_bias, ssd_a_log, ssd_d, ssd_norm_g, mla_q_norm_g, mla_w_uq, mla_kv_norm_g, mla_w_ukv, w_out, ffn_norm_g, dense_w_gate, dense_w_up, dense_w_down, moe_router, moe_w_gate, moe_w_up, moe_w_down, ple_norm_g, ple_w_gate, ple_w_proj, final_norm_g):
    bsz, s, _ = x.shape
    t = bsz * s
    tm_row = _pick(t, (512, 256, 128))
    tm_ffn = _pick(t, (1024, 512, 256, 128))
    fc = 512
    tc = _pick(s, (256, 128))
    tm_moe = _pick(t, (1024, 512, 256, 128))
    tm_tok = _pick(t, (256, 128))
    n_slots = 2 * t + N_EXPERTS * tm_moe

    inv = ROPE_BASE ** (-jnp.arange(0, MLA_ROPE, 2, dtype=F32) / MLA_ROPE)
    inv128 = _pad_cols(jnp.concatenate([inv, inv])[None, :], ROPE_LO, LANES)
    pos128 = jnp.broadcast_to(positions.astype(F32).reshape(t, 1), (t, LANES))
    ctab, stab = _rope_tables(pos128, inv128, tm_row)
    grp = np.arange(CONV_WIDTH) // (CONV_WIDTH // CONV_GROUPS)
    gmean = jnp.asarray((grp[:, None] == grp[None, :]) / (CONV_WIDTH // CONV_GROUPS), BF16)
    tril = jnp.asarray(np.tril(np.ones((SSD_CHUNK, SSD_CHUNK))), BF16)
    hd = np.arange(SSD_WIDTH) // SSD_HEADDIM
    emat = jnp.asarray(np.arange(LANES)[:, None] == hd[None, :], BF16)
    ltri = jnp.asarray(np.tril(np.ones((tm_tok, tm_tok)), -1), BF16)
    upper = jnp.asarray(np.triu(np.ones((LANES, LANES)), 1), BF16)

    dense_wg, dense_wu, dense_wd = dense_w_gate, dense_w_up, dense_w_down
    moe_wg, moe_wu, moe_wd = moe_w_gate, moe_w_up, moe_w_down

    h = x.reshape(t, D_MODEL)
    for i in range(DEPTH):
        u_conv, z, xbc, dtm, q, k, vt = _inproj(
            h, _row(attn_norm_g[i]), _arrange_w_in(w_in[i]), ctab, stab,
            _row(mla_q_norm_g[i]), _arrange_w_uq(mla_w_uq[i]),
            _row(mla_kv_norm_g[i]), *_arrange_w_ukv(mla_w_ukv[i]), tm_row)
        y_conv = _conformer_conv(
            u_conv.reshape(bsz, s, -1), jnp.pad(conv_dw_w[i], ((0, 1), (0, 0))), _row(conv_dw_b[i]),
            _row(conv_ln_g[i]), _row(conv_ln_b[i]), gmean, tc)
        y_ssd = _ssd(
            xbc.reshape(bsz, s, -1), z.reshape(bsz, s, -1), dtm.reshape(bsz, s, -1),
            jnp.pad(ssd_conv_w[i], ((0, 8 - SSD_CONV), (0, 0))), _row(ssd_conv_b[i]),
            _row(ssd_dt_bias[i], LANES), _row(ssd_a_log[i], LANES),
            _row(jnp.repeat(ssd_d[i], SSD_HEADDIM)), _row(ssd_norm_g[i]), tril, emat, tc)
        ym = _attention(q, k, vt, bsz, tc)
        yc, ys = y_conv.reshape(t, -1), y_ssd.reshape(t, -1)
        wo = w_out[i].astype(BF16)
        j = i // 2
        if i % 2 == 0:
            h1, hn = _outproj(yc, ys, ym, h, wo, _row(ffn_norm_g[i]), tm_row)
            h2 = _ffn_dense(hn, h1, dense_wg, dense_wu, dense_wd, j, tm_ffn, fc)
        else:
            router = _pad_cols(moe_router[j], 0, LANES).astype(BF16)
            h1, hn, meta, tcnt, cnt = _outproj(yc, ys, ym, h, wo, _row(ffn_norm_g[i]), tm_tok, router)
            counts = cnt[0, :N_EXPERTS].astype(jnp.int32)
            padded = ((counts + tm_moe - 1) // tm_moe) * tm_moe
            pends = jnp.cumsum(padded)
            pstarts = pends - padded
            before = tcnt[:, 0, :N_EXPERTS].astype(jnp.int32)
            n_run = jnp.concatenate([before[1:], counts[None, :]], axis=0) - before
            lo_run = jnp.cumsum(n_run, axis=1) - n_run
            g_run = pstarts[None, :] + before
            runs = (n_run.reshape(-1), lo_run.reshape(-1), g_run.reshape(-1))
            n_tiles = n_slots // tm_moe
            n_used = (pends[-1] // tm_moe).astype(jnp.int32)
            tile_start = jnp.arange(n_tiles, dtype=jnp.int32) * tm_moe
            tile_exp = jnp.minimum(jnp.sum(pends[None, :] <= tile_start[:, None], axis=1), N_EXPERTS - 1)
            tile_exp = tile_exp.astype(jnp.int32)
            tile_exp = jnp.where(jnp.arange(n_tiles) < n_used, tile_exp, tile_exp[jnp.maximum(n_used - 1, 0)])
            xs, gs = _dispatch(*runs, hn, meta, ltri, upper, n_slots, tm_tok)
            ye = _moe_ffn(tile_exp, n_used.reshape(1), xs, moe_wg, moe_wu, moe_wd, j, tm_moe, fc)
            h2 = _combine(*runs, h1, meta, gs, ltri, upper, ye, tm_tok)
        h = _ple(h2, p[i].reshape(t, PLE_DIM), _row(ple_norm_g[i]), ple_w_gate[i].astype(BF16),
                 ple_w_proj[i].astype(BF16), _row(final_norm_g), tm_row, final=(i == DEPTH - 1))
    return h.reshape(bsz, s, D_MODEL)
```

```python
import functools

import numpy as np
import jax
import jax.numpy as jnp
from jax import lax
from jax.experimental import pallas as pl
from jax.experimental.pallas import tpu as pltpu

F32 = jnp.float32
BF16 = jnp.bfloat16

D_MODEL = 1024
DEPTH = 4
PLE_DIM = 256
CONV_WIDTH = 256
CONV_GROUPS = 4
CONV_KERNEL = 31
SSD_WIDTH = 512
SSD_HEADDIM = 64
SSD_HEADS = 8
SSD_NGROUPS = 2
SSD_STATE = 128
SSD_CONV = 4
SSD_CHUNK = 128
MLA_HEADS = 4
MLA_NOPE = 64
MLA_ROPE = 32
MLA_V = 64
MLA_Q_RANK = 256
MLA_KV_RANK = 128
ROPE_BASE = 10000.0
D_FF = 3584
N_EXPERTS = 8
RMS_EPS = 1e-6
LN_EPS = 1e-5

LANES = 128
HEAD_PAD = 128
ROPE_LO = MLA_NOPE
VMEM_LIMIT = 48 * 1024 * 1024

C_CONV = 0
C_Z = 512
C_XBC = 1024
C_CQ = 2048
C_CKV = 2304
C_KR = 2432
C_DTR = 2560
IN_COLS_PAD = 2688


def _cparams(sem):
    return pltpu.CompilerParams(dimension_semantics=sem, vmem_limit_bytes=VMEM_LIMIT)


def _dot(a, b):
    return jnp.dot(a, b, preferred_element_type=F32)


def _dot_nt(a, b):
    return lax.dot_general(a, b, (((1,), (1,)), ((), ())), preferred_element_type=F32)


def _split3(a):
    a1 = a.astype(BF16)
    r1 = a - a1.astype(F32)
    a2 = r1.astype(BF16)
    a3 = (r1 - a2.astype(F32)).astype(BF16)
    return a1, a2, a3


def _dot_f32_lhs(a, m):
    a1, a2, a3 = _split3(a)
    return _dot(a1, m) + _dot(a2, m) + _dot(a3, m)


def _dot_f32_rhs(m, b):
    b1, b2, b3 = _split3(b)
    return _dot(m, b1) + _dot(m, b2) + _dot(m, b3)


def _rms(x, g, eps=RMS_EPS):
    return x * lax.rsqrt(jnp.mean(x * x, axis=-1, keepdims=True) + eps) * g


def _sigmoid(x):
    return 1.0 / (1.0 + jnp.exp(-x))


def _silu(x):
    return x * _sigmoid(x)


def _rope_kernel(pos_ref, inv_ref, c_ref, s_ref):
    ang = pos_ref[...] * inv_ref[...]
    lane = lax.broadcasted_iota(jnp.int32, ang.shape, 1)
    in_rope = (lane >= ROPE_LO) & (lane < ROPE_LO + MLA_ROPE)
    first_half = lane < ROPE_LO + MLA_ROPE // 2
    cos = jnp.cos(ang)
    sin = jnp.sin(ang)
    c_ref[...] = jnp.where(in_rope, cos, jnp.where(lane < ROPE_LO, 1.0, 0.0))
    s_ref[...] = jnp.where(in_rope, jnp.where(first_half, -sin, sin), 0.0)


def _rope_tables(pos128, inv128, tm):
    t = pos128.shape[0]
    return pl.pallas_call(
        _rope_kernel,
        grid=(t // tm,),
        in_specs=[pl.BlockSpec((tm, LANES), lambda i: (i, 0)),
                  pl.BlockSpec((1, LANES), lambda i: (0, 0))],
        out_specs=[pl.BlockSpec((tm, LANES), lambda i: (i, 0))] * 2,
        out_shape=[jax.ShapeDtypeStruct((t, LANES), F32)] * 2,
        compiler_params=_cparams(("parallel",)),
        name="rope_tables",
    )(pos128, inv128)


def _inproj_kernel(h_ref, g_ref, w_ref, c_ref, s_ref, gq_ref, wq_ref, gkv_ref, wk_ref, wvt_ref,
                   oconv_ref, oz_ref, oxbc_ref, odt_ref, oq_ref, ok_ref, ovt_ref):
    xn = _rms(h_ref[...], g_ref[...]).astype(BF16)
    oconv_ref[...] = _dot(xn, w_ref[0, :, C_CONV:C_Z])
    oz_ref[...] = _dot(xn, w_ref[0, :, C_Z:C_XBC])
    oxbc_ref[...] = _dot(xn, w_ref[0, :, C_XBC:C_CQ])
    cq = _dot(xn, w_ref[0, :, C_CQ:C_CKV])
    ckv = _dot(xn, w_ref[0, :, C_CKV:C_KR])
    kr = _dot(xn, w_ref[0, :, C_KR:C_DTR])
    dtr = _dot(xn, w_ref[0, :, C_DTR:IN_COLS_PAD])
    odt_ref[...] = dtr
    c = c_ref[...]
    s = s_ref[...]
    c4 = jnp.concatenate([c] * MLA_HEADS, axis=1)
    s4 = jnp.concatenate([s] * MLA_HEADS, axis=1)
    qq = _dot(_rms(cq, gq_ref[...]).astype(BF16), wq_ref[...])
    nq = MLA_HEADS * HEAD_PAD
    scale = (MLA_NOPE + MLA_ROPE) ** -0.5 * np.log2(np.e)
    oq_ref[...] = ((qq[:, :nq] * c4 + qq[:, nq:] * s4) * scale).astype(BF16)
    ckvn = _rms(ckv, gkv_ref[...]).astype(BF16)
    kpe = kr * c + dtr * s
    ok_ref[...] = (_dot(ckvn, wk_ref[...]) + jnp.concatenate([kpe] * MLA_HEADS, axis=1)).astype(BF16)
    ovt_ref[...] = _dot_nt(wvt_ref[...], ckvn).astype(BF16)


def _inproj(h, g, w, layer, ctab, stab, gq, wq, gkv, wk, wvt, tm):
    t = h.shape[0]
    row = lambda n: pl.BlockSpec((tm, n), lambda i: (i, 0))
    full = lambda a: pl.BlockSpec(a.shape, lambda i: (0,) * a.ndim)
    nq = MLA_HEADS * HEAD_PAD
    nv = MLA_HEADS * MLA_V
    widths = (512, 512, 1024, LANES, nq, nq)
    dtypes = (F32, F32, F32, F32, BF16, BF16)
    return pl.pallas_call(
        _inproj_kernel,
        grid=(t // tm,),
        in_specs=[row(D_MODEL), full(g), pl.BlockSpec((1,) + w.shape[1:], lambda i: (layer, 0, 0)),
                  row(LANES), row(LANES), full(gq), full(wq), full(gkv), full(wk), full(wvt)],
        out_specs=[row(n) for n in widths] + [pl.BlockSpec((nv, tm), lambda i: (0, i))],
        out_shape=[jax.ShapeDtypeStruct((t, n), d) for n, d in zip(widths, dtypes)]
        + [jax.ShapeDtypeStruct((nv, t), BF16)],
        compiler_params=_cparams(("parallel",)),
        name="inproj",
    )(h, g, w, ctab, stab, gq, wq, gkv, wk, wvt)


CONV_HALO = 32
CONV_SUB = 64


def _conv_kernel(u_ref, w_ref, b_ref, lg_ref, lb_ref, gm_ref, o_ref, gbuf, shifted, *, tc):
    @pl.when(pl.program_id(1) == 0)
    def _():
        gbuf[0:CONV_HALO, :] = jnp.zeros((CONV_HALO, CONV_WIDTH), F32)

    u = u_ref[0]
    gbuf[CONV_HALO:CONV_HALO + tc, :] = u[:, :CONV_WIDTH] * _sigmoid(u[:, CONV_WIDTH:])
    gm = gm_ref[...]
    first = CONV_HALO - (CONV_KERNEL - 1)
    span = CONV_HALO + tc - 8
    for s in range(1, 8):
        shifted[s - 1, 0:span, :] = gbuf[s:s + span, :]
    for r0 in range(0, tc, CONV_SUB):
        acc = jnp.broadcast_to(b_ref[...], (CONV_SUB, CONV_WIDTH))
        for j in range(CONV_KERNEL):
            start = first + j + r0
            s, a = start % 8, start - start % 8
            assert a + CONV_SUB <= span or s == 0
            win = gbuf[a:a + CONV_SUB, :] if s == 0 else shifted[s - 1, a:a + CONV_SUB, :]
            acc = acc + w_ref[j:j + 1, :] * win
        mu = _dot_f32_lhs(acc, gm)
        d = acc - mu
        var = _dot_f32_lhs(d * d, gm)
        hn = d * lax.rsqrt(var + LN_EPS) * lg_ref[...] + lb_ref[...]
        o_ref[0, r0:r0 + CONV_SUB, :] = _silu(hn).astype(BF16)
    gbuf[0:CONV_HALO, :] = gbuf[tc:tc + CONV_HALO, :]


def _conformer_conv(u, w, b, lg, lb, gm, tc):
    bsz, s, _ = u.shape
    full = lambda a: pl.BlockSpec(a.shape, lambda i, j: (0,) * a.ndim)
    return pl.pallas_call(
        functools.partial(_conv_kernel, tc=tc),
        grid=(bsz, s // tc),
        in_specs=[pl.BlockSpec((1, tc, 2 * CONV_WIDTH), lambda i, j: (i, j, 0)),
                  full(w), full(b), full(lg), full(lb), full(gm)],
        out_specs=pl.BlockSpec((1, tc, CONV_WIDTH), lambda i, j: (i, j, 0)),
        out_shape=jax.ShapeDtypeStruct((bsz, s, CONV_WIDTH), BF16),
        scratch_shapes=[pltpu.VMEM((CONV_HALO + tc, CONV_WIDTH), F32),
                        pltpu.VMEM((7, CONV_HALO + tc, CONV_WIDTH), F32)],
        compiler_params=_cparams(("parallel", "arbitrary")),
        name="conformer_conv",
    )(u, w, b, lg, lb, gm)


SSD_HALO = 8
SSD_XBC = SSD_WIDTH + 2 * SSD_NGROUPS * SSD_STATE
GROUP_W = SSD_WIDTH // SSD_NGROUPS
HEADS_PER_GROUP = SSD_HEADS // SSD_NGROUPS


def _ssd_chunk(r0, cbuf, z_ref, dt_ref, cw_ref, cb_ref, dtb_ref, alog_ref, dsk_ref, ng_ref, tril_ref, exp_ref,
               o_ref, state):
    L = SSD_CHUNK
    first = SSD_HALO - (SSD_CONV - 1) + r0
    acc = jnp.broadcast_to(cb_ref[...], (L, SSD_XBC))
    for j in range(SSD_CONV):
        acc = acc + cw_ref[j:j + 1, :] * cbuf[first + j:first + j + L, :]
    xc = _silu(acc)
    xs = xc[:, :SSD_WIDTH]
    bm = xc[:, SSD_WIDTH:SSD_WIDTH + SSD_NGROUPS * SSD_STATE]
    cm = xc[:, SSD_WIDTH + SSD_NGROUPS * SSD_STATE:]

    lane = lax.broadcasted_iota(jnp.int32, (1, LANES), 1)
    v = dt_ref[0, r0:r0 + L, :] + dtb_ref[...]
    dt = jnp.maximum(v, 0.0) + jnp.log1p(jnp.exp(-jnp.abs(v)))
    a = jnp.where(lane < SSD_HEADS, -jnp.exp(alog_ref[...]), 0.0)
    cs = _dot_f32_rhs(tril_ref[...], dt * a)
    cs_t = cs.T
    cs_last = cs[L - 1:L, :]
    emat = exp_ref[...]
    dt_e = _dot_f32_lhs(dt, emat)
    ecs_e = _dot_f32_lhs(jnp.exp(cs), emat)
    ds_e = _dot_f32_lhs(jnp.exp(cs_last - cs), emat)
    cd_e = _dot_f32_lhs(jnp.broadcast_to(jnp.exp(cs_last), (8, LANES)), emat)[0:1, :]

    xd = xs * dt_e
    xdb = xd.astype(BF16)
    xds = (xd * ds_e).astype(BF16)
    rows = lax.broadcasted_iota(jnp.int32, (L, L), 0)
    cols = lax.broadcasted_iota(jnp.int32, (L, L), 1)
    causal = rows >= cols
    ys = []
    for g in range(SSD_NGROUPS):
        cmg = cm[:, g * SSD_STATE:(g + 1) * SSD_STATE].astype(BF16)
        bmg = bm[:, g * SSD_STATE:(g + 1) * SSD_STATE]
        cbm = _dot_nt(cmg, bmg.astype(BF16))
        yd = []
        for r in range(HEADS_PER_GROUP):
            h = g * HEADS_PER_GROUP + r
            seg = cs[:, h:h + 1] - cs_t[h:h + 1, :]
            dec = jnp.exp(jnp.where(causal, seg, -jnp.inf))
            mix = (cbm * dec).astype(BF16)
            yd.append(_dot(mix, xdb[:, h * SSD_HEADDIM:(h + 1) * SSD_HEADDIM]))
        gs = slice(g * GROUP_W, (g + 1) * GROUP_W)
        prev = state[g]
        y_off = _dot(cmg, prev.astype(BF16)) * ecs_e[:, gs]
        st_new = _dot(bmg.T.astype(BF16), xds[:, gs])
        state[g] = prev * cd_e[:, gs] + st_new
        ys.append(jnp.concatenate(yd, axis=1) + y_off)
    y = jnp.concatenate(ys, axis=1) + dsk_ref[...] * xs
    yg = y * _silu(z_ref[0, r0:r0 + L, :])
    outs = []
    for g in range(SSD_NGROUPS):
        ygg = yg[:, g * GROUP_W:(g + 1) * GROUP_W]
        outs.append(ygg * lax.rsqrt(jnp.mean(ygg * ygg, axis=-1, keepdims=True) + RMS_EPS))
    o_ref[0, r0:r0 + L, :] = (jnp.concatenate(outs, axis=1) * ng_ref[...]).astype(BF16)


def _ssd_kernel(xbc_ref, z_ref, dt_ref, cw_ref, cb_ref, dtb_ref, alog_ref, dsk_ref, ng_ref, tril_ref, exp_ref,
                o_ref, cbuf, state, *, rows):
    @pl.when(pl.program_id(1) == 0)
    def _():
        cbuf[0:SSD_HALO, :] = jnp.zeros((SSD_HALO, SSD_XBC), F32)
        state[...] = jnp.zeros(state.shape, F32)

    cbuf[SSD_HALO:SSD_HALO + rows, :] = xbc_ref[0]
    for r0 in range(0, rows, SSD_CHUNK):
        _ssd_chunk(r0, cbuf, z_ref, dt_ref, cw_ref, cb_ref, dtb_ref, alog_ref, dsk_ref, ng_ref, tril_ref, exp_ref,
                   o_ref, state)
    cbuf[0:SSD_HALO, :] = cbuf[rows:rows + SSD_HALO, :]


def _ssd(xbc, z, dtm, cw, cb, dtb, alog, dsk, ng, tril, emat, rows):
    bsz, s, _ = xbc.shape
    full = lambda a: pl.BlockSpec(a.shape, lambda i, j: (0,) * a.ndim)
    blk = lambda n: pl.BlockSpec((1, rows, n), lambda i, j: (i, j, 0))
    return pl.pallas_call(
        functools.partial(_ssd_kernel, rows=rows),
        grid=(bsz, s // rows),
        in_specs=[blk(SSD_XBC), blk(SSD_WIDTH), blk(LANES),
                  full(cw), full(cb), full(dtb), full(alog), full(dsk), full(ng), full(tril), full(emat)],
        out_specs=blk(SSD_WIDTH),
        out_shape=jax.ShapeDtypeStruct((bsz, s, SSD_WIDTH), BF16),
        scratch_shapes=[pltpu.VMEM((SSD_HALO + rows, SSD_XBC), F32),
                        pltpu.VMEM((SSD_NGROUPS, SSD_STATE, GROUP_W), F32)],
        compiler_params=_cparams(("parallel", "arbitrary")),
        name="ssd",
    )(xbc, z, dtm, cw, cb, dtb, alog, dsk, ng, tril, emat)


def _attn_kernel(q_ref, k_ref, vt_ref, o_ref, m_sc, l_sc, acc_sc, st0, st1, *, tq):
    i = pl.program_id(1)
    krow = lax.broadcasted_iota(jnp.int32, (tq, tq), 0)
    qcol = lax.broadcasted_iota(jnp.int32, (tq, tq), 1)
    causal = krow <= qcol
    ones = jnp.ones((16, tq), BF16)
    m_sc[...] = jnp.full(m_sc.shape, -jnp.inf, F32)
    l_sc[...] = jnp.zeros(l_sc.shape, F32)
    acc_sc[...] = jnp.zeros(acc_sc.shape, F32)

    def scores(j, st_ref):
        start = pl.multiple_of(j * tq, tq)
        for h in range(MLA_HEADS):
            hs = slice(h * HEAD_PAD, (h + 1) * HEAD_PAD)
            st_ref[h] = _dot_nt(k_ref[pl.ds(start, tq), hs], q_ref[:, hs])

    def update(j, st_ref, masked):
        start = pl.multiple_of(j * tq, tq)
        for h in range(MLA_HEADS):
            vs = slice(h * MLA_V, (h + 1) * MLA_V)
            st = st_ref[h]
            if masked:
                st = jnp.where(causal, st, -jnp.inf)
            m = m_sc[h:h + 1, :]
            m_new = jnp.maximum(m, jnp.max(st, axis=0, keepdims=True))
            p = jnp.exp2(st - m_new).astype(BF16)
            alpha = jnp.exp2(m - m_new)
            m_sc[h:h + 1, :] = m_new
            lhs = jnp.concatenate([vt_ref[vs, pl.ds(start, tq)], ones], axis=0)
            pv = _dot(lhs, p)
            l_sc[h:h + 1, :] = alpha * l_sc[h:h + 1, :] + pv[MLA_V:MLA_V + 1, :]
            acc_sc[vs, :] = alpha * acc_sc[vs, :] + pv[:MLA_V, :]

    def pair(jp, c):
        j0 = 2 * jp
        scores(j0 + 1, st1)
        update(j0, st0, False)
        scores(j0 + 2, st0)
        update(j0 + 1, st1, False)
        return c

    scores(0, st0)
    npairs = i // 2
    lax.fori_loop(0, npairs, pair, 0)

    @pl.when(i == 2 * npairs)
    def _():
        update(i, st0, True)

    @pl.when(i != 2 * npairs)
    def _():
        scores(i, st1)
        update(i - 1, st0, False)
        update(i, st1, True)

    outs = [acc_sc[h * MLA_V:(h + 1) * MLA_V, :] / l_sc[h:h + 1, :] for h in range(MLA_HEADS)]
    o_ref[...] = jnp.concatenate(outs, axis=0).T.astype(BF16)


def _attention(q, k, vt, bsz, tq):
    t = q.shape[0]
    s = t // bsz
    nblk = s // tq
    nq = MLA_HEADS * HEAD_PAD
    nv = MLA_HEADS * MLA_V
    return pl.pallas_call(
        functools.partial(_attn_kernel, tq=tq),
        grid=(bsz, nblk),
        in_specs=[pl.BlockSpec((tq, nq), lambda b, i: (b * nblk + i, 0)),
                  pl.BlockSpec((s, nq), lambda b, i: (b, 0)),
                  pl.BlockSpec((nv, s), lambda b, i: (0, b))],
        out_specs=pl.BlockSpec((tq, nv), lambda b, i: (b * nblk + i, 0)),
        out_shape=jax.ShapeDtypeStruct((t, nv), BF16),
        scratch_shapes=[pltpu.VMEM((8, tq), F32), pltpu.VMEM((8, tq), F32), pltpu.VMEM((nv, tq), F32),
                        pltpu.VMEM((MLA_HEADS, tq, tq), F32), pltpu.VMEM((MLA_HEADS, tq, tq), F32)],
        compiler_params=_cparams(("parallel", "parallel")),
        name="mla_attention",
    )(q, k, vt)


M_IDX0, M_IDX1, M_GATE0, M_GATE1 = range(4)


def _outproj_body(yc_ref, ys_ref, ym_ref, h_ref, w_ref, g_ref):
    acc = _dot(yc_ref[...], w_ref[0:CONV_WIDTH, :])
    acc = acc + _dot(ys_ref[...], w_ref[CONV_WIDTH:CONV_WIDTH + SSD_WIDTH, :])
    acc = acc + _dot(ym_ref[...], w_ref[CONV_WIDTH + SSD_WIDTH:, :])
    h1 = h_ref[...] + acc
    return h1, _rms(h1, g_ref[...])


def _outproj_moe_kernel(yc_ref, ys_ref, ym_ref, h_ref, w_ref, g_ref, r_ref,
                        h1_ref, hn_ref, meta_ref, tcnt_ref, cnt_ref):
    @pl.when(pl.program_id(0) == 0)
    def _():
        cnt_ref[...] = jnp.zeros(cnt_ref.shape, F32)

    h1, hn = _outproj_body(yc_ref, ys_ref, ym_ref, h_ref, w_ref, g_ref)
    h1_ref[...] = h1
    hnb = hn.astype(BF16)
    hn_ref[...] = hnb
    logits = _dot(hnb, r_ref[...])
    lane = lax.broadcasted_iota(jnp.int32, logits.shape, 1)
    lm = jnp.where(lane < N_EXPERTS, logits, -jnp.inf)
    m1 = jnp.max(lm, axis=-1, keepdims=True)
    i1 = jnp.min(jnp.where(lm == m1, lane, LANES), axis=-1, keepdims=True)
    lm2 = jnp.where(lane == i1, -jnp.inf, lm)
    m2 = jnp.max(lm2, axis=-1, keepdims=True)
    i2 = jnp.min(jnp.where(lm2 == m2, lane, LANES), axis=-1, keepdims=True)
    e = jnp.exp(m2 - m1)
    g1 = 1.0 / (1.0 + e)
    g2 = e / (1.0 + e)
    onehot = jnp.where((lane == i1) | (lane == i2), 1.0, 0.0)
    tcnt_ref[0] = cnt_ref[...]
    cnt_ref[...] = cnt_ref[...] + jnp.sum(onehot, axis=0, keepdims=True)
    meta = jnp.where(lane == M_IDX0, i1.astype(F32), 0.0)
    meta = jnp.where(lane == M_IDX1, i2.astype(F32), meta)
    meta = jnp.where(lane == M_GATE0, g1, meta)
    meta = jnp.where(lane == M_GATE1, g2, meta)
    meta_ref[...] = meta


def _outproj_moe(yc, ys, ym, h, w, g, tm, router):
    t = h.shape[0]
    row = lambda n: pl.BlockSpec((tm, n), lambda i: (i, 0))
    full = lambda a: pl.BlockSpec(a.shape, lambda i: (0,) * a.ndim)
    return pl.pallas_call(
        _outproj_moe_kernel,
        grid=(t // tm,),
        in_specs=[row(CONV_WIDTH), row(SSD_WIDTH), row(MLA_HEADS * MLA_V), row(D_MODEL), full(w), full(g),
                  full(router)],
        out_specs=[row(D_MODEL), row(D_MODEL), row(LANES),
                   pl.BlockSpec((1, 8, LANES), lambda i: (i, 0, 0)), pl.BlockSpec((8, LANES), lambda i: (0, 0))],
        out_shape=[jax.ShapeDtypeStruct((t, D_MODEL), F32), jax.ShapeDtypeStruct((t, D_MODEL), BF16),
                   jax.ShapeDtypeStruct((t, LANES), F32), jax.ShapeDtypeStruct((t // tm, 8, LANES), F32),
                   jax.ShapeDtypeStruct((8, LANES), F32)],
        compiler_params=_cparams(("arbitrary",)),
        name="outproj_moe",
    )(yc, ys, ym, h, w, g, router)


def _ffn_kernel(yc_ref, ys_ref, ym_ref, h_ref, wo_ref, g_ref, wg_ref, wu_ref, wd_ref, o_ref, hn_sc):
    @pl.when(pl.program_id(1) == 0)
    def _():
        h1, hn = _outproj_body(yc_ref, ys_ref, ym_ref, h_ref, wo_ref, g_ref)
        o_ref[...] = h1
        hn_sc[...] = hn.astype(BF16)

    x = hn_sc[...]
    mid = (_silu(_dot(x, wg_ref[0].astype(BF16))) * _dot(x, wu_ref[0].astype(BF16))).astype(BF16)
    o_ref[...] += _dot(mid, wd_ref[0].astype(BF16))


def _outproj_ffn_dense(yc, ys, ym, h, wo, g, wg, wu, wd, layer, tm, fc):
    t = h.shape[0]
    row = lambda n: pl.BlockSpec((tm, n), lambda i, f: (i, 0))
    full = lambda a: pl.BlockSpec(a.shape, lambda i, f: (0,) * a.ndim)
    return pl.pallas_call(
        _ffn_kernel,
        grid=(t // tm, D_FF // fc),
        in_specs=[row(CONV_WIDTH), row(SSD_WIDTH), row(MLA_HEADS * MLA_V), row(D_MODEL), full(wo), full(g),
                  pl.BlockSpec((1, D_MODEL, fc), lambda i, f: (layer, 0, f)),
                  pl.BlockSpec((1, D_MODEL, fc), lambda i, f: (layer, 0, f)),
                  pl.BlockSpec((1, fc, D_MODEL), lambda i, f: (layer, f, 0))],
        out_specs=row(D_MODEL),
        out_shape=jax.ShapeDtypeStruct((t, D_MODEL), F32),
        scratch_shapes=[pltpu.VMEM((tm, D_MODEL), BF16)],
        compiler_params=_cparams(("parallel", "arbitrary")),
        name="outproj_ffn_dense",
    )(yc, ys, ym, h, wo, g, wg, wu, wd)


ROW_TILES = D_MODEL // LANES


def _rows_to_tiles(val):
    blocks = jnp.stack([val[:, c * LANES:(c + 1) * LANES] for c in range(ROW_TILES)], axis=0)
    return jnp.transpose(blocks, (1, 0, 2))


def _tiles_to_rows(val):
    blocks = jnp.transpose(val, (1, 0, 2))
    return jnp.concatenate([blocks[c] for c in range(ROW_TILES)], axis=1)


def _run_bits(tm):
    return [1 << b for b in range(tm.bit_length() - 1, -1, -1)]


def _run_copies(n, src_ref, src0, dst_ref, dst0, sem, tm, wait, src_step=1):
    off = 0
    for b in _run_bits(tm):
        part = n & b

        @pl.when(part != 0)
        def _(off=off, b=b):
            cp = pltpu.make_async_copy(src_ref.at[pl.ds(src0 + off * src_step, b)],
                                       dst_ref.at[pl.ds(dst0 + off, b)], sem)
            if wait:
                cp.wait()
            else:
                cp.start()

        off = off + part


def _sorted_positions(meta, ltri, upper):
    lane = lax.broadcasted_iota(jnp.int32, meta.shape, 1)
    i1 = meta[:, M_IDX0:M_IDX0 + 1].astype(jnp.int32)
    i2 = meta[:, M_IDX1:M_IDX1 + 1].astype(jnp.int32)
    onehot = jnp.where((lane == i1) | (lane == i2), 1.0, 0.0)
    before = _dot(ltri, onehot.astype(BF16))
    n = jnp.broadcast_to(jnp.sum(onehot, axis=0, keepdims=True), (8, LANES))
    loff = _dot_f32_lhs(n, upper)[0:1, :]
    pos = before + loff
    q1 = jnp.sum(jnp.where(lane == i1, pos, 0.0), axis=-1, keepdims=True)
    q2 = jnp.sum(jnp.where(lane == i2, pos, 0.0), axis=-1, keepdims=True)
    return q1, q2


def _dispatch_kernel(n_ref, lo_ref, g_ref, ps_ref, pn_ref, x_ref, meta_ref, ltri_ref, up_ref, xs_ref, gs_ref,
                     xbuf, zbuf, sems, zsem, *, tm):
    i = pl.program_id(0)
    nt = pl.num_programs(0)
    slot = i % 2
    zrows = zbuf.shape[0]

    def pads(wait):
        for e in range(N_EXPERTS):
            _run_copies(pn_ref[e], zbuf, 0, xs_ref, ps_ref[e], zsem, zrows, wait, src_step=0)
        tail0, tail_len = ps_ref[N_EXPERTS], pn_ref[N_EXPERTS]
        for c in range(2 * N_EXPERTS):

            @pl.when(c * zrows < tail_len)
            def _(c=c):
                cp = pltpu.make_async_copy(zbuf, xs_ref.at[pl.ds(tail0 + c * zrows, zrows)], zsem)
                if wait:
                    cp.wait()
                else:
                    cp.start()

    @pl.when(i == 0)
    def _():
        zbuf[...] = jnp.zeros(zbuf.shape, F32)
        pads(False)

    meta = meta_ref[...]
    q1, q2 = _sorted_positions(meta, ltri_ref[...], up_ref[...])
    lane = lax.broadcasted_iota(jnp.int32, meta.shape, 1)
    qmat = jnp.where(lane == 0, q1, jnp.where(lane == 1, q2, 0.0))
    qt = qmat.T
    srow = lax.broadcasted_iota(jnp.int32, (2 * tm, tm), 0).astype(F32)
    p1 = jnp.where(srow == qt[0:1, :], 1.0, 0.0).astype(BF16)
    p2 = jnp.where(srow == qt[1:2, :], 1.0, 0.0).astype(BF16)
    g1 = jnp.broadcast_to(meta[:, M_GATE0:M_GATE0 + 1], (tm, LANES))
    g2 = jnp.broadcast_to(meta[:, M_GATE1:M_GATE1 + 1], (tm, LANES))
    gs_ref[...] = _dot_f32_rhs(p1, g1) + _dot_f32_rhs(p2, g2)
    xbuf[slot] = _rows_to_tiles(_dot(p1 + p2, x_ref[...]))

    def runs(tile, sl, wait):
        for e in range(N_EXPERTS):
            k = tile * N_EXPERTS + e
            _run_copies(n_ref[k], xbuf.at[sl], lo_ref[k], xs_ref, g_ref[k], sems.at[sl], tm, wait)

    runs(i, slot, False)

    @pl.when(i > 0)
    def _():
        runs(i - 1, 1 - slot, True)

    @pl.when(i == nt - 1)
    def _():
        runs(i, slot, True)
        pads(True)


def _dispatch(n_run, lo_run, g_run, pad_start, pad_len, hn, meta, ltri, upper, n_slots, tm, tm_moe):
    t = hn.shape[0]
    full = lambda a: pl.BlockSpec(a.shape, lambda i, *_: (0,) * a.ndim)
    grid_spec = pltpu.PrefetchScalarGridSpec(
        num_scalar_prefetch=5,
        grid=(t // tm,),
        in_specs=[pl.BlockSpec((tm, D_MODEL), lambda i, *_: (i, 0)),
                  pl.BlockSpec((tm, LANES), lambda i, *_: (i, 0)),
                  full(ltri), full(upper)],
        out_specs=[pl.BlockSpec(memory_space=pl.ANY), pl.BlockSpec((2 * tm, LANES), lambda i, *_: (i, 0))],
        scratch_shapes=[pltpu.VMEM((2, 2 * tm, ROW_TILES, LANES), F32),
                        pltpu.VMEM((tm_moe // 2, ROW_TILES, LANES), F32),
                        pltpu.SemaphoreType.DMA((2,)), pltpu.SemaphoreType.DMA],
    )
    return pl.pallas_call(
        functools.partial(_dispatch_kernel, tm=tm),
        grid_spec=grid_spec,
        out_shape=[jax.ShapeDtypeStruct((n_slots, ROW_TILES, LANES), F32),
                   jax.ShapeDtypeStruct((2 * t, LANES), F32)],
        compiler_params=_cparams(("arbitrary",)),
        name="moe_dispatch",
    )(n_run, lo_run, g_run, pad_start, pad_len, hn, meta, ltri, upper)


def _moe_ffn_kernel(texp_ref, nused_ref, nvalid_ref, x_ref, wg_ref, wu_ref, wd_ref, o_ref, xb, acc, *, tm):
    del texp_ref, nused_ref
    i = pl.program_id(0)
    f = pl.program_id(1)
    half = tm // 2

    @pl.when(f == 0)
    def _():
        xb[...] = _tiles_to_rows(x_ref[...]).astype(BF16)
        acc[...] = jnp.zeros(acc.shape, F32)

    def rows(n):
        x = xb[0:n, :]
        mid = (_silu(_dot(x, wg_ref[0, 0].astype(BF16))) * _dot(x, wu_ref[0, 0].astype(BF16))).astype(BF16)
        acc[0:n, :] += _dot(mid, wd_ref[0, 0].astype(BF16))

    nvalid = nvalid_ref[i]

    @pl.when(nvalid > half)
    def _():
        rows(tm)

    @pl.when((nvalid > 0) & (nvalid <= half))
    def _():
        rows(half)

    @pl.when(f == pl.num_programs(1) - 1)
    def _():
        o_ref[...] = _rows_to_tiles(acc[...])


def _moe_ffn(tile_exp, n_used, n_valid, xs, wg, wu, wd, layer, tm, fc):
    n_slots = xs.shape[0]
    nf = D_FF // fc

    def fsel(i, f, nu):
        return jnp.where(i < nu[0], f, nf - 1)

    grid_spec = pltpu.PrefetchScalarGridSpec(
        num_scalar_prefetch=3,
        grid=(n_slots // tm, nf),
        in_specs=[pl.BlockSpec((tm, ROW_TILES, LANES), lambda i, f, te, nu, nv: (jnp.minimum(i, nu[0] - 1), 0, 0)),
                  pl.BlockSpec((1, 1, D_MODEL, fc), lambda i, f, te, nu, nv: (layer, te[i], 0, fsel(i, f, nu))),
                  pl.BlockSpec((1, 1, D_MODEL, fc), lambda i, f, te, nu, nv: (layer, te[i], 0, fsel(i, f, nu))),
                  pl.BlockSpec((1, 1, fc, D_MODEL), lambda i, f, te, nu, nv: (layer, te[i], fsel(i, f, nu), 0))],
        out_specs=pl.BlockSpec((tm, ROW_TILES, LANES), lambda i, f, te, nu, nv: (i, 0, 0)),
        scratch_shapes=[pltpu.VMEM((tm, D_MODEL), BF16), pltpu.VMEM((tm, D_MODEL), F32)],
    )
    return pl.pallas_call(
        functools.partial(_moe_ffn_kernel, tm=tm),
        grid_spec=grid_spec,
        out_shape=jax.ShapeDtypeStruct((n_slots, ROW_TILES, LANES), F32),
        compiler_params=_cparams(("parallel", "arbitrary")),
        name="moe_ffn",
    )(tile_exp, n_used, n_valid, xs, wg, wu, wd)


def _combine_kernel(n_ref, lo_ref, g_ref, h1_ref, meta_ref, gs_ref, ltri_ref, up_ref, ye_ref, o_ref,
                    ybuf, sems, *, tm):
    i = pl.program_id(0)
    nt = pl.num_programs(0)
    slot = i % 2

    def runs(tile, sl, wait):
        for e in range(N_EXPERTS):
            k = tile * N_EXPERTS + e
            _run_copies(n_ref[k], ye_ref, g_ref[k], ybuf.at[sl], lo_ref[k], sems.at[sl], tm, wait)

    @pl.when(i == 0)
    def _():
        runs(i, slot, False)

    @pl.when(i + 1 < nt)
    def _():
        runs(i + 1, 1 - slot, False)

    q1, q2 = _sorted_positions(meta_ref[...], ltri_ref[...], up_ref[...])
    scol = lax.broadcasted_iota(jnp.int32, (tm, 2 * tm), 1).astype(F32)
    sel = jnp.where((scol == q1) | (scol == q2), 1.0, 0.0).astype(BF16)
    runs(i, slot, True)
    y = _tiles_to_rows(ybuf[slot]) * gs_ref[:, 0:1]
    hi = y.astype(BF16)
    lo = (y - hi.astype(F32)).astype(BF16)
    o_ref[...] = h1_ref[...] + _dot(sel, hi) + _dot(sel, lo)


def _combine(n_run, lo_run, g_run, h1, meta, gs, ltri, upper, ye, tm):
    t = h1.shape[0]
    full = lambda a: pl.BlockSpec(a.shape, lambda i, *_: (0,) * a.ndim)
    grid_spec = pltpu.PrefetchScalarGridSpec(
        num_scalar_prefetch=3,
        grid=(t // tm,),
        in_specs=[pl.BlockSpec((tm, D_MODEL), lambda i, *_: (i, 0)),
                  pl.BlockSpec((tm, LANES), lambda i, *_: (i, 0)),
                  pl.BlockSpec((2 * tm, LANES), lambda i, *_: (i, 0)),
                  full(ltri), full(upper),
                  pl.BlockSpec(memory_space=pl.ANY)],
        out_specs=pl.BlockSpec((tm, D_MODEL), lambda i, *_: (i, 0)),
        scratch_shapes=[pltpu.VMEM((2, 2 * tm, ROW_TILES, LANES), F32), pltpu.SemaphoreType.DMA((2,))],
    )
    return pl.pallas_call(
        functools.partial(_combine_kernel, tm=tm),
        grid_spec=grid_spec,
        out_shape=jax.ShapeDtypeStruct((t, D_MODEL), F32),
        compiler_params=_cparams(("arbitrary",)),
        name="moe_combine",
    )(n_run, lo_run, g_run, h1, meta, gs, ltri, upper, ye)


def _ple_kernel(h_ref, p_ref, g_ref, wg_ref, wp_ref, fg_ref, o_ref, *, final):
    h = h_ref[...]
    gate = _sigmoid(_dot(_rms(h, g_ref[...]).astype(BF16), wg_ref[...]))
    out = h + _dot(p_ref[0].astype(BF16), wp_ref[...]) * gate
    if final:
        out = _rms(out, fg_ref[...])
    o_ref[...] = out


def _ple(h, p, layer, g, wg, wp, fg, tm, final):
    t = h.shape[0]
    row = lambda n: pl.BlockSpec((tm, n), lambda i: (i, 0))
    full = lambda a: pl.BlockSpec(a.shape, lambda i: (0,) * a.ndim)
    return pl.pallas_call(
        functools.partial(_ple_kernel, final=final),
        grid=(t // tm,),
        in_specs=[row(D_MODEL), pl.BlockSpec((1, tm, PLE_DIM), lambda i: (layer, i, 0)),
                  full(g), full(wg), full(wp), full(fg)],
        out_specs=row(D_MODEL),
        out_shape=jax.ShapeDtypeStruct((t, D_MODEL), F32),
        compiler_params=_cparams(("parallel",)),
        name="ple",
    )(h, p, g, wg, wp, fg)


def _swap_halves(w):
    half = w.shape[-1] // 2
    return jnp.concatenate([w[..., half:], w[..., :half]], axis=-1)


def _pad_cols(w, left, total):
    return jnp.pad(w, ((0, 0),) * (w.ndim - 1) + ((left, total - left - w.shape[-1]),))


def _arrange_w_in(w):
    sizes = (512, 512, SSD_XBC, SSD_HEADS, MLA_Q_RANK, MLA_KV_RANK, MLA_ROPE)
    pts = np.cumsum(sizes)[:-1].tolist()
    w_conv, w_z, w_xbc, w_dt, w_cq, w_ckv, w_kr = jnp.split(w, pts, axis=-1)
    seg_kr = _pad_cols(w_kr, ROPE_LO, LANES)
    seg_dtr = _pad_cols(w_dt, 0, LANES) + _pad_cols(_swap_halves(w_kr), ROPE_LO, LANES)
    return jnp.concatenate([w_conv, w_z, w_xbc, w_cq, w_ckv, seg_kr, seg_dtr], axis=-1).astype(BF16)


def _arrange_w_uq(w):
    main, rot = [], []
    for h in range(MLA_HEADS):
        wh = w[:, h * (MLA_NOPE + MLA_ROPE):(h + 1) * (MLA_NOPE + MLA_ROPE)]
        main.append(_pad_cols(wh, 0, HEAD_PAD))
        rot.append(_pad_cols(_swap_halves(wh[:, MLA_NOPE:]), ROPE_LO, HEAD_PAD))
    return jnp.concatenate(main + rot, axis=1).astype(BF16)


def _arrange_w_ukv(w):
    ks, vs = [], []
    for h in range(MLA_HEADS):
        wh = w[:, h * (MLA_NOPE + MLA_V):(h + 1) * (MLA_NOPE + MLA_V)]
        ks.append(_pad_cols(wh[:, :MLA_NOPE], 0, HEAD_PAD))
        vs.append(wh[:, MLA_NOPE:])
    return jnp.concatenate(ks, axis=1).astype(BF16), jnp.concatenate(vs, axis=1).T.astype(BF16)


def _row(v, width=None):
    v = v.reshape(1, -1).astype(F32)
    if width is not None:
        v = jnp.pad(v, ((0, 0), (0, width - v.shape[1])))
    return v


def _pick(n, prefs):
    for c in prefs:
        if n % c == 0:
            return c
    return n


def kernel(x, p, positions, attn_norm_g, w_in, conv_dw_w, conv_dw_b, conv_ln_g, conv_ln_b, ssd_conv_w, ssd_conv_b, ssd_dt_bias, ssd_a_log, ssd_d, ssd_norm_g, mla_q_norm_g, mla_w_uq, mla_kv_norm_g, mla_w_ukv, w_out, ffn_norm_g, dense_w_gate, dense_w_up, dense_w_down, moe_router, moe_w_gate, moe_w_up, moe_w_down, ple_norm_g, ple_w_gate, ple_w_proj, final_norm_g):
    bsz, s, _ = x.shape
    t = bsz * s
    tm_row = _pick(t, (512, 256, 128))
    tm_ffn = _pick(t, (1024, 512, 256, 128))
    fc = 512
    tc = _pick(s, (256, 128))
    ts = _pick(s, (512, 256, 128))
    tm_moe = _pick(t, (1024, 512, 256, 128))
    tm_tok = _pick(t, (256, 128))
    n_slots = 2 * t + N_EXPERTS * tm_moe

    inv = ROPE_BASE ** (-jnp.arange(0, MLA_ROPE, 2, dtype=F32) / MLA_ROPE)
    inv128 = _pad_cols(jnp.concatenate([inv, inv])[None, :], ROPE_LO, LANES)
    pos128 = jnp.broadcast_to(positions.astype(F32).reshape(t, 1), (t, LANES))
    ctab, stab = _rope_tables(pos128, inv128, tm_row)
    grp = np.arange(CONV_WIDTH) // (CONV_WIDTH // CONV_GROUPS)
    gmean = jnp.asarray((grp[:, None] == grp[None, :]) / (CONV_WIDTH // CONV_GROUPS), BF16)
    tril = jnp.asarray(np.tril(np.ones((SSD_CHUNK, SSD_CHUNK))), BF16)
    hd = np.arange(SSD_WIDTH) // SSD_HEADDIM
    emat = jnp.asarray(np.arange(LANES)[:, None] == hd[None, :], BF16)
    ltri = jnp.asarray(np.tril(np.ones((tm_tok, tm_tok)), -1), BF16)
    upper = jnp.asarray(np.triu(np.ones((LANES, LANES)), 1), BF16)

    w_in_all = _arrange_w_in(w_in)
    p_all = p.reshape(DEPTH, t, PLE_DIM)
    dense_wg, dense_wu, dense_wd = dense_w_gate, dense_w_up, dense_w_down
    moe_wg, moe_wu, moe_wd = moe_w_gate, moe_w_up, moe_w_down

    h = x.reshape(t, D_MODEL)
    for i in range(DEPTH):
        u_conv, z, xbc, dtm, q, k, vt = _inproj(
            h, _row(attn_norm_g[i]), w_in_all, i, ctab, stab,
            _row(mla_q_norm_g[i]), _arrange_w_uq(mla_w_uq[i]),
            _row(mla_kv_norm_g[i]), *_arrange_w_ukv(mla_w_ukv[i]), tm_row)
        y_conv = _conformer_conv(
            u_conv.reshape(bsz, s, -1), jnp.pad(conv_dw_w[i], ((0, 1), (0, 0))), _row(conv_dw_b[i]),
            _row(conv_ln_g[i]), _row(conv_ln_b[i]), gmean, ts)
        y_ssd = _ssd(
            xbc.reshape(bsz, s, -1), z.reshape(bsz, s, -1), dtm.reshape(bsz, s, -1),
            jnp.pad(ssd_conv_w[i], ((0, 8 - SSD_CONV), (0, 0))), _row(ssd_conv_b[i]),
            _row(ssd_dt_bias[i], LANES), _row(ssd_a_log[i], LANES),
            _row(jnp.repeat(ssd_d[i], SSD_HEADDIM)), _row(ssd_norm_g[i]), tril, emat, ts)
        ym = _attention(q, k, vt, bsz, tc)
        yc, ys = y_conv.reshape(t, -1), y_ssd.reshape(t, -1)
        wo = w_out[i].astype(BF16)
        j = i // 2
        if i % 2 == 0:
            h2 = _outproj_ffn_dense(yc, ys, ym, h, wo, _row(ffn_norm_g[i]), dense_wg, dense_wu, dense_wd, j,
                                    tm_ffn, fc)
        else:
            router = _pad_cols(moe_router[j], 0, LANES).astype(BF16)
            h1, hn, meta, tcnt, cnt = _outproj_moe(yc, ys, ym, h, wo, _row(ffn_norm_g[i]), tm_tok, router)
            counts = cnt[0, :N_EXPERTS].astype(jnp.int32)
            padded = ((counts + tm_moe - 1) // tm_moe) * tm_moe
            pends = jnp.cumsum(padded)
            pstarts = pends - padded
            before = tcnt[:, 0, :N_EXPERTS].astype(jnp.int32)
            n_run = jnp.concatenate([before[1:], counts[None, :]], axis=0) - before
            lo_run = jnp.cumsum(n_run, axis=1) - n_run
            g_run = pstarts[None, :] + before
            runs = (n_run.reshape(-1), lo_run.reshape(-1), g_run.reshape(-1))
            n_tiles = n_slots // tm_moe
            n_used = (pends[-1] // tm_moe).astype(jnp.int32)
            tile_start = jnp.arange(n_tiles, dtype=jnp.int32) * tm_moe
            tile_exp = jnp.minimum(jnp.sum(pends[None, :] <= tile_start[:, None], axis=1), N_EXPERTS - 1)
            tile_exp = tile_exp.astype(jnp.int32)
            n_valid = jnp.clip((pstarts + counts)[tile_exp] - tile_start, 0, tm_moe)
            tile_exp = jnp.where(jnp.arange(n_tiles) < n_used, tile_exp, tile_exp[jnp.maximum(n_used - 1, 0)])
            pad_start = jnp.concatenate([pstarts + counts, pends[-1:]])
            pad_len = jnp.concatenate([padded - counts, n_slots - pends[-1:]])
            xs, gs = _dispatch(*runs, pad_start, pad_len, hn, meta, ltri, upper, n_slots, tm_tok, tm_moe)
            ye = _moe_ffn(tile_exp, n_used.reshape(1), n_valid, xs, moe_wg, moe_wu, moe_wd, j, tm_moe, fc)
            h2 = _combine(*runs, h1, meta, gs, ltri, upper, ye, tm_tok)
        h = _ple(h2, p_all, i, _row(ple_norm_g[i]), ple_w_gate[i].astype(BF16),
                 ple_w_proj[i].astype(BF16), _row(final_norm_g), tm_row, final=(i == DEPTH - 1))
    return h.reshape(bsz, s, D_MODEL)
```

```python
import functools

import numpy as np
import jax
import jax.numpy as jnp
from jax import lax
from jax.experimental import pallas as pl
from jax.experimental.pallas import tpu as pltpu

F32 = jnp.float32
BF16 = jnp.bfloat16

D_MODEL = 1024
DEPTH = 4
PLE_DIM = 256
CONV_WIDTH = 256
CONV_GROUPS = 4
CONV_KERNEL = 31
SSD_WIDTH = 512
SSD_HEADDIM = 64
SSD_HEADS = 8
SSD_NGROUPS = 2
SSD_STATE = 128
SSD_CONV = 4
SSD_CHUNK = 128
MLA_HEADS = 4
MLA_NOPE = 64
MLA_ROPE = 32
MLA_V = 64
MLA_Q_RANK = 256
MLA_KV_RANK = 128
ROPE_BASE = 10000.0
D_FF = 3584
N_EXPERTS = 8
RMS_EPS = 1e-6
LN_EPS = 1e-5

LANES = 128
HEAD_PAD = 128
ROPE_LO = MLA_NOPE
VMEM_LIMIT = 48 * 1024 * 1024

C_CONV = 0
C_Z = 512
C_XBC = 1024
C_CQ = 2048
C_CKV = 2304
C_KR = 2432
C_DTR = 2560
IN_COLS_PAD = 2688


def _cparams(sem):
    return pltpu.CompilerParams(dimension_semantics=sem, vmem_limit_bytes=VMEM_LIMIT)


def _dot(a, b):
    return jnp.dot(a, b, preferred_element_type=F32)


def _dot_nt(a, b):
    return lax.dot_general(a, b, (((1,), (1,)), ((), ())), preferred_element_type=F32)


def _split3(a):
    a1 = a.astype(BF16)
    r1 = a - a1.astype(F32)
    a2 = r1.astype(BF16)
    a3 = (r1 - a2.astype(F32)).astype(BF16)
    return a1, a2, a3


def _dot_f32_lhs(a, m):
    a1, a2, a3 = _split3(a)
    return _dot(a1, m) + _dot(a2, m) + _dot(a3, m)


def _dot_f32_rhs(m, b):
    b1, b2, b3 = _split3(b)
    return _dot(m, b1) + _dot(m, b2) + _dot(m, b3)


def _rms(x, g, eps=RMS_EPS):
    return x * lax.rsqrt(jnp.mean(x * x, axis=-1, keepdims=True) + eps) * g


def _sigmoid(x):
    return 1.0 / (1.0 + jnp.exp(-x))


def _silu(x):
    return x * _sigmoid(x)


def _rope_kernel(pos_ref, inv_ref, c_ref, s_ref):
    ang = pos_ref[...] * inv_ref[...]
    lane = lax.broadcasted_iota(jnp.int32, ang.shape, 1)
    in_rope = (lane >= ROPE_LO) & (lane < ROPE_LO + MLA_ROPE)
    first_half = lane < ROPE_LO + MLA_ROPE // 2
    cos = jnp.cos(ang)
    sin = jnp.sin(ang)
    c_ref[...] = jnp.where(in_rope, cos, jnp.where(lane < ROPE_LO, 1.0, 0.0))
    s_ref[...] = jnp.where(in_rope, jnp.where(first_half, -sin, sin), 0.0)


def _rope_tables(pos128, inv128, tm):
    t = pos128.shape[0]
    return pl.pallas_call(
        _rope_kernel,
        grid=(t // tm,),
        in_specs=[pl.BlockSpec((tm, LANES), lambda i: (i, 0)),
                  pl.BlockSpec((1, LANES), lambda i: (0, 0))],
        out_specs=[pl.BlockSpec((tm, LANES), lambda i: (i, 0))] * 2,
        out_shape=[jax.ShapeDtypeStruct((t, LANES), F32)] * 2,
        compiler_params=_cparams(("parallel",)),
        name="rope_tables",
    )(pos128, inv128)


def _inproj_kernel(h_ref, g_ref, w_ref, c_ref, s_ref, gq_ref, wq_ref, gkv_ref, wk_ref, wvt_ref,
                   oconv_ref, oz_ref, oxbc_ref, odt_ref, oq_ref, ok_ref, ovt_ref):
    xn = _rms(h_ref[...], g_ref[...]).astype(BF16)
    oconv_ref[...] = _dot(xn, w_ref[0, :, C_CONV:C_Z])
    oz_ref[...] = _dot(xn, w_ref[0, :, C_Z:C_XBC])
    oxbc_ref[...] = _dot(xn, w_ref[0, :, C_XBC:C_CQ])
    cq = _dot(xn, w_ref[0, :, C_CQ:C_CKV])
    ckv = _dot(xn, w_ref[0, :, C_CKV:C_KR])
    kr = _dot(xn, w_ref[0, :, C_KR:C_DTR])
    dtr = _dot(xn, w_ref[0, :, C_DTR:IN_COLS_PAD])
    odt_ref[...] = dtr
    c = c_ref[...]
    s = s_ref[...]
    c4 = jnp.concatenate([c] * MLA_HEADS, axis=1)
    s4 = jnp.concatenate([s] * MLA_HEADS, axis=1)
    qq = _dot(_rms(cq, gq_ref[...]).astype(BF16), wq_ref[...])
    nq = MLA_HEADS * HEAD_PAD
    scale = (MLA_NOPE + MLA_ROPE) ** -0.5 * np.log2(np.e)
    oq_ref[...] = ((qq[:, :nq] * c4 + qq[:, nq:] * s4) * scale).astype(BF16)
    ckvn = _rms(ckv, gkv_ref[...]).astype(BF16)
    kpe = kr * c + dtr * s
    ok_ref[...] = (_dot(ckvn, wk_ref[...]) + jnp.concatenate([kpe] * MLA_HEADS, axis=1)).astype(BF16)
    ovt_ref[...] = _dot_nt(wvt_ref[...], ckvn).astype(BF16)


def _inproj(h, g, w, layer, ctab, stab, gq, wq, gkv, wk, wvt, tm):
    t = h.shape[0]
    row = lambda n: pl.BlockSpec((tm, n), lambda i: (i, 0))
    full = lambda a: pl.BlockSpec(a.shape, lambda i: (0,) * a.ndim)
    nq = MLA_HEADS * HEAD_PAD
    nv = MLA_HEADS * MLA_V
    widths = (512, 512, 1024, LANES, nq, nq)
    dtypes = (F32, F32, F32, F32, BF16, BF16)
    return pl.pallas_call(
        _inproj_kernel,
        grid=(t // tm,),
        in_specs=[row(D_MODEL), full(g), pl.BlockSpec((1,) + w.shape[1:], lambda i: (layer, 0, 0)),
                  row(LANES), row(LANES), full(gq), full(wq), full(gkv), full(wk), full(wvt)],
        out_specs=[row(n) for n in widths] + [pl.BlockSpec((nv, tm), lambda i: (0, i))],
        out_shape=[jax.ShapeDtypeStruct((t, n), d) for n, d in zip(widths, dtypes)]
        + [jax.ShapeDtypeStruct((nv, t), BF16)],
        compiler_params=_cparams(("parallel",)),
        name="inproj",
    )(h, g, w, ctab, stab, gq, wq, gkv, wk, wvt)


CONV_HALO = 32
CONV_SUB = 64


def _conv_kernel(u_ref, w_ref, b_ref, lg_ref, lb_ref, gm_ref, o_ref, gbuf, shifted, *, tc):
    @pl.when(pl.program_id(1) == 0)
    def _():
        gbuf[0:CONV_HALO, :] = jnp.zeros((CONV_HALO, CONV_WIDTH), F32)

    u = u_ref[0]
    gbuf[CONV_HALO:CONV_HALO + tc, :] = u[:, :CONV_WIDTH] * _sigmoid(u[:, CONV_WIDTH:])
    gm = gm_ref[...]
    first = CONV_HALO - (CONV_KERNEL - 1)
    span = CONV_HALO + tc - 8
    for s in range(1, 8):
        shifted[s - 1, 0:span, :] = gbuf[s:s + span, :]
    for r0 in range(0, tc, CONV_SUB):
        acc = jnp.broadcast_to(b_ref[...], (CONV_SUB, CONV_WIDTH))
        for j in range(CONV_KERNEL):
            start = first + j + r0
            s, a = start % 8, start - start % 8
            assert a + CONV_SUB <= span or s == 0
            win = gbuf[a:a + CONV_SUB, :] if s == 0 else shifted[s - 1, a:a + CONV_SUB, :]
            acc = acc + w_ref[j:j + 1, :] * win
        mu = _dot_f32_lhs(acc, gm)
        d = acc - mu
        var = _dot_f32_lhs(d * d, gm)
        hn = d * lax.rsqrt(var + LN_EPS) * lg_ref[...] + lb_ref[...]
        o_ref[0, r0:r0 + CONV_SUB, :] = _silu(hn).astype(BF16)
    gbuf[0:CONV_HALO, :] = gbuf[tc:tc + CONV_HALO, :]


def _conformer_conv(u, w, b, lg, lb, gm, tc):
    bsz, s, _ = u.shape
    full = lambda a: pl.BlockSpec(a.shape, lambda i, j: (0,) * a.ndim)
    return pl.pallas_call(
        functools.partial(_conv_kernel, tc=tc),
        grid=(bsz, s // tc),
        in_specs=[pl.BlockSpec((1, tc, 2 * CONV_WIDTH), lambda i, j: (i, j, 0)),
                  full(w), full(b), full(lg), full(lb), full(gm)],
        out_specs=pl.BlockSpec((1, tc, CONV_WIDTH), lambda i, j: (i, j, 0)),
        out_shape=jax.ShapeDtypeStruct((bsz, s, CONV_WIDTH), BF16),
        scratch_shapes=[pltpu.VMEM((CONV_HALO + tc, CONV_WIDTH), F32),
                        pltpu.VMEM((7, CONV_HALO + tc, CONV_WIDTH), F32)],
        compiler_params=_cparams(("parallel", "arbitrary")),
        name="conformer_conv",
    )(u, w, b, lg, lb, gm)


SSD_HALO = 8
SSD_XBC = SSD_WIDTH + 2 * SSD_NGROUPS * SSD_STATE
GROUP_W = SSD_WIDTH // SSD_NGROUPS
HEADS_PER_GROUP = SSD_HEADS // SSD_NGROUPS


def _ssd_chunk(r0, cbuf, z_ref, dt_ref, cw_ref, cb_ref, dtb_ref, alog_ref, dsk_ref, ng_ref, tril_ref, exp_ref,
               o_ref, state):
    L = SSD_CHUNK
    first = SSD_HALO - (SSD_CONV - 1) + r0
    acc = jnp.broadcast_to(cb_ref[...], (L, SSD_XBC))
    for j in range(SSD_CONV):
        acc = acc + cw_ref[j:j + 1, :] * cbuf[first + j:first + j + L, :]
    xc = _silu(acc)
    xs = xc[:, :SSD_WIDTH]
    bm = xc[:, SSD_WIDTH:SSD_WIDTH + SSD_NGROUPS * SSD_STATE]
    cm = xc[:, SSD_WIDTH + SSD_NGROUPS * SSD_STATE:]

    lane = lax.broadcasted_iota(jnp.int32, (1, LANES), 1)
    v = dt_ref[0, r0:r0 + L, :] + dtb_ref[...]
    dt = jnp.maximum(v, 0.0) + jnp.log1p(jnp.exp(-jnp.abs(v)))
    a = jnp.where(lane < SSD_HEADS, -jnp.exp(alog_ref[...]), 0.0)
    cs = _dot_f32_rhs(tril_ref[...], dt * a)
    cs_t = cs.T
    cs_last = cs[L - 1:L, :]
    emat = exp_ref[...]
    dt_e = _dot_f32_lhs(dt, emat)
    ecs_e = _dot_f32_lhs(jnp.exp(cs), emat)
    ds_e = _dot_f32_lhs(jnp.exp(cs_last - cs), emat)
    cd_e = _dot_f32_lhs(jnp.broadcast_to(jnp.exp(cs_last), (8, LANES)), emat)[0:1, :]

    xd = xs * dt_e
    xdb = xd.astype(BF16)
    xds = (xd * ds_e).astype(BF16)
    rows = lax.broadcasted_iota(jnp.int32, (L, L), 0)
    cols = lax.broadcasted_iota(jnp.int32, (L, L), 1)
    causal = rows >= cols
    ys = []
    for g in range(SSD_NGROUPS):
        cmg = cm[:, g * SSD_STATE:(g + 1) * SSD_STATE].astype(BF16)
        bmg = bm[:, g * SSD_STATE:(g + 1) * SSD_STATE]
        cbm = _dot_nt(cmg, bmg.astype(BF16))
        yd = []
        for r in range(HEADS_PER_GROUP):
            h = g * HEADS_PER_GROUP + r
            seg = cs[:, h:h + 1] - cs_t[h:h + 1, :]
            dec = jnp.exp(jnp.where(causal, seg, -jnp.inf))
            mix = (cbm * dec).astype(BF16)
            yd.append(_dot(mix, xdb[:, h * SSD_HEADDIM:(h + 1) * SSD_HEADDIM]))
        gs = slice(g * GROUP_W, (g + 1) * GROUP_W)
        prev = state[g]
        y_off = _dot(cmg, prev.astype(BF16)) * ecs_e[:, gs]
        st_new = _dot(bmg.T.astype(BF16), xds[:, gs])
        state[g] = prev * cd_e[:, gs] + st_new
        ys.append(jnp.concatenate(yd, axis=1) + y_off)
    y = jnp.concatenate(ys, axis=1) + dsk_ref[...] * xs
    yg = y * _silu(z_ref[0, r0:r0 + L, :])
    outs = []
    for g in range(SSD_NGROUPS):
        ygg = yg[:, g * GROUP_W:(g + 1) * GROUP_W]
        outs.append(ygg * lax.rsqrt(jnp.mean(ygg * ygg, axis=-1, keepdims=True) + RMS_EPS))
    o_ref[0, r0:r0 + L, :] = (jnp.concatenate(outs, axis=1) * ng_ref[...]).astype(BF16)


def _ssd_kernel(xbc_ref, z_ref, dt_ref, cw_ref, cb_ref, dtb_ref, alog_ref, dsk_ref, ng_ref, tril_ref, exp_ref,
                o_ref, cbuf, state, *, rows):
    @pl.when(pl.program_id(1) == 0)
    def _():
        cbuf[0:SSD_HALO, :] = jnp.zeros((SSD_HALO, SSD_XBC), F32)
        state[...] = jnp.zeros(state.shape, F32)

    cbuf[SSD_HALO:SSD_HALO + rows, :] = xbc_ref[0]
    for r0 in range(0, rows, SSD_CHUNK):
        _ssd_chunk(r0, cbuf, z_ref, dt_ref, cw_ref, cb_ref, dtb_ref, alog_ref, dsk_ref, ng_ref, tril_ref, exp_ref,
                   o_ref, state)
    cbuf[0:SSD_HALO, :] = cbuf[rows:rows + SSD_HALO, :]


def _ssd(xbc, z, dtm, cw, cb, dtb, alog, dsk, ng, tril, emat, rows):
    bsz, s, _ = xbc.shape
    full = lambda a: pl.BlockSpec(a.shape, lambda i, j: (0,) * a.ndim)
    blk = lambda n: pl.BlockSpec((1, rows, n), lambda i, j: (i, j, 0))
    return pl.pallas_call(
        functools.partial(_ssd_kernel, rows=rows),
        grid=(bsz, s // rows),
        in_specs=[blk(SSD_XBC), blk(SSD_WIDTH), blk(LANES),
                  full(cw), full(cb), full(dtb), full(alog), full(dsk), full(ng), full(tril), full(emat)],
        out_specs=blk(SSD_WIDTH),
        out_shape=jax.ShapeDtypeStruct((bsz, s, SSD_WIDTH), BF16),
        scratch_shapes=[pltpu.VMEM((SSD_HALO + rows, SSD_XBC), F32),
                        pltpu.VMEM((SSD_NGROUPS, SSD_STATE, GROUP_W), F32)],
        compiler_params=_cparams(("parallel", "arbitrary")),
        name="ssd",
    )(xbc, z, dtm, cw, cb, dtb, alog, dsk, ng, tril, emat)


def _attn_kernel(q_ref, k_ref, vt_ref, o_ref, m_sc, l_sc, acc_sc, st0, st1, *, tq):
    i = pl.program_id(1)
    krow = lax.broadcasted_iota(jnp.int32, (tq, tq), 0)
    qcol = lax.broadcasted_iota(jnp.int32, (tq, tq), 1)
    causal = krow <= qcol
    ones = jnp.ones((16, tq), BF16)
    m_sc[...] = jnp.full(m_sc.shape, -jnp.inf, F32)
    l_sc[...] = jnp.zeros(l_sc.shape, F32)
    acc_sc[...] = jnp.zeros(acc_sc.shape, F32)

    def scores(j, st_ref):
        start = pl.multiple_of(j * tq, tq)
        for h in range(MLA_HEADS):
            hs = slice(h * HEAD_PAD, (h + 1) * HEAD_PAD)
            st_ref[h] = _dot_nt(k_ref[pl.ds(start, tq), hs], q_ref[:, hs])

    def update(j, st_ref, masked):
        start = pl.multiple_of(j * tq, tq)
        for h in range(MLA_HEADS):
            vs = slice(h * MLA_V, (h + 1) * MLA_V)
            st = st_ref[h]
            if masked:
                st = jnp.where(causal, st, -jnp.inf)
            m = m_sc[h:h + 1, :]
            m_new = jnp.maximum(m, jnp.max(st, axis=0, keepdims=True))
            p = jnp.exp2(st - m_new).astype(BF16)
            alpha = jnp.exp2(m - m_new)
            m_sc[h:h + 1, :] = m_new
            lhs = jnp.concatenate([vt_ref[vs, pl.ds(start, tq)], ones], axis=0)
            pv = _dot(lhs, p)
            l_sc[h:h + 1, :] = alpha * l_sc[h:h + 1, :] + pv[MLA_V:MLA_V + 1, :]
            acc_sc[vs, :] = alpha * acc_sc[vs, :] + pv[:MLA_V, :]

    def pair(jp, c):
        j0 = 2 * jp
        scores(j0 + 1, st1)
        update(j0, st0, False)
        scores(j0 + 2, st0)
        update(j0 + 1, st1, False)
        return c

    scores(0, st0)
    npairs = i // 2
    lax.fori_loop(0, npairs, pair, 0)

    @pl.when(i == 2 * npairs)
    def _():
        update(i, st0, True)

    @pl.when(i != 2 * npairs)
    def _():
        scores(i, st1)
        update(i - 1, st0, False)
        update(i, st1, True)

    outs = [acc_sc[h * MLA_V:(h + 1) * MLA_V, :] / l_sc[h:h + 1, :] for h in range(MLA_HEADS)]
    o_ref[...] = jnp.concatenate(outs, axis=0).T.astype(BF16)


def _attention(q, k, vt, bsz, tq):
    t = q.shape[0]
    s = t // bsz
    nblk = s // tq
    nq = MLA_HEADS * HEAD_PAD
    nv = MLA_HEADS * MLA_V
    return pl.pallas_call(
        functools.partial(_attn_kernel, tq=tq),
        grid=(bsz, nblk),
        in_specs=[pl.BlockSpec((tq, nq), lambda b, i: (b * nblk + i, 0)),
                  pl.BlockSpec((s, nq), lambda b, i: (b, 0)),
                  pl.BlockSpec((nv, s), lambda b, i: (0, b))],
        out_specs=pl.BlockSpec((tq, nv), lambda b, i: (b * nblk + i, 0)),
        out_shape=jax.ShapeDtypeStruct((t, nv), BF16),
        scratch_shapes=[pltpu.VMEM((8, tq), F32), pltpu.VMEM((8, tq), F32), pltpu.VMEM((nv, tq), F32),
                        pltpu.VMEM((MLA_HEADS, tq, tq), F32), pltpu.VMEM((MLA_HEADS, tq, tq), F32)],
        compiler_params=_cparams(("parallel", "parallel")),
        name="mla_attention",
    )(q, k, vt)


M_IDX0, M_IDX1, M_GATE0, M_GATE1 = range(4)


def _outproj_body(yc_ref, ys_ref, ym_ref, h_ref, w_ref, g_ref):
    acc = _dot(yc_ref[...], w_ref[0:CONV_WIDTH, :])
    acc = acc + _dot(ys_ref[...], w_ref[CONV_WIDTH:CONV_WIDTH + SSD_WIDTH, :])
    acc = acc + _dot(ym_ref[...], w_ref[CONV_WIDTH + SSD_WIDTH:, :])
    h1 = h_ref[...] + acc
    return h1, _rms(h1, g_ref[...])


def _outproj_moe_kernel(yc_ref, ys_ref, ym_ref, h_ref, w_ref, g_ref, r_ref,
                        h1_ref, hn_ref, meta_ref, tcnt_ref, cnt_ref):
    @pl.when(pl.program_id(0) == 0)
    def _():
        cnt_ref[...] = jnp.zeros(cnt_ref.shape, F32)

    h1, hn = _outproj_body(yc_ref, ys_ref, ym_ref, h_ref, w_ref, g_ref)
    h1_ref[...] = h1
    hnb = hn.astype(BF16)
    hn_ref[...] = hnb
    logits = _dot(hnb, r_ref[...])
    lane = lax.broadcasted_iota(jnp.int32, logits.shape, 1)
    lm = jnp.where(lane < N_EXPERTS, logits, -jnp.inf)
    m1 = jnp.max(lm, axis=-1, keepdims=True)
    i1 = jnp.min(jnp.where(lm == m1, lane, LANES), axis=-1, keepdims=True)
    lm2 = jnp.where(lane == i1, -jnp.inf, lm)
    m2 = jnp.max(lm2, axis=-1, keepdims=True)
    i2 = jnp.min(jnp.where(lm2 == m2, lane, LANES), axis=-1, keepdims=True)
    e = jnp.exp(m2 - m1)
    g1 = 1.0 / (1.0 + e)
    g2 = e / (1.0 + e)
    onehot = jnp.where((lane == i1) | (lane == i2), 1.0, 0.0)
    tcnt_ref[0] = cnt_ref[...]
    cnt_ref[...] = cnt_ref[...] + jnp.sum(onehot, axis=0, keepdims=True)
    meta = jnp.where(lane == M_IDX0, i1.astype(F32), 0.0)
    meta = jnp.where(lane == M_IDX1, i2.astype(F32), meta)
    meta = jnp.where(lane == M_GATE0, g1, meta)
    meta = jnp.where(lane == M_GATE1, g2, meta)
    meta_ref[...] = meta


def _outproj_moe(yc, ys, ym, h, w, g, tm, router):
    t = h.shape[0]
    row = lambda n: pl.BlockSpec((tm, n), lambda i: (i, 0))
    full = lambda a: pl.BlockSpec(a.shape, lambda i: (0,) * a.ndim)
    return pl.pallas_call(
        _outproj_moe_kernel,
        grid=(t // tm,),
        in_specs=[row(CONV_WIDTH), row(SSD_WIDTH), row(MLA_HEADS * MLA_V), row(D_MODEL), full(w), full(g),
                  full(router)],
        out_specs=[row(D_MODEL), row(D_MODEL), row(LANES),
                   pl.BlockSpec((1, 8, LANES), lambda i: (i, 0, 0)), pl.BlockSpec((8, LANES), lambda i: (0, 0))],
        out_shape=[jax.ShapeDtypeStruct((t, D_MODEL), F32), jax.ShapeDtypeStruct((t, D_MODEL), BF16),
                   jax.ShapeDtypeStruct((t, LANES), F32), jax.ShapeDtypeStruct((t // tm, 8, LANES), F32),
                   jax.ShapeDtypeStruct((8, LANES), F32)],
        compiler_params=_cparams(("arbitrary",)),
        name="outproj_moe",
    )(yc, ys, ym, h, w, g, router)


def _ple_update(h, p_ref, pg_ref, pwg_ref, pwp_ref, fg_ref, final):
    gate = _sigmoid(_dot(_rms(h, pg_ref[...]).astype(BF16), pwg_ref[...]))
    out = h + _dot(p_ref[0, 0].astype(BF16), pwp_ref[...]) * gate
    if final:
        out = _rms(out, fg_ref[...])
    return out


def _p_spec(tm, seq, layer):
    tps = seq // tm
    return pl.BlockSpec((1, 1, tm, PLE_DIM), lambda i, *_: (layer, i // tps, i % tps, 0))


def _ffn_kernel(yc_ref, ys_ref, ym_ref, h_ref, wo_ref, g_ref, wg_ref, wu_ref, wd_ref,
                p_ref, pg_ref, pwg_ref, pwp_ref, fg_ref, o_ref, hn_sc, *, final):
    f = pl.program_id(1)

    @pl.when(f == 0)
    def _():
        h1, hn = _outproj_body(yc_ref, ys_ref, ym_ref, h_ref, wo_ref, g_ref)
        o_ref[...] = h1
        hn_sc[...] = hn.astype(BF16)

    x = hn_sc[...]
    mid = (_silu(_dot(x, wg_ref[0].astype(BF16))) * _dot(x, wu_ref[0].astype(BF16))).astype(BF16)
    o_ref[...] += _dot(mid, wd_ref[0].astype(BF16))

    @pl.when(f == pl.num_programs(1) - 1)
    def _():
        o_ref[...] = _ple_update(o_ref[...], p_ref, pg_ref, pwg_ref, pwp_ref, fg_ref, final)


def _outproj_ffn_ple_dense(yc, ys, ym, h, wo, g, wg, wu, wd, p, pg, pwg, pwp, fg, layer, ffn_layer, seq, tm, fc, final):
    t = h.shape[0]
    row = lambda n: pl.BlockSpec((tm, n), lambda i, f: (i, 0))
    full = lambda a: pl.BlockSpec(a.shape, lambda i, f: (0,) * a.ndim)
    return pl.pallas_call(
        functools.partial(_ffn_kernel, final=final),
        grid=(t // tm, D_FF // fc),
        in_specs=[row(CONV_WIDTH), row(SSD_WIDTH), row(MLA_HEADS * MLA_V), row(D_MODEL), full(wo), full(g),
                  pl.BlockSpec((1, D_MODEL, fc), lambda i, f: (ffn_layer, 0, f)),
                  pl.BlockSpec((1, D_MODEL, fc), lambda i, f: (ffn_layer, 0, f)),
                  pl.BlockSpec((1, fc, D_MODEL), lambda i, f: (ffn_layer, f, 0)),
                  _p_spec(tm, seq, layer), full(pg), full(pwg), full(pwp), full(fg)],
        out_specs=row(D_MODEL),
        out_shape=jax.ShapeDtypeStruct((t, D_MODEL), F32),
        scratch_shapes=[pltpu.VMEM((tm, D_MODEL), BF16)],
        compiler_params=_cparams(("parallel", "arbitrary")),
        name="outproj_ffn_ple_dense",
    )(yc, ys, ym, h, wo, g, wg, wu, wd, p, pg, pwg, pwp, fg)


ROW_TILES = D_MODEL // LANES


def _rows_to_tiles(val):
    blocks = jnp.stack([val[:, c * LANES:(c + 1) * LANES] for c in range(ROW_TILES)], axis=0)
    return jnp.transpose(blocks, (1, 0, 2))


def _tiles_to_rows(val):
    blocks = jnp.transpose(val, (1, 0, 2))
    return jnp.concatenate([blocks[c] for c in range(ROW_TILES)], axis=1)


def _run_bits(tm):
    return [1 << b for b in range(tm.bit_length() - 1, -1, -1)]


def _run_copies(n, src_ref, src0, dst_ref, dst0, sem, tm, wait, src_step=1):
    off = 0
    for b in _run_bits(tm):
        part = n & b

        @pl.when(part != 0)
        def _(off=off, b=b):
            cp = pltpu.make_async_copy(src_ref.at[pl.ds(src0 + off * src_step, b)],
                                       dst_ref.at[pl.ds(dst0 + off, b)], sem)
            if wait:
                cp.wait()
            else:
                cp.start()

        off = off + part


def _sorted_positions(meta, ltri, upper):
    lane = lax.broadcasted_iota(jnp.int32, meta.shape, 1)
    i1 = meta[:, M_IDX0:M_IDX0 + 1].astype(jnp.int32)
    i2 = meta[:, M_IDX1:M_IDX1 + 1].astype(jnp.int32)
    onehot = jnp.where((lane == i1) | (lane == i2), 1.0, 0.0)
    before = _dot(ltri, onehot.astype(BF16))
    n = jnp.broadcast_to(jnp.sum(onehot, axis=0, keepdims=True), (8, LANES))
    loff = _dot_f32_lhs(n, upper)[0:1, :]
    pos = before + loff
    q1 = jnp.sum(jnp.where(lane == i1, pos, 0.0), axis=-1, keepdims=True)
    q2 = jnp.sum(jnp.where(lane == i2, pos, 0.0), axis=-1, keepdims=True)
    return q1, q2


def _dispatch_kernel(n_ref, lo_ref, g_ref, ps_ref, pn_ref, x_ref, meta_ref, ltri_ref, up_ref, xs_ref, gs_ref,
                     xbuf, zbuf, sems, zsem, *, tm):
    i = pl.program_id(0)
    nt = pl.num_programs(0)
    slot = i % 2
    zrows = zbuf.shape[0]

    def pads(wait):
        for e in range(N_EXPERTS):
            _run_copies(pn_ref[e], zbuf, 0, xs_ref, ps_ref[e], zsem, zrows, wait, src_step=0)
        tail0, tail_len = ps_ref[N_EXPERTS], pn_ref[N_EXPERTS]
        for c in range(2 * N_EXPERTS):

            @pl.when(c * zrows < tail_len)
            def _(c=c):
                cp = pltpu.make_async_copy(zbuf, xs_ref.at[pl.ds(tail0 + c * zrows, zrows)], zsem)
                if wait:
                    cp.wait()
                else:
                    cp.start()

    @pl.when(i == 0)
    def _():
        zbuf[...] = jnp.zeros(zbuf.shape, F32)
        pads(False)

    meta = meta_ref[...]
    q1, q2 = _sorted_positions(meta, ltri_ref[...], up_ref[...])
    lane = lax.broadcasted_iota(jnp.int32, meta.shape, 1)
    qmat = jnp.where(lane == 0, q1, jnp.where(lane == 1, q2, 0.0))
    qt = qmat.T
    srow = lax.broadcasted_iota(jnp.int32, (2 * tm, tm), 0).astype(F32)
    p1 = jnp.where(srow == qt[0:1, :], 1.0, 0.0).astype(BF16)
    p2 = jnp.where(srow == qt[1:2, :], 1.0, 0.0).astype(BF16)
    g1 = jnp.broadcast_to(meta[:, M_GATE0:M_GATE0 + 1], (tm, LANES))
    g2 = jnp.broadcast_to(meta[:, M_GATE1:M_GATE1 + 1], (tm, LANES))
    gs_ref[...] = _dot_f32_rhs(p1, g1) + _dot_f32_rhs(p2, g2)
    xbuf[slot] = _rows_to_tiles(_dot(p1 + p2, x_ref[...]))

    for e in range(N_EXPERTS):
        k = i * N_EXPERTS + e
        _run_copies(n_ref[k], xbuf.at[slot], lo_ref[k], xs_ref, g_ref[k], sems.at[slot], tm, False)

    def wait_tile(sl):
        pltpu.make_async_copy(xbuf.at[sl], xs_ref.at[pl.ds(0, 2 * tm)], sems.at[sl]).wait()

    @pl.when(i > 0)
    def _():
        wait_tile(1 - slot)

    @pl.when(i == nt - 1)
    def _():
        wait_tile(slot)
        pads(True)


def _dispatch(n_run, lo_run, g_run, pad_start, pad_len, hn, meta, ltri, upper, n_slots, tm, tm_moe):
    t = hn.shape[0]
    full = lambda a: pl.BlockSpec(a.shape, lambda i, *_: (0,) * a.ndim)
    grid_spec = pltpu.PrefetchScalarGridSpec(
        num_scalar_prefetch=5,
        grid=(t // tm,),
        in_specs=[pl.BlockSpec((tm, D_MODEL), lambda i, *_: (i, 0)),
                  pl.BlockSpec((tm, LANES), lambda i, *_: (i, 0)),
                  full(ltri), full(upper)],
        out_specs=[pl.BlockSpec(memory_space=pl.ANY), pl.BlockSpec((2 * tm, LANES), lambda i, *_: (i, 0))],
        scratch_shapes=[pltpu.VMEM((2, 2 * tm, ROW_TILES, LANES), F32),
                        pltpu.VMEM((tm_moe // 2, ROW_TILES, LANES), F32),
                        pltpu.SemaphoreType.DMA((2,)), pltpu.SemaphoreType.DMA],
    )
    return pl.pallas_call(
        functools.partial(_dispatch_kernel, tm=tm),
        grid_spec=grid_spec,
        out_shape=[jax.ShapeDtypeStruct((n_slots, ROW_TILES, LANES), F32),
                   jax.ShapeDtypeStruct((2 * t, LANES), F32)],
        compiler_params=_cparams(("arbitrary",)),
        name="moe_dispatch",
    )(n_run, lo_run, g_run, pad_start, pad_len, hn, meta, ltri, upper)


def _moe_ffn_kernel(texp_ref, nused_ref, nvalid_ref, x_ref, wg_ref, wu_ref, wd_ref, o_ref, xb, acc, *, tm):
    del texp_ref, nused_ref
    i = pl.program_id(0)
    f = pl.program_id(1)
    half = tm // 2

    @pl.when(f == 0)
    def _():
        xb[...] = _tiles_to_rows(x_ref[...]).astype(BF16)
        acc[...] = jnp.zeros(acc.shape, F32)

    def rows(n):
        x = xb[0:n, :]
        mid = (_silu(_dot(x, wg_ref[0, 0].astype(BF16))) * _dot(x, wu_ref[0, 0].astype(BF16))).astype(BF16)
        acc[0:n, :] += _dot(mid, wd_ref[0, 0].astype(BF16))

    nvalid = nvalid_ref[i]

    @pl.when(nvalid > half)
    def _():
        rows(tm)

    @pl.when((nvalid > 0) & (nvalid <= half))
    def _():
        rows(half)

    @pl.when(f == pl.num_programs(1) - 1)
    def _():
        o_ref[...] = _rows_to_tiles(acc[...])


def _moe_ffn(tile_exp, n_used, n_valid, xs, wg, wu, wd, layer, tm, fc):
    n_slots = xs.shape[0]
    nf = D_FF // fc

    def fsel(i, f, nu):
        return jnp.where(i < nu[0], f, nf - 1)

    grid_spec = pltpu.PrefetchScalarGridSpec(
        num_scalar_prefetch=3,
        grid=(n_slots // tm, nf),
        in_specs=[pl.BlockSpec((tm, ROW_TILES, LANES), lambda i, f, te, nu, nv: (jnp.minimum(i, nu[0] - 1), 0, 0)),
                  pl.BlockSpec((1, 1, D_MODEL, fc), lambda i, f, te, nu, nv: (layer, te[i], 0, fsel(i, f, nu))),
                  pl.BlockSpec((1, 1, D_MODEL, fc), lambda i, f, te, nu, nv: (layer, te[i], 0, fsel(i, f, nu))),
                  pl.BlockSpec((1, 1, fc, D_MODEL), lambda i, f, te, nu, nv: (layer, te[i], fsel(i, f, nu), 0))],
        out_specs=pl.BlockSpec((tm, ROW_TILES, LANES), lambda i, f, te, nu, nv: (i, 0, 0)),
        scratch_shapes=[pltpu.VMEM((tm, D_MODEL), BF16), pltpu.VMEM((tm, D_MODEL), F32)],
    )
    return pl.pallas_call(
        functools.partial(_moe_ffn_kernel, tm=tm),
        grid_spec=grid_spec,
        out_shape=jax.ShapeDtypeStruct((n_slots, ROW_TILES, LANES), F32),
        compiler_params=_cparams(("parallel", "arbitrary")),
        name="moe_ffn",
    )(tile_exp, n_used, n_valid, xs, wg, wu, wd)


def _combine_kernel(n_ref, lo_ref, g_ref, h1_ref, meta_ref, gs_ref, ltri_ref, up_ref,
                    p_ref, pg_ref, pwg_ref, pwp_ref, fg_ref, ye_ref, o_ref, ybuf, sems, *, tm, final):
    i = pl.program_id(0)
    nt = pl.num_programs(0)
    slot = i % 2

    def fetch(tile, sl):
        for e in range(N_EXPERTS):
            k = tile * N_EXPERTS + e
            _run_copies(n_ref[k], ye_ref, g_ref[k], ybuf.at[sl], lo_ref[k], sems.at[sl], tm, False)

    @pl.when(i == 0)
    def _():
        fetch(i, slot)

    @pl.when(i + 1 < nt)
    def _():
        fetch(i + 1, 1 - slot)

    q1, q2 = _sorted_positions(meta_ref[...], ltri_ref[...], up_ref[...])
    scol = lax.broadcasted_iota(jnp.int32, (tm, 2 * tm), 1).astype(F32)
    sel = jnp.where((scol == q1) | (scol == q2), 1.0, 0.0).astype(BF16)
    pltpu.make_async_copy(ye_ref.at[pl.ds(0, 2 * tm)], ybuf.at[slot], sems.at[slot]).wait()
    y = _tiles_to_rows(ybuf[slot]) * gs_ref[:, 0:1]
    hi = y.astype(BF16)
    lo = (y - hi.astype(F32)).astype(BF16)
    h2 = h1_ref[...] + _dot(sel, hi) + _dot(sel, lo)
    o_ref[...] = _ple_update(h2, p_ref, pg_ref, pwg_ref, pwp_ref, fg_ref, final)


def _combine_ple(n_run, lo_run, g_run, h1, meta, gs, ltri, upper, p, pg, pwg, pwp, fg, ye, layer, seq, tm, final):
    t = h1.shape[0]
    full = lambda a: pl.BlockSpec(a.shape, lambda i, *_: (0,) * a.ndim)
    grid_spec = pltpu.PrefetchScalarGridSpec(
        num_scalar_prefetch=3,
        grid=(t // tm,),
        in_specs=[pl.BlockSpec((tm, D_MODEL), lambda i, *_: (i, 0)),
                  pl.BlockSpec((tm, LANES), lambda i, *_: (i, 0)),
                  pl.BlockSpec((2 * tm, LANES), lambda i, *_: (i, 0)),
                  full(ltri), full(upper),
                  _p_spec(tm, seq, layer), full(pg), full(pwg), full(pwp), full(fg),
                  pl.BlockSpec(memory_space=pl.ANY)],
        out_specs=pl.BlockSpec((tm, D_MODEL), lambda i, *_: (i, 0)),
        scratch_shapes=[pltpu.VMEM((2, 2 * tm, ROW_TILES, LANES), F32), pltpu.SemaphoreType.DMA((2,))],
    )
    return pl.pallas_call(
        functools.partial(_combine_kernel, tm=tm, final=final),
        grid_spec=grid_spec,
        out_shape=jax.ShapeDtypeStruct((t, D_MODEL), F32),
        compiler_params=_cparams(("arbitrary",)),
        name="moe_combine_ple",
    )(n_run, lo_run, g_run, h1, meta, gs, ltri, upper, p, pg, pwg, pwp, fg, ye)


def _swap_halves(w):
    half = w.shape[-1] // 2
    return jnp.concatenate([w[..., half:], w[..., :half]], axis=-1)


def _pad_cols(w, left, total):
    return jnp.pad(w, ((0, 0),) * (w.ndim - 1) + ((left, total - left - w.shape[-1]),))


def _arrange_w_in(w):
    sizes = (512, 512, SSD_XBC, SSD_HEADS, MLA_Q_RANK, MLA_KV_RANK, MLA_ROPE)
    pts = np.cumsum(sizes)[:-1].tolist()
    w_conv, w_z, w_xbc, w_dt, w_cq, w_ckv, w_kr = jnp.split(w, pts, axis=-1)
    seg_kr = _pad_cols(w_kr, ROPE_LO, LANES)
    seg_dtr = _pad_cols(w_dt, 0, LANES) + _pad_cols(_swap_halves(w_kr), ROPE_LO, LANES)
    return jnp.concatenate([w_conv, w_z, w_xbc, w_cq, w_ckv, seg_kr, seg_dtr], axis=-1).astype(BF16)


def _arrange_w_uq(w):
    main, rot = [], []
    for h in range(MLA_HEADS):
        wh = w[:, h * (MLA_NOPE + MLA_ROPE):(h + 1) * (MLA_NOPE + MLA_ROPE)]
        main.append(_pad_cols(wh, 0, HEAD_PAD))
        rot.append(_pad_cols(_swap_halves(wh[:, MLA_NOPE:]), ROPE_LO, HEAD_PAD))
    return jnp.concatenate(main + rot, axis=1).astype(BF16)


def _arrange_w_ukv(w):
    ks, vs = [], []
    for h in range(MLA_HEADS):
        wh = w[:, h * (MLA_NOPE + MLA_V):(h + 1) * (MLA_NOPE + MLA_V)]
        ks.append(_pad_cols(wh[:, :MLA_NOPE], 0, HEAD_PAD))
        vs.append(wh[:, MLA_NOPE:])
    return jnp.concatenate(ks, axis=1).astype(BF16), jnp.concatenate(vs, axis=1).T.astype(BF16)


def _row(v, width=None):
    v = v.reshape(1, -1).astype(F32)
    if width is not None:
        v = jnp.pad(v, ((0, 0), (0, width - v.shape[1])))
    return v


def _pick(n, prefs):
    for c in prefs:
        if n % c == 0:
            return c
    return n


def kernel(x, p, positions, attn_norm_g, w_in, conv_dw_w, conv_dw_b, conv_ln_g, conv_ln_b, ssd_conv_w, ssd_conv_b, ssd_dt_bias, ssd_a_log, ssd_d, ssd_norm_g, mla_q_norm_g, mla_w_uq, mla_kv_norm_g, mla_w_ukv, w_out, ffn_norm_g, dense_w_gate, dense_w_up, dense_w_down, moe_router, moe_w_gate, moe_w_up, moe_w_down, ple_norm_g, ple_w_gate, ple_w_proj, final_norm_g):
    bsz, s, _ = x.shape
    t = bsz * s
    tm_row = _pick(t, (512, 256, 128))
    tm_ffn = _pick(s, (1024, 512, 256, 128))
    fc = 512
    tc = _pick(s, (256, 128))
    ts = _pick(s, (512, 256, 128))
    tm_moe = _pick(t, (1024, 512, 256, 128))
    tm_tok = _pick(s, (256, 128))
    n_slots = 2 * t + N_EXPERTS * tm_moe

    inv = ROPE_BASE ** (-jnp.arange(0, MLA_ROPE, 2, dtype=F32) / MLA_ROPE)
    inv128 = _pad_cols(jnp.concatenate([inv, inv])[None, :], ROPE_LO, LANES)
    pos128 = jnp.broadcast_to(positions.astype(F32).reshape(t, 1), (t, LANES))
    ctab, stab = _rope_tables(pos128, inv128, tm_row)
    grp = np.arange(CONV_WIDTH) // (CONV_WIDTH // CONV_GROUPS)
    gmean = jnp.asarray((grp[:, None] == grp[None, :]) / (CONV_WIDTH // CONV_GROUPS), BF16)
    tril = jnp.asarray(np.tril(np.ones((SSD_CHUNK, SSD_CHUNK))), BF16)
    hd = np.arange(SSD_WIDTH) // SSD_HEADDIM
    emat = jnp.asarray(np.arange(LANES)[:, None] == hd[None, :], BF16)
    ltri = jnp.asarray(np.tril(np.ones((tm_tok, tm_tok)), -1), BF16)
    upper = jnp.asarray(np.triu(np.ones((LANES, LANES)), 1), BF16)

    w_in_all = _arrange_w_in(w_in)
    dense_wg, dense_wu, dense_wd = dense_w_gate, dense_w_up, dense_w_down
    moe_wg, moe_wu, moe_wd = moe_w_gate, moe_w_up, moe_w_down

    h = x.reshape(t, D_MODEL)
    for i in range(DEPTH):
        u_conv, z, xbc, dtm, q, k, vt = _inproj(
            h, _row(attn_norm_g[i]), w_in_all, i, ctab, stab,
            _row(mla_q_norm_g[i]), _arrange_w_uq(mla_w_uq[i]),
            _row(mla_kv_norm_g[i]), *_arrange_w_ukv(mla_w_ukv[i]), tm_row)
        y_conv = _conformer_conv(
            u_conv.reshape(bsz, s, -1), jnp.pad(conv_dw_w[i], ((0, 1), (0, 0))), _row(conv_dw_b[i]),
            _row(conv_ln_g[i]), _row(conv_ln_b[i]), gmean, ts)
        y_ssd = _ssd(
            xbc.reshape(bsz, s, -1), z.reshape(bsz, s, -1), dtm.reshape(bsz, s, -1),
            jnp.pad(ssd_conv_w[i], ((0, 8 - SSD_CONV), (0, 0))), _row(ssd_conv_b[i]),
            _row(ssd_dt_bias[i], LANES), _row(ssd_a_log[i], LANES),
            _row(jnp.repeat(ssd_d[i], SSD_HEADDIM)), _row(ssd_norm_g[i]), tril, emat, ts)
        ym = _attention(q, k, vt, bsz, tc)
        yc, ys = y_conv.reshape(t, -1), y_ssd.reshape(t, -1)
        wo = w_out[i].astype(BF16)
        j = i // 2
        final = i == DEPTH - 1
        ple = (p, _row(ple_norm_g[i]), ple_w_gate[i].astype(BF16), ple_w_proj[i].astype(BF16), _row(final_norm_g))
        if i % 2 == 0:
            h = _outproj_ffn_ple_dense(yc, ys, ym, h, wo, _row(ffn_norm_g[i]), dense_wg, dense_wu, dense_wd, *ple,
                                       i, j, s, tm_ffn, fc, final)
        else:
            router = _pad_cols(moe_router[j], 0, LANES).astype(BF16)
            h1, hn, meta, tcnt, cnt = _outproj_moe(yc, ys, ym, h, wo, _row(ffn_norm_g[i]), tm_tok, router)
            counts = cnt[0, :N_EXPERTS].astype(jnp.int32)
            padded = ((counts + tm_moe - 1) // tm_moe) * tm_moe
            pends = jnp.cumsum(padded)
            pstarts = pends - padded
            before = tcnt[:, 0, :N_EXPERTS].astype(jnp.int32)
            n_run = jnp.concatenate([before[1:], counts[None, :]], axis=0) - before
            lo_run = jnp.cumsum(n_run, axis=1) - n_run
            g_run = pstarts[None, :] + before
            runs = (n_run.reshape(-1), lo_run.reshape(-1), g_run.reshape(-1))
            n_tiles = n_slots // tm_moe
            n_used = (pends[-1] // tm_moe).astype(jnp.int32)
            tile_start = jnp.arange(n_tiles, dtype=jnp.int32) * tm_moe
            tile_exp = jnp.minimum(jnp.sum(pends[None, :] <= tile_start[:, None], axis=1), N_EXPERTS - 1)
            tile_exp = tile_exp.astype(jnp.int32)
            n_valid = jnp.clip((pstarts + counts)[tile_exp] - tile_start, 0, tm_moe)
            tile_exp = jnp.where(jnp.arange(n_tiles) < n_used, tile_exp, tile_exp[jnp.maximum(n_used - 1, 0)])
            pad_start = jnp.concatenate([pstarts + counts, pends[-1:]])
            pad_len = jnp.concatenate([padded - counts, n_slots - pends[-1:]])
            xs, gs = _dispatch(*runs, pad_start, pad_len, hn, meta, ltri, upper, n_slots, tm_tok, tm_moe)
            ye = _moe_ffn(tile_exp, n_used.reshape(1), n_valid, xs, moe_wg, moe_wu, moe_wd, j, tm_moe, fc)
            h = _combine_ple(*runs, h1, meta, gs, ltri, upper, *ple, ye, i, s, tm_tok, final)
    return h.reshape(bsz, s, D_MODEL)
```

```python
import functools

import numpy as np
import jax
import jax.numpy as jnp
from jax import lax
from jax.experimental import pallas as pl
from jax.experimental.pallas import tpu as pltpu

F32 = jnp.float32
BF16 = jnp.bfloat16

D_MODEL = 1024
DEPTH = 4
PLE_DIM = 256
CONV_WIDTH = 256
CONV_GROUPS = 4
CONV_KERNEL = 31
SSD_WIDTH = 512
SSD_HEADDIM = 64
SSD_HEADS = 8
SSD_NGROUPS = 2
SSD_STATE = 128
SSD_CONV = 4
SSD_CHUNK = 128
MLA_HEADS = 4
MLA_NOPE = 64
MLA_ROPE = 32
MLA_V = 64
MLA_Q_RANK = 256
MLA_KV_RANK = 128
ROPE_BASE = 10000.0
D_FF = 3584
N_EXPERTS = 8
RMS_EPS = 1e-6
LN_EPS = 1e-5

LANES = 128
HEAD_PAD = 128
ROPE_LO = MLA_NOPE
VMEM_LIMIT = 48 * 1024 * 1024

C_CONV = 0
C_Z = 512
C_XBC = 1024
C_CQ = 2048
C_CKV = 2304
C_KR = 2432
C_DTR = 2560
IN_COLS_PAD = 2688


def _cparams(sem):
    return pltpu.CompilerParams(dimension_semantics=sem, vmem_limit_bytes=VMEM_LIMIT)


def _dot(a, b):
    return jnp.dot(a, b, preferred_element_type=F32)


def _dot_nt(a, b):
    return lax.dot_general(a, b, (((1,), (1,)), ((), ())), preferred_element_type=F32)


def _split3(a):
    a1 = a.astype(BF16)
    r1 = a - a1.astype(F32)
    a2 = r1.astype(BF16)
    a3 = (r1 - a2.astype(F32)).astype(BF16)
    return a1, a2, a3


def _dot_f32_lhs(a, m):
    a1, a2, a3 = _split3(a)
    return _dot(a1, m) + _dot(a2, m) + _dot(a3, m)


def _dot_f32_rhs(m, b):
    b1, b2, b3 = _split3(b)
    return _dot(m, b1) + _dot(m, b2) + _dot(m, b3)


def _rms(x, g, eps=RMS_EPS):
    return x * lax.rsqrt(jnp.mean(x * x, axis=-1, keepdims=True) + eps) * g


def _sigmoid(x):
    return 1.0 / (1.0 + jnp.exp(-x))


def _silu(x):
    return x * _sigmoid(x)


def _rope_kernel(pos_ref, inv_ref, c_ref, s_ref):
    ang = pos_ref[...] * inv_ref[...]
    lane = lax.broadcasted_iota(jnp.int32, ang.shape, 1)
    in_rope = (lane >= ROPE_LO) & (lane < ROPE_LO + MLA_ROPE)
    first_half = lane < ROPE_LO + MLA_ROPE // 2
    cos = jnp.cos(ang)
    sin = jnp.sin(ang)
    c_ref[...] = jnp.where(in_rope, cos, jnp.where(lane < ROPE_LO, 1.0, 0.0))
    s_ref[...] = jnp.where(in_rope, jnp.where(first_half, -sin, sin), 0.0)


def _rope_tables(pos128, inv128, tm):
    t = pos128.shape[0]
    return pl.pallas_call(
        _rope_kernel,
        grid=(t // tm,),
        in_specs=[pl.BlockSpec((tm, LANES), lambda i: (i, 0)),
                  pl.BlockSpec((1, LANES), lambda i: (0, 0))],
        out_specs=[pl.BlockSpec((tm, LANES), lambda i: (i, 0))] * 2,
        out_shape=[jax.ShapeDtypeStruct((t, LANES), F32)] * 2,
        compiler_params=_cparams(("parallel",)),
        name="rope_tables",
    )(pos128, inv128)


def _inproj_kernel(h_ref, g_ref, w_ref, c_ref, s_ref, gq_ref, wq_ref, gkv_ref, wk_ref, wvt_ref,
                   oconv_ref, oz_ref, oxbc_ref, odt_ref, oq_ref, ok_ref, ovt_ref):
    xn = _rms(h_ref[...], g_ref[...]).astype(BF16)
    oconv_ref[...] = _dot(xn, w_ref[0, :, C_CONV:C_Z])
    oz_ref[...] = _dot(xn, w_ref[0, :, C_Z:C_XBC])
    oxbc_ref[...] = _dot(xn, w_ref[0, :, C_XBC:C_CQ])
    cq = _dot(xn, w_ref[0, :, C_CQ:C_CKV])
    ckv = _dot(xn, w_ref[0, :, C_CKV:C_KR])
    kr = _dot(xn, w_ref[0, :, C_KR:C_DTR])
    dtr = _dot(xn, w_ref[0, :, C_DTR:IN_COLS_PAD])
    odt_ref[...] = dtr
    c = c_ref[...]
    s = s_ref[...]
    c4 = jnp.concatenate([c] * MLA_HEADS, axis=1)
    s4 = jnp.concatenate([s] * MLA_HEADS, axis=1)
    qq = _dot(_rms(cq, gq_ref[...]).astype(BF16), wq_ref[...])
    nq = MLA_HEADS * HEAD_PAD
    scale = (MLA_NOPE + MLA_ROPE) ** -0.5 * np.log2(np.e)
    oq_ref[...] = ((qq[:, :nq] * c4 + qq[:, nq:] * s4) * scale).astype(BF16)
    ckvn = _rms(ckv, gkv_ref[...]).astype(BF16)
    kpe = kr * c + dtr * s
    ok_ref[...] = (_dot(ckvn, wk_ref[...]) + jnp.concatenate([kpe] * MLA_HEADS, axis=1)).astype(BF16)
    ovt_ref[...] = _dot_nt(wvt_ref[...], ckvn).astype(BF16)


def _inproj(h, g, w, layer, ctab, stab, gq, wq, gkv, wk, wvt, tm):
    t = h.shape[0]
    row = lambda n: pl.BlockSpec((tm, n), lambda i: (i, 0))
    full = lambda a: pl.BlockSpec(a.shape, lambda i: (0,) * a.ndim)
    nq = MLA_HEADS * HEAD_PAD
    nv = MLA_HEADS * MLA_V
    widths = (512, 512, 1024, LANES, nq, nq)
    dtypes = (F32, F32, F32, F32, BF16, BF16)
    return pl.pallas_call(
        _inproj_kernel,
        grid=(t // tm,),
        in_specs=[row(D_MODEL), full(g), pl.BlockSpec((1,) + w.shape[1:], lambda i: (layer, 0, 0)),
                  row(LANES), row(LANES), full(gq), full(wq), full(gkv), full(wk), full(wvt)],
        out_specs=[row(n) for n in widths] + [pl.BlockSpec((nv, tm), lambda i: (0, i))],
        out_shape=[jax.ShapeDtypeStruct((t, n), d) for n, d in zip(widths, dtypes)]
        + [jax.ShapeDtypeStruct((nv, t), BF16)],
        compiler_params=_cparams(("parallel",)),
        name="inproj",
    )(h, g, w, ctab, stab, gq, wq, gkv, wk, wvt)


CONV_HALO = 32
CONV_SUB = 64


def _conv_kernel(u_ref, w_ref, b_ref, lg_ref, lb_ref, gm_ref, o_ref, gbuf, shifted, *, tc):
    @pl.when(pl.program_id(1) == 0)
    def _():
        gbuf[0:CONV_HALO, :] = jnp.zeros((CONV_HALO, CONV_WIDTH), F32)

    u = u_ref[0]
    gbuf[CONV_HALO:CONV_HALO + tc, :] = u[:, :CONV_WIDTH] * _sigmoid(u[:, CONV_WIDTH:])
    gm = gm_ref[...]
    first = CONV_HALO - (CONV_KERNEL - 1)
    span = CONV_HALO + tc - 8
    for s in range(1, 8):
        shifted[s - 1, 0:span, :] = gbuf[s:s + span, :]
    for r0 in range(0, tc, CONV_SUB):
        acc = jnp.broadcast_to(b_ref[...], (CONV_SUB, CONV_WIDTH))
        for j in range(CONV_KERNEL):
            start = first + j + r0
            s, a = start % 8, start - start % 8
            assert a + CONV_SUB <= span or s == 0
            win = gbuf[a:a + CONV_SUB, :] if s == 0 else shifted[s - 1, a:a + CONV_SUB, :]
            acc = acc + w_ref[j:j + 1, :] * win
        mu = _dot_f32_lhs(acc, gm)
        d = acc - mu
        var = _dot_f32_lhs(d * d, gm)
        hn = d * lax.rsqrt(var + LN_EPS) * lg_ref[...] + lb_ref[...]
        o_ref[0, r0:r0 + CONV_SUB, :] = _silu(hn).astype(BF16)
    gbuf[0:CONV_HALO, :] = gbuf[tc:tc + CONV_HALO, :]


def _conformer_conv(u, w, b, lg, lb, gm, tc):
    bsz, s, _ = u.shape
    full = lambda a: pl.BlockSpec(a.shape, lambda i, j: (0,) * a.ndim)
    return pl.pallas_call(
        functools.partial(_conv_kernel, tc=tc),
        grid=(bsz, s // tc),
        in_specs=[pl.BlockSpec((1, tc, 2 * CONV_WIDTH), lambda i, j: (i, j, 0)),
                  full(w), full(b), full(lg), full(lb), full(gm)],
        out_specs=pl.BlockSpec((1, tc, CONV_WIDTH), lambda i, j: (i, j, 0)),
        out_shape=jax.ShapeDtypeStruct((bsz, s, CONV_WIDTH), BF16),
        scratch_shapes=[pltpu.VMEM((CONV_HALO + tc, CONV_WIDTH), F32),
                        pltpu.VMEM((7, CONV_HALO + tc, CONV_WIDTH), F32)],
        compiler_params=_cparams(("parallel", "arbitrary")),
        name="conformer_conv",
    )(u, w, b, lg, lb, gm)


SSD_HALO = 8
SSD_XBC = SSD_WIDTH + 2 * SSD_NGROUPS * SSD_STATE
GROUP_W = SSD_WIDTH // SSD_NGROUPS
HEADS_PER_GROUP = SSD_HEADS // SSD_NGROUPS


def _ssd_chunk(r0, cbuf, z_ref, dt_ref, cw_ref, cb_ref, dtb_ref, alog_ref, dsk_ref, ng_ref, tril_ref, exp_ref,
               o_ref, state):
    L = SSD_CHUNK
    first = SSD_HALO - (SSD_CONV - 1) + r0
    acc = jnp.broadcast_to(cb_ref[...], (L, SSD_XBC))
    for j in range(SSD_CONV):
        acc = acc + cw_ref[j:j + 1, :] * cbuf[first + j:first + j + L, :]
    xc = _silu(acc)
    xs = xc[:, :SSD_WIDTH]
    bm = xc[:, SSD_WIDTH:SSD_WIDTH + SSD_NGROUPS * SSD_STATE]
    cm = xc[:, SSD_WIDTH + SSD_NGROUPS * SSD_STATE:]

    lane = lax.broadcasted_iota(jnp.int32, (1, LANES), 1)
    v = dt_ref[0, r0:r0 + L, :] + dtb_ref[...]
    dt = jnp.maximum(v, 0.0) + jnp.log1p(jnp.exp(-jnp.abs(v)))
    a = jnp.where(lane < SSD_HEADS, -jnp.exp(alog_ref[...]), 0.0)
    cs = _dot_f32_rhs(tril_ref[...], dt * a)
    cs_t = cs.T
    cs_last = cs[L - 1:L, :]
    emat = exp_ref[...]
    dt_e = _dot_f32_lhs(dt, emat)
    ecs_e = _dot_f32_lhs(jnp.exp(cs), emat)
    ds_e = _dot_f32_lhs(jnp.exp(cs_last - cs), emat)
    cd_e = _dot_f32_lhs(jnp.broadcast_to(jnp.exp(cs_last), (8, LANES)), emat)[0:1, :]

    xd = xs * dt_e
    xdb = xd.astype(BF16)
    xds = (xd * ds_e).astype(BF16)
    rows = lax.broadcasted_iota(jnp.int32, (L, L), 0)
    cols = lax.broadcasted_iota(jnp.int32, (L, L), 1)
    causal = rows >= cols
    ys = []
    for g in range(SSD_NGROUPS):
        cmg = cm[:, g * SSD_STATE:(g + 1) * SSD_STATE].astype(BF16)
        bmg = bm[:, g * SSD_STATE:(g + 1) * SSD_STATE]
        cbm = _dot_nt(cmg, bmg.astype(BF16))
        yd = []
        for r in range(HEADS_PER_GROUP):
            h = g * HEADS_PER_GROUP + r
            seg = cs[:, h:h + 1] - cs_t[h:h + 1, :]
            dec = jnp.exp(jnp.where(causal, seg, -jnp.inf))
            mix = (cbm * dec).astype(BF16)
            yd.append(_dot(mix, xdb[:, h * SSD_HEADDIM:(h + 1) * SSD_HEADDIM]))
        gs = slice(g * GROUP_W, (g + 1) * GROUP_W)
        prev = state[g]
        y_off = _dot(cmg, prev.astype(BF16)) * ecs_e[:, gs]
        st_new = _dot(bmg.T.astype(BF16), xds[:, gs])
        state[g] = prev * cd_e[:, gs] + st_new
        ys.append(jnp.concatenate(yd, axis=1) + y_off)
    y = jnp.concatenate(ys, axis=1) + dsk_ref[...] * xs
    yg = y * _silu(z_ref[0, r0:r0 + L, :])
    outs = []
    for g in range(SSD_NGROUPS):
        ygg = yg[:, g * GROUP_W:(g + 1) * GROUP_W]
        outs.append(ygg * lax.rsqrt(jnp.mean(ygg * ygg, axis=-1, keepdims=True) + RMS_EPS))
    o_ref[0, r0:r0 + L, :] = (jnp.concatenate(outs, axis=1) * ng_ref[...]).astype(BF16)


def _ssd_kernel(xbc_ref, z_ref, dt_ref, cw_ref, cb_ref, dtb_ref, alog_ref, dsk_ref, ng_ref, tril_ref, exp_ref,
                o_ref, cbuf, state, *, rows):
    @pl.when(pl.program_id(1) == 0)
    def _():
        cbuf[0:SSD_HALO, :] = jnp.zeros((SSD_HALO, SSD_XBC), F32)
        state[...] = jnp.zeros(state.shape, F32)

    cbuf[SSD_HALO:SSD_HALO + rows, :] = xbc_ref[0]
    for r0 in range(0, rows, SSD_CHUNK):
        _ssd_chunk(r0, cbuf, z_ref, dt_ref, cw_ref, cb_ref, dtb_ref, alog_ref, dsk_ref, ng_ref, tril_ref, exp_ref,
                   o_ref, state)
    cbuf[0:SSD_HALO, :] = cbuf[rows:rows + SSD_HALO, :]


def _ssd(xbc, z, dtm, cw, cb, dtb, alog, dsk, ng, tril, emat, rows):
    bsz, s, _ = xbc.shape
    full = lambda a: pl.BlockSpec(a.shape, lambda i, j: (0,) * a.ndim)
    blk = lambda n: pl.BlockSpec((1, rows, n), lambda i, j: (i, j, 0))
    return pl.pallas_call(
        functools.partial(_ssd_kernel, rows=rows),
        grid=(bsz, s // rows),
        in_specs=[blk(SSD_XBC), blk(SSD_WIDTH), blk(LANES),
                  full(cw), full(cb), full(dtb), full(alog), full(dsk), full(ng), full(tril), full(emat)],
        out_specs=blk(SSD_WIDTH),
        out_shape=jax.ShapeDtypeStruct((bsz, s, SSD_WIDTH), BF16),
        scratch_shapes=[pltpu.VMEM((SSD_HALO + rows, SSD_XBC), F32),
                        pltpu.VMEM((SSD_NGROUPS, SSD_STATE, GROUP_W), F32)],
        compiler_params=_cparams(("parallel", "arbitrary")),
        name="ssd",
    )(xbc, z, dtm, cw, cb, dtb, alog, dsk, ng, tril, emat)


def _attn_kernel(q_ref, k_ref, vt_ref, o_ref, m_sc, l_sc, acc_sc, st0, st1, *, tq):
    i = pl.program_id(1)
    krow = lax.broadcasted_iota(jnp.int32, (tq, tq), 0)
    qcol = lax.broadcasted_iota(jnp.int32, (tq, tq), 1)
    causal = krow <= qcol
    ones = jnp.ones((16, tq), BF16)
    m_sc[...] = jnp.full(m_sc.shape, -jnp.inf, F32)
    l_sc[...] = jnp.zeros(l_sc.shape, F32)
    acc_sc[...] = jnp.zeros(acc_sc.shape, F32)

    def scores(j, st_ref):
        start = pl.multiple_of(j * tq, tq)
        for h in range(MLA_HEADS):
            hs = slice(h * HEAD_PAD, (h + 1) * HEAD_PAD)
            st_ref[h] = _dot_nt(k_ref[pl.ds(start, tq), hs], q_ref[:, hs])

    def update(j, st_ref, masked):
        start = pl.multiple_of(j * tq, tq)
        for h in range(MLA_HEADS):
            vs = slice(h * MLA_V, (h + 1) * MLA_V)
            st = st_ref[h]
            if masked:
                st = jnp.where(causal, st, -jnp.inf)
            m = m_sc[h:h + 1, :]
            m_new = jnp.maximum(m, jnp.max(st, axis=0, keepdims=True))
            p = jnp.exp2(st - m_new).astype(BF16)
            alpha = jnp.exp2(m - m_new)
            m_sc[h:h + 1, :] = m_new
            lhs = jnp.concatenate([vt_ref[vs, pl.ds(start, tq)], ones], axis=0)
            pv = _dot(lhs, p)
            l_sc[h:h + 1, :] = alpha * l_sc[h:h + 1, :] + pv[MLA_V:MLA_V + 1, :]
            acc_sc[vs, :] = alpha * acc_sc[vs, :] + pv[:MLA_V, :]

    def pair(jp, c):
        j0 = 2 * jp
        scores(j0 + 1, st1)
        update(j0, st0, False)
        scores(j0 + 2, st0)
        update(j0 + 1, st1, False)
        return c

    scores(0, st0)
    npairs = i // 2
    lax.fori_loop(0, npairs, pair, 0)

    @pl.when(i == 2 * npairs)
    def _():
        update(i, st0, True)

    @pl.when(i != 2 * npairs)
    def _():
        scores(i, st1)
        update(i - 1, st0, False)
        update(i, st1, True)

    outs = [acc_sc[h * MLA_V:(h + 1) * MLA_V, :] / l_sc[h:h + 1, :] for h in range(MLA_HEADS)]
    o_ref[...] = jnp.concatenate(outs, axis=0).T.astype(BF16)


def _attention(q, k, vt, bsz, tq):
    t = q.shape[0]
    s = t // bsz
    nblk = s // tq
    nq = MLA_HEADS * HEAD_PAD
    nv = MLA_HEADS * MLA_V
    return pl.pallas_call(
        functools.partial(_attn_kernel, tq=tq),
        grid=(bsz, nblk),
        in_specs=[pl.BlockSpec((tq, nq), lambda b, i: (b * nblk + i, 0)),
                  pl.BlockSpec((s, nq), lambda b, i: (b, 0)),
                  pl.BlockSpec((nv, s), lambda b, i: (0, b))],
        out_specs=pl.BlockSpec((tq, nv), lambda b, i: (b * nblk + i, 0)),
        out_shape=jax.ShapeDtypeStruct((t, nv), BF16),
        scratch_shapes=[pltpu.VMEM((8, tq), F32), pltpu.VMEM((8, tq), F32), pltpu.VMEM((nv, tq), F32),
                        pltpu.VMEM((MLA_HEADS, tq, tq), F32), pltpu.VMEM((MLA_HEADS, tq, tq), F32)],
        compiler_params=_cparams(("parallel", "parallel")),
        name="mla_attention",
    )(q, k, vt)


M_IDX0, M_IDX1, M_GATE0, M_GATE1 = range(4)


def _outproj_body(yc, ys, ym, h, w_ref, g_ref):
    acc = _dot(yc, w_ref[0:CONV_WIDTH, :])
    acc = acc + _dot(ys, w_ref[CONV_WIDTH:CONV_WIDTH + SSD_WIDTH, :])
    acc = acc + _dot(ym, w_ref[CONV_WIDTH + SSD_WIDTH:, :])
    h1 = h + acc
    return h1, _rms(h1, g_ref[...])


def _outproj_moe_kernel(yc_ref, ys_ref, ym_ref, h_ref, w_ref, g_ref, r_ref,
                        h1_ref, hn_ref, meta_ref, tcnt_ref, cnt_ref):
    @pl.when(pl.program_id(0) == 0)
    def _():
        cnt_ref[...] = jnp.zeros(cnt_ref.shape, F32)

    h1, hn = _outproj_body(yc_ref[...], ys_ref[...], ym_ref[...], h_ref[...], w_ref, g_ref)
    h1_ref[...] = h1
    hnb = hn.astype(BF16)
    hn_ref[...] = hnb
    logits = _dot(hnb, r_ref[...])
    lane = lax.broadcasted_iota(jnp.int32, logits.shape, 1)
    lm = jnp.where(lane < N_EXPERTS, logits, -jnp.inf)
    m1 = jnp.max(lm, axis=-1, keepdims=True)
    i1 = jnp.min(jnp.where(lm == m1, lane, LANES), axis=-1, keepdims=True)
    lm2 = jnp.where(lane == i1, -jnp.inf, lm)
    m2 = jnp.max(lm2, axis=-1, keepdims=True)
    i2 = jnp.min(jnp.where(lm2 == m2, lane, LANES), axis=-1, keepdims=True)
    e = jnp.exp(m2 - m1)
    g1 = 1.0 / (1.0 + e)
    g2 = e / (1.0 + e)
    onehot = jnp.where((lane == i1) | (lane == i2), 1.0, 0.0)
    tcnt_ref[0] = cnt_ref[...]
    cnt_ref[...] = cnt_ref[...] + jnp.sum(onehot, axis=0, keepdims=True)
    meta = jnp.where(lane == M_IDX0, i1.astype(F32), 0.0)
    meta = jnp.where(lane == M_IDX1, i2.astype(F32), meta)
    meta = jnp.where(lane == M_GATE0, g1, meta)
    meta = jnp.where(lane == M_GATE1, g2, meta)
    meta_ref[...] = meta


def _outproj_moe(yc, ys, ym, h, w, g, tm, router):
    t = h.shape[0]
    row = lambda n: pl.BlockSpec((tm, n), lambda i: (i, 0))
    full = lambda a: pl.BlockSpec(a.shape, lambda i: (0,) * a.ndim)
    return pl.pallas_call(
        _outproj_moe_kernel,
        grid=(t // tm,),
        in_specs=[row(CONV_WIDTH), row(SSD_WIDTH), row(MLA_HEADS * MLA_V), row(D_MODEL), full(w), full(g),
                  full(router)],
        out_specs=[row(D_MODEL), row(D_MODEL), row(LANES),
                   pl.BlockSpec((1, 8, LANES), lambda i: (i, 0, 0)), pl.BlockSpec((8, LANES), lambda i: (0, 0))],
        out_shape=[jax.ShapeDtypeStruct((t, D_MODEL), F32), jax.ShapeDtypeStruct((t, D_MODEL), BF16),
                   jax.ShapeDtypeStruct((t, LANES), F32), jax.ShapeDtypeStruct((t // tm, 8, LANES), F32),
                   jax.ShapeDtypeStruct((8, LANES), F32)],
        compiler_params=_cparams(("arbitrary",)),
        name="outproj_moe",
    )(yc, ys, ym, h, w, g, router)


def _ple_update(h, p, pg_ref, pwg_ref, pwp_ref, fg_ref, final):
    gate = _sigmoid(_dot(_rms(h, pg_ref[...]).astype(BF16), pwg_ref[...]))
    out = h + _dot(p.astype(BF16), pwp_ref[...]) * gate
    if final:
        out = _rms(out, fg_ref[...])
    return out


def _p_spec(tm, seq, layer):
    tps = seq // tm
    return pl.BlockSpec((1, 1, tm, PLE_DIM), lambda i, *_: (layer, i // tps, i % tps, 0))


def _ffn_kernel(yc_ref, ys_ref, ym_ref, h_ref, wo_ref, g_ref, wg_ref, wu_ref, wd_ref,
                p_ref, pg_ref, pwg_ref, pwp_ref, fg_ref, o_ref, hn_sc, *, final):
    f = pl.program_id(1)
    half = hn_sc.shape[0] // 2
    halves = [slice(r0, r0 + half) for r0 in (0, half)]

    @pl.when(f == 0)
    def _():
        for rs in halves:
            h1, hn = _outproj_body(yc_ref[rs, :], ys_ref[rs, :], ym_ref[rs, :], h_ref[rs, :], wo_ref, g_ref)
            o_ref[rs, :] = h1
            hn_sc[rs, :] = hn.astype(BF16)

    wg, wu, wd = wg_ref[0].astype(BF16), wu_ref[0].astype(BF16), wd_ref[0].astype(BF16)
    for rs in halves:
        x = hn_sc[rs, :]
        mid = (_silu(_dot(x, wg)) * _dot(x, wu)).astype(BF16)
        o_ref[rs, :] += _dot(mid, wd)

    @pl.when(f == pl.num_programs(1) - 1)
    def _():
        for rs in halves:
            o_ref[rs, :] = _ple_update(o_ref[rs, :], p_ref[0, 0, rs, :], pg_ref, pwg_ref, pwp_ref, fg_ref, final)


def _outproj_ffn_ple_dense(yc, ys, ym, h, wo, g, wg, wu, wd, p, pg, pwg, pwp, fg, layer, ffn_layer, seq, tm, fc, final):
    t = h.shape[0]
    row = lambda n: pl.BlockSpec((tm, n), lambda i, f: (i, 0))
    full = lambda a: pl.BlockSpec(a.shape, lambda i, f: (0,) * a.ndim)
    return pl.pallas_call(
        functools.partial(_ffn_kernel, final=final),
        grid=(t // tm, D_FF // fc),
        in_specs=[row(CONV_WIDTH), row(SSD_WIDTH), row(MLA_HEADS * MLA_V), row(D_MODEL), full(wo), full(g),
                  pl.BlockSpec((1, D_MODEL, fc), lambda i, f: (ffn_layer, 0, f)),
                  pl.BlockSpec((1, D_MODEL, fc), lambda i, f: (ffn_layer, 0, f)),
                  pl.BlockSpec((1, fc, D_MODEL), lambda i, f: (ffn_layer, f, 0)),
                  _p_spec(tm, seq, layer), full(pg), full(pwg), full(pwp), full(fg)],
        out_specs=row(D_MODEL),
        out_shape=jax.ShapeDtypeStruct((t, D_MODEL), F32),
        scratch_shapes=[pltpu.VMEM((tm, D_MODEL), BF16)],
        compiler_params=_cparams(("parallel", "arbitrary")),
        name="outproj_ffn_ple_dense",
    )(yc, ys, ym, h, wo, g, wg, wu, wd, p, pg, pwg, pwp, fg)


ROW_TILES = D_MODEL // LANES


def _rows_to_tiles(val):
    blocks = jnp.stack([val[:, c * LANES:(c + 1) * LANES] for c in range(ROW_TILES)], axis=0)
    return jnp.transpose(blocks, (1, 0, 2))


def _tiles_to_rows(val):
    blocks = jnp.transpose(val, (1, 0, 2))
    return jnp.concatenate([blocks[c] for c in range(ROW_TILES)], axis=1)


def _run_bits(tm):
    return [1 << b for b in range(tm.bit_length() - 1, -1, -1)]


def _run_copies(n, src_ref, src0, dst_ref, dst0, sem, tm, wait, src_step=1):
    off = 0
    for b in _run_bits(tm):
        part = n & b

        @pl.when(part != 0)
        def _(off=off, b=b):
            cp = pltpu.make_async_copy(src_ref.at[pl.ds(src0 + off * src_step, b)],
                                       dst_ref.at[pl.ds(dst0 + off, b)], sem)
            if wait:
                cp.wait()
            else:
                cp.start()

        off = off + part


def _sorted_positions(meta, ltri, upper):
    lane = lax.broadcasted_iota(jnp.int32, meta.shape, 1)
    i1 = meta[:, M_IDX0:M_IDX0 + 1].astype(jnp.int32)
    i2 = meta[:, M_IDX1:M_IDX1 + 1].astype(jnp.int32)
    onehot = jnp.where((lane == i1) | (lane == i2), 1.0, 0.0)
    before = _dot(ltri, onehot.astype(BF16))
    n = jnp.broadcast_to(jnp.sum(onehot, axis=0, keepdims=True), (8, LANES))
    loff = _dot_f32_lhs(n, upper)[0:1, :]
    pos = before + loff
    q1 = jnp.sum(jnp.where(lane == i1, pos, 0.0), axis=-1, keepdims=True)
    q2 = jnp.sum(jnp.where(lane == i2, pos, 0.0), axis=-1, keepdims=True)
    return q1, q2


def _dispatch_kernel(n_ref, lo_ref, g_ref, ps_ref, pn_ref, x_ref, meta_ref, ltri_ref, up_ref, xs_ref, gs_ref,
                     xbuf, zbuf, sems, zsem, *, tm):
    i = pl.program_id(0)
    nt = pl.num_programs(0)
    slot = i % 2
    zrows = zbuf.shape[0]

    def pads(wait):
        for e in range(N_EXPERTS):
            _run_copies(pn_ref[e], zbuf, 0, xs_ref, ps_ref[e], zsem, zrows, wait, src_step=0)
        tail0, tail_len = ps_ref[N_EXPERTS], pn_ref[N_EXPERTS]
        for c in range(2 * N_EXPERTS):

            @pl.when(c * zrows < tail_len)
            def _(c=c):
                cp = pltpu.make_async_copy(zbuf, xs_ref.at[pl.ds(tail0 + c * zrows, zrows)], zsem)
                if wait:
                    cp.wait()
                else:
                    cp.start()

    @pl.when(i == 0)
    def _():
        zbuf[...] = jnp.zeros(zbuf.shape, F32)
        pads(False)

    meta = meta_ref[...]
    q1, q2 = _sorted_positions(meta, ltri_ref[...], up_ref[...])
    lane = lax.broadcasted_iota(jnp.int32, meta.shape, 1)
    qmat = jnp.where(lane == 0, q1, jnp.where(lane == 1, q2, 0.0))
    qt = qmat.T
    srow = lax.broadcasted_iota(jnp.int32, (2 * tm, tm), 0).astype(F32)
    p1 = jnp.where(srow == qt[0:1, :], 1.0, 0.0).astype(BF16)
    p2 = jnp.where(srow == qt[1:2, :], 1.0, 0.0).astype(BF16)
    g1 = jnp.broadcast_to(meta[:, M_GATE0:M_GATE0 + 1], (tm, LANES))
    g2 = jnp.broadcast_to(meta[:, M_GATE1:M_GATE1 + 1], (tm, LANES))
    gs_ref[...] = _dot_f32_rhs(p1, g1) + _dot_f32_rhs(p2, g2)
    xbuf[slot] = _rows_to_tiles(_dot(p1 + p2, x_ref[...]))

    for e in range(N_EXPERTS):
        k = i * N_EXPERTS + e
        _run_copies(n_ref[k], xbuf.at[slot], lo_ref[k], xs_ref, g_ref[k], sems.at[slot], tm, False)

    def wait_tile(sl):
        pltpu.make_async_copy(xbuf.at[sl], xs_ref.at[pl.ds(0, 2 * tm)], sems.at[sl]).wait()

    @pl.when(i > 0)
    def _():
        wait_tile(1 - slot)

    @pl.when(i == nt - 1)
    def _():
        wait_tile(slot)
        pads(True)


def _dispatch(n_run, lo_run, g_run, pad_start, pad_len, hn, meta, ltri, upper, n_slots, tm, tm_moe):
    t = hn.shape[0]
    full = lambda a: pl.BlockSpec(a.shape, lambda i, *_: (0,) * a.ndim)
    grid_spec = pltpu.PrefetchScalarGridSpec(
        num_scalar_prefetch=5,
        grid=(t // tm,),
        in_specs=[pl.BlockSpec((tm, D_MODEL), lambda i, *_: (i, 0)),
                  pl.BlockSpec((tm, LANES), lambda i, *_: (i, 0)),
                  full(ltri), full(upper)],
        out_specs=[pl.BlockSpec(memory_space=pl.ANY), pl.BlockSpec((2 * tm, LANES), lambda i, *_: (i, 0))],
        scratch_shapes=[pltpu.VMEM((2, 2 * tm, ROW_TILES, LANES), F32),
                        pltpu.VMEM((tm_moe // 2, ROW_TILES, LANES), F32),
                        pltpu.SemaphoreType.DMA((2,)), pltpu.SemaphoreType.DMA],
    )
    return pl.pallas_call(
        functools.partial(_dispatch_kernel, tm=tm),
        grid_spec=grid_spec,
        out_shape=[jax.ShapeDtypeStruct((n_slots, ROW_TILES, LANES), F32),
                   jax.ShapeDtypeStruct((2 * t, LANES), F32)],
        compiler_params=_cparams(("arbitrary",)),
        name="moe_dispatch",
    )(n_run, lo_run, g_run, pad_start, pad_len, hn, meta, ltri, upper)


def _moe_ffn_kernel(texp_ref, nused_ref, nvalid_ref, x_ref, wg_ref, wu_ref, wd_ref, o_ref, xb, acc, *, tm):
    del texp_ref, nused_ref
    i = pl.program_id(0)
    f = pl.program_id(1)
    half = tm // 2

    @pl.when(f == 0)
    def _():
        xb[...] = _tiles_to_rows(x_ref[...]).astype(BF16)
        acc[...] = jnp.zeros(acc.shape, F32)

    def rows(n):
        wg, wu, wd = wg_ref[0, 0].astype(BF16), wu_ref[0, 0].astype(BF16), wd_ref[0, 0].astype(BF16)
        step = min(n, half)
        for r0 in range(0, n, step):
            x = xb[r0:r0 + step, :]
            mid = (_silu(_dot(x, wg)) * _dot(x, wu)).astype(BF16)
            acc[r0:r0 + step, :] += _dot(mid, wd)

    nvalid = nvalid_ref[i]
    quarter = half // 2

    @pl.when(nvalid > half)
    def _():
        rows(tm)

    @pl.when((nvalid > quarter) & (nvalid <= half))
    def _():
        rows(half)

    @pl.when((nvalid > 0) & (nvalid <= quarter))
    def _():
        rows(quarter)

    @pl.when(f == pl.num_programs(1) - 1)
    def _():
        o_ref[...] = _rows_to_tiles(acc[...])


def _moe_ffn(tile_exp, n_used, n_valid, xs, wg, wu, wd, layer, tm, fc):
    n_slots = xs.shape[0]
    nf = D_FF // fc

    def fsel(i, f, nu):
        return jnp.where(i < nu[0], f, nf - 1)

    grid_spec = pltpu.PrefetchScalarGridSpec(
        num_scalar_prefetch=3,
        grid=(n_slots // tm, nf),
        in_specs=[pl.BlockSpec((tm, ROW_TILES, LANES), lambda i, f, te, nu, nv: (jnp.minimum(i, nu[0] - 1), 0, 0)),
                  pl.BlockSpec((1, 1, D_MODEL, fc), lambda i, f, te, nu, nv: (layer, te[i], 0, fsel(i, f, nu))),
                  pl.BlockSpec((1, 1, D_MODEL, fc), lambda i, f, te, nu, nv: (layer, te[i], 0, fsel(i, f, nu))),
                  pl.BlockSpec((1, 1, fc, D_MODEL), lambda i, f, te, nu, nv: (layer, te[i], fsel(i, f, nu), 0))],
        out_specs=pl.BlockSpec((tm, ROW_TILES, LANES), lambda i, f, te, nu, nv: (i, 0, 0)),
        scratch_shapes=[pltpu.VMEM((tm, D_MODEL), BF16), pltpu.VMEM((tm, D_MODEL), F32)],
    )
    return pl.pallas_call(
        functools.partial(_moe_ffn_kernel, tm=tm),
        grid_spec=grid_spec,
        out_shape=jax.ShapeDtypeStruct((n_slots, ROW_TILES, LANES), F32),
        compiler_params=_cparams(("parallel", "arbitrary")),
        name="moe_ffn",
    )(tile_exp, n_used, n_valid, xs, wg, wu, wd)


def _combine_kernel(n_ref, lo_ref, g_ref, h1_ref, meta_ref, gs_ref, ltri_ref, up_ref,
                    p_ref, pg_ref, pwg_ref, pwp_ref, fg_ref, ye_ref, o_ref, ybuf, sems, *, tm, final):
    i = pl.program_id(0)
    nt = pl.num_programs(0)
    slot = i % 2

    def fetch(tile, sl):
        for e in range(N_EXPERTS):
            k = tile * N_EXPERTS + e
            _run_copies(n_ref[k], ye_ref, g_ref[k], ybuf.at[sl], lo_ref[k], sems.at[sl], tm, False)

    @pl.when(i == 0)
    def _():
        fetch(i, slot)

    @pl.when(i + 1 < nt)
    def _():
        fetch(i + 1, 1 - slot)

    q1, q2 = _sorted_positions(meta_ref[...], ltri_ref[...], up_ref[...])
    scol = lax.broadcasted_iota(jnp.int32, (tm, 2 * tm), 1).astype(F32)
    sel = jnp.where((scol == q1) | (scol == q2), 1.0, 0.0).astype(BF16)
    pltpu.make_async_copy(ye_ref.at[pl.ds(0, 2 * tm)], ybuf.at[slot], sems.at[slot]).wait()
    y = _tiles_to_rows(ybuf[slot]) * gs_ref[:, 0:1]
    hi = y.astype(BF16)
    lo = (y - hi.astype(F32)).astype(BF16)
    h2 = h1_ref[...] + _dot(sel, hi) + _dot(sel, lo)
    o_ref[...] = _ple_update(h2, p_ref[0, 0], pg_ref, pwg_ref, pwp_ref, fg_ref, final)


def _combine_ple(n_run, lo_run, g_run, h1, meta, gs, ltri, upper, p, pg, pwg, pwp, fg, ye, layer, seq, tm, final):
    t = h1.shape[0]
    full = lambda a: pl.BlockSpec(a.shape, lambda i, *_: (0,) * a.ndim)
    grid_spec = pltpu.PrefetchScalarGridSpec(
        num_scalar_prefetch=3,
        grid=(t // tm,),
        in_specs=[pl.BlockSpec((tm, D_MODEL), lambda i, *_: (i, 0)),
                  pl.BlockSpec((tm, LANES), lambda i, *_: (i, 0)),
                  pl.BlockSpec((2 * tm, LANES), lambda i, *_: (i, 0)),
                  full(ltri), full(upper),
                  _p_spec(tm, seq, layer), full(pg), full(pwg), full(pwp), full(fg),
                  pl.BlockSpec(memory_space=pl.ANY)],
        out_specs=pl.BlockSpec((tm, D_MODEL), lambda i, *_: (i, 0)),
        scratch_shapes=[pltpu.VMEM((2, 2 * tm, ROW_TILES, LANES), F32), pltpu.SemaphoreType.DMA((2,))],
    )
    return pl.pallas_call(
        functools.partial(_combine_kernel, tm=tm, final=final),
        grid_spec=grid_spec,
        out_shape=jax.ShapeDtypeStruct((t, D_MODEL), F32),
        compiler_params=_cparams(("arbitrary",)),
        name="moe_combine_ple",
    )(n_run, lo_run, g_run, h1, meta, gs, ltri, upper, p, pg, pwg, pwp, fg, ye)


def _swap_halves(w):
    half = w.shape[-1] // 2
    return jnp.concatenate([w[..., half:], w[..., :half]], axis=-1)


def _pad_cols(w, left, total):
    return jnp.pad(w, ((0, 0),) * (w.ndim - 1) + ((left, total - left - w.shape[-1]),))


def _arrange_w_in(w):
    sizes = (512, 512, SSD_XBC, SSD_HEADS, MLA_Q_RANK, MLA_KV_RANK, MLA_ROPE)
    pts = np.cumsum(sizes)[:-1].tolist()
    w = w.astype(BF16)
    w_conv, w_z, w_xbc, w_dt, w_cq, w_ckv, w_kr = jnp.split(w, pts, axis=-1)
    seg_kr = _pad_cols(w_kr, ROPE_LO, LANES)
    seg_dtr = _pad_cols(w_dt, 0, LANES) + _pad_cols(_swap_halves(w_kr), ROPE_LO, LANES)
    return jnp.concatenate([w_conv, w_z, w_xbc, w_cq, w_ckv, seg_kr, seg_dtr], axis=-1)


def _arrange_w_uq(w):
    main, rot = [], []
    for h in range(MLA_HEADS):
        wh = w[:, h * (MLA_NOPE + MLA_ROPE):(h + 1) * (MLA_NOPE + MLA_ROPE)]
        main.append(_pad_cols(wh, 0, HEAD_PAD))
        rot.append(_pad_cols(_swap_halves(wh[:, MLA_NOPE:]), ROPE_LO, HEAD_PAD))
    return jnp.concatenate(main + rot, axis=1).astype(BF16)


def _arrange_w_ukv(w):
    ks, vs = [], []
    for h in range(MLA_HEADS):
        wh = w[:, h * (MLA_NOPE + MLA_V):(h + 1) * (MLA_NOPE + MLA_V)]
        ks.append(_pad_cols(wh[:, :MLA_NOPE], 0, HEAD_PAD))
        vs.append(wh[:, MLA_NOPE:])
    return jnp.concatenate(ks, axis=1).astype(BF16), jnp.concatenate(vs, axis=1).T.astype(BF16)


def _row(v, width=None):
    v = v.reshape(1, -1).astype(F32)
    if width is not None:
        v = jnp.pad(v, ((0, 0), (0, width - v.shape[1])))
    return v


def _pick(n, prefs):
    for c in prefs:
        if n % c == 0:
            return c
    return n


def kernel(x, p, positions, attn_norm_g, w_in, conv_dw_w, conv_dw_b, conv_ln_g, conv_ln_b, ssd_conv_w, ssd_conv_b, ssd_dt_bias, ssd_a_log, ssd_d, ssd_norm_g, mla_q_norm_g, mla_w_uq, mla_kv_norm_g, mla_w_ukv, w_out, ffn_norm_g, dense_w_gate, dense_w_up, dense_w_down, moe_router, moe_w_gate, moe_w_up, moe_w_down, ple_norm_g, ple_w_gate, ple_w_proj, final_norm_g):
    bsz, s, _ = x.shape
    t = bsz * s
    tm_row = _pick(t, (512, 256, 128))
    tm_ffn = _pick(s, (1024, 512, 256, 128))
    fc = 512
    tc = _pick(s, (256, 128))
    ts = _pick(s, (512, 256, 128))
    tm_moe = _pick(t, (1024, 512, 256, 128))
    tm_tok = _pick(s, (256, 128))
    n_slots = 2 * t + N_EXPERTS * tm_moe

    inv = ROPE_BASE ** (-jnp.arange(0, MLA_ROPE, 2, dtype=F32) / MLA_ROPE)
    inv128 = _pad_cols(jnp.concatenate([inv, inv])[None, :], ROPE_LO, LANES)
    pos128 = jnp.broadcast_to(positions.astype(F32).reshape(t, 1), (t, LANES))
    ctab, stab = _rope_tables(pos128, inv128, tm_row)
    grp = np.arange(CONV_WIDTH) // (CONV_WIDTH // CONV_GROUPS)
    gmean = jnp.asarray((grp[:, None] == grp[None, :]) / (CONV_WIDTH // CONV_GROUPS), BF16)
    tril = jnp.asarray(np.tril(np.ones((SSD_CHUNK, SSD_CHUNK))), BF16)
    hd = np.arange(SSD_WIDTH) // SSD_HEADDIM
    emat = jnp.asarray(np.arange(LANES)[:, None] == hd[None, :], BF16)
    ltri = jnp.asarray(np.tril(np.ones((tm_tok, tm_tok)), -1), BF16)
    upper = jnp.asarray(np.triu(np.ones((LANES, LANES)), 1), BF16)

    w_in_all = _arrange_w_in(w_in)
    dense_wg, dense_wu, dense_wd = dense_w_gate, dense_w_up, dense_w_down
    moe_wg, moe_wu, moe_wd = moe_w_gate, moe_w_up, moe_w_down

    h = x.reshape(t, D_MODEL)
    for i in range(DEPTH):
        u_conv, z, xbc, dtm, q, k, vt = _inproj(
            h, _row(attn_norm_g[i]), w_in_all, i, ctab, stab,
            _row(mla_q_norm_g[i]), _arrange_w_uq(mla_w_uq[i]),
            _row(mla_kv_norm_g[i]), *_arrange_w_ukv(mla_w_ukv[i]), tm_row)
        y_conv = _conformer_conv(
            u_conv.reshape(bsz, s, -1), jnp.pad(conv_dw_w[i], ((0, 1), (0, 0))), _row(conv_dw_b[i]),
            _row(conv_ln_g[i]), _row(conv_ln_b[i]), gmean, ts)
        y_ssd = _ssd(
            xbc.reshape(bsz, s, -1), z.reshape(bsz, s, -1), dtm.reshape(bsz, s, -1),
            jnp.pad(ssd_conv_w[i], ((0, 8 - SSD_CONV), (0, 0))), _row(ssd_conv_b[i]),
            _row(ssd_dt_bias[i], LANES), _row(ssd_a_log[i], LANES),
            _row(jnp.repeat(ssd_d[i], SSD_HEADDIM)), _row(ssd_norm_g[i]), tril, emat, ts)
        ym = _attention(q, k, vt, bsz, tc)
        yc, ys = y_conv.reshape(t, -1), y_ssd.reshape(t, -1)
        wo = w_out[i].astype(BF16)
        j = i // 2
        final = i == DEPTH - 1
        ple = (p, _row(ple_norm_g[i]), ple_w_gate[i].astype(BF16), ple_w_proj[i].astype(BF16), _row(final_norm_g))
        if i % 2 == 0:
            h = _outproj_ffn_ple_dense(yc, ys, ym, h, wo, _row(ffn_norm_g[i]), dense_wg, dense_wu, dense_wd, *ple,
                                       i, j, s, tm_ffn, fc, final)
        else:
            router = _pad_cols(moe_router[j], 0, LANES).astype(BF16)
            h1, hn, meta, tcnt, cnt = _outproj_moe(yc, ys, ym, h, wo, _row(ffn_norm_g[i]), tm_tok, router)
            counts = cnt[0, :N_EXPERTS].astype(jnp.int32)
            padded = ((counts + tm_moe - 1) // tm_moe) * tm_moe
            pends = jnp.cumsum(padded)
            pstarts = pends - padded
            before = tcnt[:, 0, :N_EXPERTS].astype(jnp.int32)
            n_run = jnp.concatenate([before[1:], counts[None, :]], axis=0) - before
            lo_run = jnp.cumsum(n_run, axis=1) - n_run
            g_run = pstarts[None, :] + before
            runs = (n_run.reshape(-1), lo_run.reshape(-1), g_run.reshape(-1))
            n_tiles = n_slots // tm_moe
            n_used = (pends[-1] // tm_moe).astype(jnp.int32)
            tile_start = jnp.arange(n_tiles, dtype=jnp.int32) * tm_moe
            tile_exp = jnp.minimum(jnp.sum(pends[None, :] <= tile_start[:, None], axis=1), N_EXPERTS - 1)
            tile_exp = tile_exp.astype(jnp.int32)
            n_valid = jnp.clip((pstarts + counts)[tile_exp] - tile_start, 0, tm_moe)
            tile_exp = jnp.where(jnp.arange(n_tiles) < n_used, tile_exp, tile_exp[jnp.maximum(n_used - 1, 0)])
            pad_start = jnp.concatenate([pstarts + counts, pends[-1:]])
            pad_len = jnp.concatenate([padded - counts, n_slots - pends[-1:]])
            xs, gs = _dispatch(*runs, pad_start, pad_len, hn, meta, ltri, upper, n_slots, tm_tok, tm_moe)
            ye = _moe_ffn(tile_exp, n_used.reshape(1), n_valid, xs, moe_wg, moe_wu, moe_wd, j, tm_moe, fc)
            h = _combine_ple(*runs, h1, meta, gs, ltri, upper, *ple, ye, i, s, tm_tok, final)
    return h.reshape(bsz, s, D_MODEL)
```

```python
import functools

import numpy as np
import jax
import jax.numpy as jnp
from jax import lax
from jax.experimental import pallas as pl
from jax.experimental.pallas import tpu as pltpu

F32 = jnp.float32
BF16 = jnp.bfloat16

D_MODEL = 1024
DEPTH = 4
PLE_DIM = 256
CONV_WIDTH = 256
CONV_GROUPS = 4
CONV_KERNEL = 31
SSD_WIDTH = 512
SSD_HEADDIM = 64
SSD_HEADS = 8
SSD_NGROUPS = 2
SSD_STATE = 128
SSD_CONV = 4
SSD_CHUNK = 128
MLA_HEADS = 4
MLA_NOPE = 64
MLA_ROPE = 32
MLA_V = 64
MLA_Q_RANK = 256
MLA_KV_RANK = 128
ROPE_BASE = 10000.0
D_FF = 3584
N_EXPERTS = 8
RMS_EPS = 1e-6
LN_EPS = 1e-5

LANES = 128
HEAD_PAD = 128
ROPE_LO = MLA_NOPE
VMEM_LIMIT = 48 * 1024 * 1024

C_CONV = 0
C_Z = 512
C_XBC = 1024
C_CQ = 2048
C_CKV = 2304
C_KR = 2432
C_DTR = 2560
IN_COLS_PAD = 2688


def _cparams(sem):
    return pltpu.CompilerParams(dimension_semantics=sem, vmem_limit_bytes=VMEM_LIMIT)


def _dot(a, b):
    return jnp.dot(a, b, preferred_element_type=F32)


def _dot_nt(a, b):
    return lax.dot_general(a, b, (((1,), (1,)), ((), ())), preferred_element_type=F32)


def _split3(a):
    a1 = a.astype(BF16)
    r1 = a - a1.astype(F32)
    a2 = r1.astype(BF16)
    a3 = (r1 - a2.astype(F32)).astype(BF16)
    return a1, a2, a3


def _dot_f32_lhs(a, m):
    a1, a2, a3 = _split3(a)
    return _dot(a1, m) + _dot(a2, m) + _dot(a3, m)


def _dot_f32_rhs(m, b):
    b1, b2, b3 = _split3(b)
    return _dot(m, b1) + _dot(m, b2) + _dot(m, b3)


def _rms(x, g, eps=RMS_EPS):
    return x * lax.rsqrt(jnp.mean(x * x, axis=-1, keepdims=True) + eps) * g


def _sigmoid(x):
    return 1.0 / (1.0 + jnp.exp(-x))


def _silu(x):
    return x * _sigmoid(x)


def _rope_kernel(pos_ref, inv_ref, c_ref, s_ref):
    ang = pos_ref[...] * inv_ref[...]
    lane = lax.broadcasted_iota(jnp.int32, ang.shape, 1)
    in_rope = (lane >= ROPE_LO) & (lane < ROPE_LO + MLA_ROPE)
    first_half = lane < ROPE_LO + MLA_ROPE // 2
    cos = jnp.cos(ang)
    sin = jnp.sin(ang)
    c_ref[...] = jnp.where(in_rope, cos, jnp.where(lane < ROPE_LO, 1.0, 0.0))
    s_ref[...] = jnp.where(in_rope, jnp.where(first_half, -sin, sin), 0.0)


def _rope_tables(pos128, inv128, tm):
    t = pos128.shape[0]
    return pl.pallas_call(
        _rope_kernel,
        grid=(t // tm,),
        in_specs=[pl.BlockSpec((tm, LANES), lambda i: (i, 0)),
                  pl.BlockSpec((1, LANES), lambda i: (0, 0))],
        out_specs=[pl.BlockSpec((tm, LANES), lambda i: (i, 0))] * 2,
        out_shape=[jax.ShapeDtypeStruct((t, LANES), F32)] * 2,
        compiler_params=_cparams(("parallel",)),
        name="rope_tables",
    )(pos128, inv128)


def _inproj_kernel(h_ref, g_ref, w_ref, c_ref, s_ref, gq_ref, wq_ref, gkv_ref, wk_ref, wvt_ref,
                   oconv_ref, oz_ref, oxbc_ref, odt_ref, oq_ref, ok_ref, ovt_ref):
    xn = _rms(h_ref[...], g_ref[...]).astype(BF16)
    oconv_ref[...] = _dot(xn, w_ref[0, :, C_CONV:C_Z])
    oz_ref[...] = _dot(xn, w_ref[0, :, C_Z:C_XBC])
    oxbc_ref[...] = _dot(xn, w_ref[0, :, C_XBC:C_CQ])
    cq = _dot(xn, w_ref[0, :, C_CQ:C_CKV])
    ckv = _dot(xn, w_ref[0, :, C_CKV:C_KR])
    kr = _dot(xn, w_ref[0, :, C_KR:C_DTR])
    dtr = _dot(xn, w_ref[0, :, C_DTR:IN_COLS_PAD])
    odt_ref[...] = dtr
    c = c_ref[...]
    s = s_ref[...]
    c4 = jnp.concatenate([c] * MLA_HEADS, axis=1)
    s4 = jnp.concatenate([s] * MLA_HEADS, axis=1)
    qq = _dot(_rms(cq, gq_ref[...]).astype(BF16), wq_ref[...])
    nq = MLA_HEADS * HEAD_PAD
    scale = (MLA_NOPE + MLA_ROPE) ** -0.5 * np.log2(np.e)
    oq_ref[...] = ((qq[:, :nq] * c4 + qq[:, nq:] * s4) * scale).astype(BF16)
    ckvn = _rms(ckv, gkv_ref[...]).astype(BF16)
    kpe = kr * c + dtr * s
    ok_ref[...] = (_dot(ckvn, wk_ref[...]) + jnp.concatenate([kpe] * MLA_HEADS, axis=1)).astype(BF16)
    ovt_ref[...] = _dot_nt(wvt_ref[...], ckvn).astype(BF16)


def _inproj(h, g, w, layer, ctab, stab, gq, wq, gkv, wk, wvt, tm):
    t = h.shape[0]
    row = lambda n: pl.BlockSpec((tm, n), lambda i: (i, 0))
    full = lambda a: pl.BlockSpec(a.shape, lambda i: (0,) * a.ndim)
    nq = MLA_HEADS * HEAD_PAD
    nv = MLA_HEADS * MLA_V
    widths = (512, 512, 1024, LANES, nq, nq)
    dtypes = (F32, F32, F32, F32, BF16, BF16)
    return pl.pallas_call(
        _inproj_kernel,
        grid=(t // tm,),
        in_specs=[row(D_MODEL), full(g), pl.BlockSpec((1,) + w.shape[1:], lambda i: (layer, 0, 0)),
                  row(LANES), row(LANES), full(gq), full(wq), full(gkv), full(wk), full(wvt)],
        out_specs=[row(n) for n in widths] + [pl.BlockSpec((nv, tm), lambda i: (0, i))],
        out_shape=[jax.ShapeDtypeStruct((t, n), d) for n, d in zip(widths, dtypes)]
        + [jax.ShapeDtypeStruct((nv, t), BF16)],
        compiler_params=_cparams(("parallel",)),
        name="inproj",
    )(h, g, w, ctab, stab, gq, wq, gkv, wk, wvt)


CONV_HALO = 32
CONV_SUB = 64


def _conv_kernel(u_ref, w_ref, b_ref, lg_ref, lb_ref, gm_ref, o_ref, gbuf, shifted, *, tc):
    @pl.when(pl.program_id(1) == 0)
    def _():
        gbuf[0:CONV_HALO, :] = jnp.zeros((CONV_HALO, CONV_WIDTH), F32)

    u = u_ref[0]
    gbuf[CONV_HALO:CONV_HALO + tc, :] = u[:, :CONV_WIDTH] * _sigmoid(u[:, CONV_WIDTH:])
    gm = gm_ref[...]
    first = CONV_HALO - (CONV_KERNEL - 1)
    span = CONV_HALO + tc - 8
    for s in range(1, 8):
        shifted[s - 1, 0:span, :] = gbuf[s:s + span, :]
    for r0 in range(0, tc, CONV_SUB):
        acc = jnp.broadcast_to(b_ref[...], (CONV_SUB, CONV_WIDTH))
        for j in range(CONV_KERNEL):
            start = first + j + r0
            s, a = start % 8, start - start % 8
            assert a + CONV_SUB <= span or s == 0
            win = gbuf[a:a + CONV_SUB, :] if s == 0 else shifted[s - 1, a:a + CONV_SUB, :]
            acc = acc + w_ref[j:j + 1, :] * win
        mu = _dot_f32_lhs(acc, gm)
        d = acc - mu
        var = _dot_f32_lhs(d * d, gm)
        hn = d * lax.rsqrt(var + LN_EPS) * lg_ref[...] + lb_ref[...]
        o_ref[0, r0:r0 + CONV_SUB, :] = _silu(hn).astype(BF16)
    gbuf[0:CONV_HALO, :] = gbuf[tc:tc + CONV_HALO, :]


def _conformer_conv(u, w, b, lg, lb, gm, tc):
    bsz, s, _ = u.shape
    full = lambda a: pl.BlockSpec(a.shape, lambda i, j: (0,) * a.ndim)
    return pl.pallas_call(
        functools.partial(_conv_kernel, tc=tc),
        grid=(bsz, s // tc),
        in_specs=[pl.BlockSpec((1, tc, 2 * CONV_WIDTH), lambda i, j: (i, j, 0)),
                  full(w), full(b), full(lg), full(lb), full(gm)],
        out_specs=pl.BlockSpec((1, tc, CONV_WIDTH), lambda i, j: (i, j, 0)),
        out_shape=jax.ShapeDtypeStruct((bsz, s, CONV_WIDTH), BF16),
        scratch_shapes=[pltpu.VMEM((CONV_HALO + tc, CONV_WIDTH), F32),
                        pltpu.VMEM((7, CONV_HALO + tc, CONV_WIDTH), F32)],
        compiler_params=_cparams(("parallel", "arbitrary")),
        name="conformer_conv",
    )(u, w, b, lg, lb, gm)


SSD_HALO = 8
SSD_XBC = SSD_WIDTH + 2 * SSD_NGROUPS * SSD_STATE
GROUP_W = SSD_WIDTH // SSD_NGROUPS
HEADS_PER_GROUP = SSD_HEADS // SSD_NGROUPS


def _ssd_chunk(r0, cbuf, z_ref, dt_ref, cw_ref, cb_ref, dtb_ref, alog_ref, dsk_ref, ng_ref, tril_ref, exp_ref,
               o_ref, state):
    L = SSD_CHUNK
    first = SSD_HALO - (SSD_CONV - 1) + r0
    acc = jnp.broadcast_to(cb_ref[...], (L, SSD_XBC))
    for j in range(SSD_CONV):
        acc = acc + cw_ref[j:j + 1, :] * cbuf[first + j:first + j + L, :]
    xc = _silu(acc)
    xs = xc[:, :SSD_WIDTH]
    bm = xc[:, SSD_WIDTH:SSD_WIDTH + SSD_NGROUPS * SSD_STATE]
    cm = xc[:, SSD_WIDTH + SSD_NGROUPS * SSD_STATE:]

    lane = lax.broadcasted_iota(jnp.int32, (1, LANES), 1)
    v = dt_ref[0, r0:r0 + L, :] + dtb_ref[...]
    dt = jnp.maximum(v, 0.0) + jnp.log1p(jnp.exp(-jnp.abs(v)))
    a = jnp.where(lane < SSD_HEADS, -jnp.exp(alog_ref[...]), 0.0)
    cs = _dot_f32_rhs(tril_ref[...], dt * a)
    cs_t = cs.T
    cs_last = cs[L - 1:L, :]
    emat = exp_ref[...]
    dt_e = _dot_f32_lhs(dt, emat)
    ecs_e = _dot_f32_lhs(jnp.exp(cs), emat)
    ds_e = _dot_f32_lhs(jnp.exp(cs_last - cs), emat)
    cd_e = _dot_f32_lhs(jnp.broadcast_to(jnp.exp(cs_last), (8, LANES)), emat)[0:1, :]

    xd = xs * dt_e
    xdb = xd.astype(BF16)
    xds = (xd * ds_e).astype(BF16)
    rows = lax.broadcasted_iota(jnp.int32, (L, L), 0)
    cols = lax.broadcasted_iota(jnp.int32, (L, L), 1)
    causal = rows >= cols
    ys = []
    for g in range(SSD_NGROUPS):
        cmg = cm[:, g * SSD_STATE:(g + 1) * SSD_STATE].astype(BF16)
        bmg = bm[:, g * SSD_STATE:(g + 1) * SSD_STATE]
        cbm = _dot_nt(cmg, bmg.astype(BF16))
        yd = []
        for r in range(HEADS_PER_GROUP):
            h = g * HEADS_PER_GROUP + r
            seg = cs[:, h:h + 1] - cs_t[h:h + 1, :]
            dec = jnp.exp(jnp.where(causal, seg, -jnp.inf))
            mix = (cbm * dec).astype(BF16)
            yd.append(_dot(mix, xdb[:, h * SSD_HEADDIM:(h + 1) * SSD_HEADDIM]))
        gs = slice(g * GROUP_W, (g + 1) * GROUP_W)
        prev = state[g]
        y_off = _dot(cmg, prev.astype(BF16)) * ecs_e[:, gs]
        st_new = _dot(bmg.T.astype(BF16), xds[:, gs])
        state[g] = prev * cd_e[:, gs] + st_new
        ys.append(jnp.concatenate(yd, axis=1) + y_off)
    y = jnp.concatenate(ys, axis=1) + dsk_ref[...] * xs
    yg = y * _silu(z_ref[0, r0:r0 + L, :])
    outs = []
    for g in range(SSD_NGROUPS):
        ygg = yg[:, g * GROUP_W:(g + 1) * GROUP_W]
        outs.append(ygg * lax.rsqrt(jnp.mean(ygg * ygg, axis=-1, keepdims=True) + RMS_EPS))
    o_ref[0, r0:r0 + L, :] = (jnp.concatenate(outs, axis=1) * ng_ref[...]).astype(BF16)


def _ssd_kernel(xbc_ref, z_ref, dt_ref, cw_ref, cb_ref, dtb_ref, alog_ref, dsk_ref, ng_ref, tril_ref, exp_ref,
                o_ref, cbuf, state, *, rows):
    @pl.when(pl.program_id(1) == 0)
    def _():
        cbuf[0:SSD_HALO, :] = jnp.zeros((SSD_HALO, SSD_XBC), F32)
        state[...] = jnp.zeros(state.shape, F32)

    cbuf[SSD_HALO:SSD_HALO + rows, :] = xbc_ref[0]
    for r0 in range(0, rows, SSD_CHUNK):
        _ssd_chunk(r0, cbuf, z_ref, dt_ref, cw_ref, cb_ref, dtb_ref, alog_ref, dsk_ref, ng_ref, tril_ref, exp_ref,
                   o_ref, state)
    cbuf[0:SSD_HALO, :] = cbuf[rows:rows + SSD_HALO, :]


def _ssd(xbc, z, dtm, cw, cb, dtb, alog, dsk, ng, tril, emat, rows):
    bsz, s, _ = xbc.shape
    full = lambda a: pl.BlockSpec(a.shape, lambda i, j: (0,) * a.ndim)
    blk = lambda n: pl.BlockSpec((1, rows, n), lambda i, j: (i, j, 0))
    return pl.pallas_call(
        functools.partial(_ssd_kernel, rows=rows),
        grid=(bsz, s // rows),
        in_specs=[blk(SSD_XBC), blk(SSD_WIDTH), blk(LANES),
                  full(cw), full(cb), full(dtb), full(alog), full(dsk), full(ng), full(tril), full(emat)],
        out_specs=blk(SSD_WIDTH),
        out_shape=jax.ShapeDtypeStruct((bsz, s, SSD_WIDTH), BF16),
        scratch_shapes=[pltpu.VMEM((SSD_HALO + rows, SSD_XBC), F32),
                        pltpu.VMEM((SSD_NGROUPS, SSD_STATE, GROUP_W), F32)],
        compiler_params=_cparams(("parallel", "arbitrary")),
        name="ssd",
    )(xbc, z, dtm, cw, cb, dtb, alog, dsk, ng, tril, emat)


def _attn_kernel(q_ref, k_ref, vt_ref, o_ref, m_sc, l_sc, acc_sc, st0, st1, *, tq, tk):
    i = pl.program_id(1)
    krow = lax.broadcasted_iota(jnp.int32, (tk, tq), 0)
    qcol = lax.broadcasted_iota(jnp.int32, (tk, tq), 1)
    ones = jnp.ones((16, tk), BF16)
    m_sc[...] = jnp.full(m_sc.shape, -jnp.inf, F32)
    l_sc[...] = jnp.zeros(l_sc.shape, F32)
    acc_sc[...] = jnp.zeros(acc_sc.shape, F32)

    def scores(j, st_ref):
        start = pl.multiple_of(j * tk, tk)
        for h in range(MLA_HEADS):
            hs = slice(h * HEAD_PAD, (h + 1) * HEAD_PAD)
            st_ref[h] = _dot_nt(k_ref[pl.ds(start, tk), hs], q_ref[:, hs])

    def update(j, st_ref, key0):
        start = pl.multiple_of(j * tk, tk)
        for h in range(MLA_HEADS):
            vs = slice(h * MLA_V, (h + 1) * MLA_V)
            st = st_ref[h]
            if key0 is not None:
                st = jnp.where(krow + key0 <= qcol, st, -jnp.inf)
            m = m_sc[h:h + 1, :]
            m_new = jnp.maximum(m, jnp.max(st, axis=0, keepdims=True))
            p = jnp.exp2(st - m_new).astype(BF16)
            alpha = jnp.exp2(m - m_new)
            m_sc[h:h + 1, :] = m_new
            lhs = jnp.concatenate([vt_ref[vs, pl.ds(start, tk)], ones], axis=0)
            pv = _dot(lhs, p)
            l_sc[h:h + 1, :] = alpha * l_sc[h:h + 1, :] + pv[MLA_V:MLA_V + 1, :]
            acc_sc[vs, :] = alpha * acc_sc[vs, :] + pv[:MLA_V, :]

    def pair(jp, c):
        j0 = 2 * jp
        scores(j0 + 1, st1)
        update(j0, st0, None)
        scores(j0 + 2, st0)
        update(j0 + 1, st1, None)
        return c

    scores(0, st0)
    lax.fori_loop(0, i, pair, 0)
    scores(2 * i + 1, st1)
    update(2 * i, st0, 0)
    update(2 * i + 1, st1, tk)
    outs = [acc_sc[h * MLA_V:(h + 1) * MLA_V, :] / l_sc[h:h + 1, :] for h in range(MLA_HEADS)]
    o_ref[...] = jnp.concatenate(outs, axis=0).T.astype(BF16)


def _attention(q, k, vt, bsz, tk):
    t = q.shape[0]
    s = t // bsz
    tq = 2 * tk
    nblk = s // tq
    nq = MLA_HEADS * HEAD_PAD
    nv = MLA_HEADS * MLA_V
    return pl.pallas_call(
        functools.partial(_attn_kernel, tq=tq, tk=tk),
        grid=(bsz, nblk),
        in_specs=[pl.BlockSpec((tq, nq), lambda b, i: (b * nblk + i, 0)),
                  pl.BlockSpec((s, nq), lambda b, i: (b, 0)),
                  pl.BlockSpec((nv, s), lambda b, i: (0, b))],
        out_specs=pl.BlockSpec((tq, nv), lambda b, i: (b * nblk + i, 0)),
        out_shape=jax.ShapeDtypeStruct((t, nv), BF16),
        scratch_shapes=[pltpu.VMEM((8, tq), F32), pltpu.VMEM((8, tq), F32), pltpu.VMEM((nv, tq), F32),
                        pltpu.VMEM((MLA_HEADS, tk, tq), F32), pltpu.VMEM((MLA_HEADS, tk, tq), F32)],
        compiler_params=_cparams(("parallel", "parallel")),
        name="mla_attention",
    )(q, k, vt)


M_IDX0, M_IDX1, M_GATE0, M_GATE1 = range(4)


def _outproj_body(yc, ys, ym, h, w_ref, g_ref):
    acc = _dot(yc, w_ref[0:CONV_WIDTH, :])
    acc = acc + _dot(ys, w_ref[CONV_WIDTH:CONV_WIDTH + SSD_WIDTH, :])
    acc = acc + _dot(ym, w_ref[CONV_WIDTH + SSD_WIDTH:, :])
    h1 = h + acc
    return h1, _rms(h1, g_ref[...])


def _outproj_moe_kernel(yc_ref, ys_ref, ym_ref, h_ref, w_ref, g_ref, r_ref,
                        h1_ref, hn_ref, meta_ref, tcnt_ref, cnt_ref):
    @pl.when(pl.program_id(0) == 0)
    def _():
        cnt_ref[...] = jnp.zeros(cnt_ref.shape, F32)

    h1, hn = _outproj_body(yc_ref[...], ys_ref[...], ym_ref[...], h_ref[...], w_ref, g_ref)
    h1_ref[...] = h1
    hnb = hn.astype(BF16)
    hn_ref[...] = hnb
    logits = _dot(hnb, r_ref[...])
    lane = lax.broadcasted_iota(jnp.int32, logits.shape, 1)
    lm = jnp.where(lane < N_EXPERTS, logits, -jnp.inf)
    m1 = jnp.max(lm, axis=-1, keepdims=True)
    i1 = jnp.min(jnp.where(lm == m1, lane, LANES), axis=-1, keepdims=True)
    lm2 = jnp.where(lane == i1, -jnp.inf, lm)
    m2 = jnp.max(lm2, axis=-1, keepdims=True)
    i2 = jnp.min(jnp.where(lm2 == m2, lane, LANES), axis=-1, keepdims=True)
    e = jnp.exp(m2 - m1)
    g1 = 1.0 / (1.0 + e)
    g2 = e / (1.0 + e)
    onehot = jnp.where((lane == i1) | (lane == i2), 1.0, 0.0)
    tcnt_ref[0] = cnt_ref[...]
    cnt_ref[...] = cnt_ref[...] + jnp.sum(onehot, axis=0, keepdims=True)
    meta = jnp.where(lane == M_IDX0, i1.astype(F32), 0.0)
    meta = jnp.where(lane == M_IDX1, i2.astype(F32), meta)
    meta = jnp.where(lane == M_GATE0, g1, meta)
    meta = jnp.where(lane == M_GATE1, g2, meta)
    meta_ref[...] = meta


def _outproj_moe(yc, ys, ym, h, w, g, tm, router):
    t = h.shape[0]
    row = lambda n: pl.BlockSpec((tm, n), lambda i: (i, 0))
    full = lambda a: pl.BlockSpec(a.shape, lambda i: (0,) * a.ndim)
    return pl.pallas_call(
        _outproj_moe_kernel,
        grid=(t // tm,),
        in_specs=[row(CONV_WIDTH), row(SSD_WIDTH), row(MLA_HEADS * MLA_V), row(D_MODEL), full(w), full(g),
                  full(router)],
        out_specs=[row(D_MODEL), row(D_MODEL), row(LANES),
                   pl.BlockSpec((1, 8, LANES), lambda i: (i, 0, 0)), pl.BlockSpec((8, LANES), lambda i: (0, 0))],
        out_shape=[jax.ShapeDtypeStruct((t, D_MODEL), F32), jax.ShapeDtypeStruct((t, D_MODEL), BF16),
                   jax.ShapeDtypeStruct((t, LANES), F32), jax.ShapeDtypeStruct((t // tm, 8, LANES), F32),
                   jax.ShapeDtypeStruct((8, LANES), F32)],
        compiler_params=_cparams(("arbitrary",)),
        name="outproj_moe",
    )(yc, ys, ym, h, w, g, router)


def _ple_update(h, p, pg_ref, pwg_ref, pwp_ref, fg_ref, final):
    gate = _sigmoid(_dot(_rms(h, pg_ref[...]).astype(BF16), pwg_ref[...]))
    out = h + _dot(p.astype(BF16), pwp_ref[...]) * gate
    if final:
        out = _rms(out, fg_ref[...])
    return out


def _p_spec(tm, seq, layer):
    tps = seq // tm
    return pl.BlockSpec((1, 1, tm, PLE_DIM), lambda i, *_: (layer, i // tps, i % tps, 0))


def _ffn_kernel(yc_ref, ys_ref, ym_ref, h_ref, wo_ref, g_ref, wg_ref, wu_ref, wd_ref,
                p_ref, pg_ref, pwg_ref, pwp_ref, fg_ref, o_ref, hn_sc, *, final):
    f = pl.program_id(1)
    half = hn_sc.shape[0] // 2
    halves = [slice(r0, r0 + half) for r0 in (0, half)]

    @pl.when(f == 0)
    def _():
        for rs in halves:
            h1, hn = _outproj_body(yc_ref[rs, :], ys_ref[rs, :], ym_ref[rs, :], h_ref[rs, :], wo_ref, g_ref)
            o_ref[rs, :] = h1
            hn_sc[rs, :] = hn.astype(BF16)

    wg, wu, wd = wg_ref[0].astype(BF16), wu_ref[0].astype(BF16), wd_ref[0].astype(BF16)
    for rs in halves:
        x = hn_sc[rs, :]
        mid = (_silu(_dot(x, wg)) * _dot(x, wu)).astype(BF16)
        o_ref[rs, :] += _dot(mid, wd)

    @pl.when(f == pl.num_programs(1) - 1)
    def _():
        for rs in halves:
            o_ref[rs, :] = _ple_update(o_ref[rs, :], p_ref[0, 0, rs, :], pg_ref, pwg_ref, pwp_ref, fg_ref, final)


def _outproj_ffn_ple_dense(yc, ys, ym, h, wo, g, wg, wu, wd, p, pg, pwg, pwp, fg, layer, ffn_layer, seq, tm, fc, final):
    t = h.shape[0]
    row = lambda n: pl.BlockSpec((tm, n), lambda i, f: (i, 0))
    full = lambda a: pl.BlockSpec(a.shape, lambda i, f: (0,) * a.ndim)
    return pl.pallas_call(
        functools.partial(_ffn_kernel, final=final),
        grid=(t // tm, D_FF // fc),
        in_specs=[row(CONV_WIDTH), row(SSD_WIDTH), row(MLA_HEADS * MLA_V), row(D_MODEL), full(wo), full(g),
                  pl.BlockSpec((1, D_MODEL, fc), lambda i, f: (ffn_layer, 0, f)),
                  pl.BlockSpec((1, D_MODEL, fc), lambda i, f: (ffn_layer, 0, f)),
                  pl.BlockSpec((1, fc, D_MODEL), lambda i, f: (ffn_layer, f, 0)),
                  _p_spec(tm, seq, layer), full(pg), full(pwg), full(pwp), full(fg)],
        out_specs=row(D_MODEL),
        out_shape=jax.ShapeDtypeStruct((t, D_MODEL), F32),
        scratch_shapes=[pltpu.VMEM((tm, D_MODEL), BF16)],
        compiler_params=_cparams(("parallel", "arbitrary")),
        name="outproj_ffn_ple_dense",
    )(yc, ys, ym, h, wo, g, wg, wu, wd, p, pg, pwg, pwp, fg)


ROW_TILES = D_MODEL // LANES


def _rows_to_tiles(val):
    blocks = jnp.stack([val[:, c * LANES:(c + 1) * LANES] for c in range(ROW_TILES)], axis=0)
    return jnp.transpose(blocks, (1, 0, 2))


def _tiles_to_rows(val):
    blocks = jnp.transpose(val, (1, 0, 2))
    return jnp.concatenate([blocks[c] for c in range(ROW_TILES)], axis=1)


def _run_bits(tm):
    return [1 << b for b in range(tm.bit_length() - 1, -1, -1)]


def _run_copies(n, src_ref, src0, dst_ref, dst0, sem, tm, wait, src_step=1):
    off = 0
    for b in _run_bits(tm):
        part = n & b

        @pl.when(part != 0)
        def _(off=off, b=b):
            cp = pltpu.make_async_copy(src_ref.at[pl.ds(src0 + off * src_step, b)],
                                       dst_ref.at[pl.ds(dst0 + off, b)], sem)
            if wait:
                cp.wait()
            else:
                cp.start()

        off = off + part


def _sorted_positions(meta, ltri, upper):
    lane = lax.broadcasted_iota(jnp.int32, meta.shape, 1)
    i1 = meta[:, M_IDX0:M_IDX0 + 1].astype(jnp.int32)
    i2 = meta[:, M_IDX1:M_IDX1 + 1].astype(jnp.int32)
    onehot = jnp.where((lane == i1) | (lane == i2), 1.0, 0.0)
    before = _dot(ltri, onehot.astype(BF16))
    n = jnp.broadcast_to(jnp.sum(onehot, axis=0, keepdims=True), (8, LANES))
    loff = _dot_f32_lhs(n, upper)[0:1, :]
    pos = before + loff
    q1 = jnp.sum(jnp.where(lane == i1, pos, 0.0), axis=-1, keepdims=True)
    q2 = jnp.sum(jnp.where(lane == i2, pos, 0.0), axis=-1, keepdims=True)
    return q1, q2


def _dispatch_kernel(n_ref, lo_ref, g_ref, ps_ref, pn_ref, x_ref, meta_ref, ltri_ref, up_ref, xs_ref, gs_ref,
                     xbuf, zbuf, sems, zsem, *, tm):
    i = pl.program_id(0)
    nt = pl.num_programs(0)
    slot = i % 2
    zrows = zbuf.shape[0]

    def pads(wait):
        for e in range(N_EXPERTS):
            _run_copies(pn_ref[e], zbuf, 0, xs_ref, ps_ref[e], zsem, zrows, wait, src_step=0)
        tail0, tail_len = ps_ref[N_EXPERTS], pn_ref[N_EXPERTS]
        for c in range(2 * N_EXPERTS):

            @pl.when(c * zrows < tail_len)
            def _(c=c):
                cp = pltpu.make_async_copy(zbuf, xs_ref.at[pl.ds(tail0 + c * zrows, zrows)], zsem)
                if wait:
                    cp.wait()
                else:
                    cp.start()

    @pl.when(i == 0)
    def _():
        zbuf[...] = jnp.zeros(zbuf.shape, F32)
        pads(False)

    meta = meta_ref[...]
    q1, q2 = _sorted_positions(meta, ltri_ref[...], up_ref[...])
    lane = lax.broadcasted_iota(jnp.int32, meta.shape, 1)
    qmat = jnp.where(lane == 0, q1, jnp.where(lane == 1, q2, 0.0))
    qt = qmat.T
    srow = lax.broadcasted_iota(jnp.int32, (2 * tm, tm), 0).astype(F32)
    p1 = jnp.where(srow == qt[0:1, :], 1.0, 0.0).astype(BF16)
    p2 = jnp.where(srow == qt[1:2, :], 1.0, 0.0).astype(BF16)
    g1 = jnp.broadcast_to(meta[:, M_GATE0:M_GATE0 + 1], (tm, LANES))
    g2 = jnp.broadcast_to(meta[:, M_GATE1:M_GATE1 + 1], (tm, LANES))
    gs_ref[...] = _dot_f32_rhs(p1, g1) + _dot_f32_rhs(p2, g2)
    xbuf[slot] = _rows_to_tiles(_dot(p1 + p2, x_ref[...]))

    for e in range(N_EXPERTS):
        k = i * N_EXPERTS + e
        _run_copies(n_ref[k], xbuf.at[slot], lo_ref[k], xs_ref, g_ref[k], sems.at[slot], tm, False)

    def wait_tile(sl):
        pltpu.make_async_copy(xbuf.at[sl], xs_ref.at[pl.ds(0, 2 * tm)], sems.at[sl]).wait()

    @pl.when(i > 0)
    def _():
        wait_tile(1 - slot)

    @pl.when(i == nt - 1)
    def _():
        wait_tile(slot)
        pads(True)


def _dispatch(n_run, lo_run, g_run, pad_start, pad_len, hn, meta, ltri, upper, n_slots, tm, tm_moe):
    t = hn.shape[0]
    full = lambda a: pl.BlockSpec(a.shape, lambda i, *_: (0,) * a.ndim)
    grid_spec = pltpu.PrefetchScalarGridSpec(
        num_scalar_prefetch=5,
        grid=(t // tm,),
        in_specs=[pl.BlockSpec((tm, D_MODEL), lambda i, *_: (i, 0)),
                  pl.BlockSpec((tm, LANES), lambda i, *_: (i, 0)),
                  full(ltri), full(upper)],
        out_specs=[pl.BlockSpec(memory_space=pl.ANY), pl.BlockSpec((2 * tm, LANES), lambda i, *_: (i, 0))],
        scratch_shapes=[pltpu.VMEM((2, 2 * tm, ROW_TILES, LANES), F32),
                        pltpu.VMEM((tm_moe // 2, ROW_TILES, LANES), F32),
                        pltpu.SemaphoreType.DMA((2,)), pltpu.SemaphoreType.DMA],
    )
    return pl.pallas_call(
        functools.partial(_dispatch_kernel, tm=tm),
        grid_spec=grid_spec,
        out_shape=[jax.ShapeDtypeStruct((n_slots, ROW_TILES, LANES), F32),
                   jax.ShapeDtypeStruct((2 * t, LANES), F32)],
        compiler_params=_cparams(("arbitrary",)),
        name="moe_dispatch",
    )(n_run, lo_run, g_run, pad_start, pad_len, hn, meta, ltri, upper)


def _moe_ffn_kernel(texp_ref, nused_ref, nvalid_ref, x_ref, wg_ref, wu_ref, wd_ref, o_ref, xb, acc, *, tm):
    del texp_ref, nused_ref
    i = pl.program_id(0)
    f = pl.program_id(1)
    half = tm // 2

    @pl.when(f == 0)
    def _():
        xb[...] = _tiles_to_rows(x_ref[...]).astype(BF16)
        acc[...] = jnp.zeros(acc.shape, F32)

    def rows(n):
        wg, wu, wd = wg_ref[0, 0].astype(BF16), wu_ref[0, 0].astype(BF16), wd_ref[0, 0].astype(BF16)
        step = min(n, half)
        for r0 in range(0, n, step):
            x = xb[r0:r0 + step, :]
            mid = (_silu(_dot(x, wg)) * _dot(x, wu)).astype(BF16)
            acc[r0:r0 + step, :] += _dot(mid, wd)

    nvalid = nvalid_ref[i]
    quarter = half // 2

    @pl.when(nvalid > half)
    def _():
        rows(tm)

    @pl.when((nvalid > quarter) & (nvalid <= half))
    def _():
        rows(half)

    @pl.when((nvalid > 0) & (nvalid <= quarter))
    def _():
        rows(quarter)

    @pl.when(f == pl.num_programs(1) - 1)
    def _():
        o_ref[...] = _rows_to_tiles(acc[...])


def _moe_ffn(tile_exp, n_used, n_valid, xs, wg, wu, wd, layer, tm, fc):
    n_slots = xs.shape[0]
    nf = D_FF // fc

    def fsel(i, f, nu):
        return jnp.where(i < nu[0], f, nf - 1)

    grid_spec = pltpu.PrefetchScalarGridSpec(
        num_scalar_prefetch=3,
        grid=(n_slots // tm, nf),
        in_specs=[pl.BlockSpec((tm, ROW_TILES, LANES), lambda i, f, te, nu, nv: (jnp.minimum(i, nu[0] - 1), 0, 0)),
                  pl.BlockSpec((1, 1, D_MODEL, fc), lambda i, f, te, nu, nv: (layer, te[i], 0, fsel(i, f, nu))),
                  pl.BlockSpec((1, 1, D_MODEL, fc), lambda i, f, te, nu, nv: (layer, te[i], 0, fsel(i, f, nu))),
                  pl.BlockSpec((1, 1, fc, D_MODEL), lambda i, f, te, nu, nv: (layer, te[i], fsel(i, f, nu), 0))],
        out_specs=pl.BlockSpec((tm, ROW_TILES, LANES), lambda i, f, te, nu, nv: (i, 0, 0)),
        scratch_shapes=[pltpu.VMEM((tm, D_MODEL), BF16), pltpu.VMEM((tm, D_MODEL), F32)],
    )
    return pl.pallas_call(
        functools.partial(_moe_ffn_kernel, tm=tm),
        grid_spec=grid_spec,
        out_shape=jax.ShapeDtypeStruct((n_slots, ROW_TILES, LANES), F32),
        compiler_params=_cparams(("parallel", "arbitrary")),
        name="moe_ffn",
    )(tile_exp, n_used, n_valid, xs, wg, wu, wd)


def _combine_kernel(n_ref, lo_ref, g_ref, h1_ref, meta_ref, gs_ref, ltri_ref, up_ref,
                    p_ref, pg_ref, pwg_ref, pwp_ref, fg_ref, ye_ref, o_ref, ybuf, sems, *, tm, final):
    i = pl.program_id(0)
    nt = pl.num_programs(0)
    slot = i % 2

    def fetch(tile, sl):
        for e in range(N_EXPERTS):
            k = tile * N_EXPERTS + e
            _run_copies(n_ref[k], ye_ref, g_ref[k], ybuf.at[sl], lo_ref[k], sems.at[sl], tm, False)

    @pl.when(i == 0)
    def _():
        fetch(i, slot)

    @pl.when(i + 1 < nt)
    def _():
        fetch(i + 1, 1 - slot)

    q1, q2 = _sorted_positions(meta_ref[...], ltri_ref[...], up_ref[...])
    scol = lax.broadcasted_iota(jnp.int32, (tm, 2 * tm), 1).astype(F32)
    sel = jnp.where((scol == q1) | (scol == q2), 1.0, 0.0).astype(BF16)
    pltpu.make_async_copy(ye_ref.at[pl.ds(0, 2 * tm)], ybuf.at[slot], sems.at[slot]).wait()
    y = _tiles_to_rows(ybuf[slot]) * gs_ref[:, 0:1]
    hi = y.astype(BF16)
    lo = (y - hi.astype(F32)).astype(BF16)
    h2 = h1_ref[...] + _dot(sel, hi) + _dot(sel, lo)
    o_ref[...] = _ple_update(h2, p_ref[0, 0], pg_ref, pwg_ref, pwp_ref, fg_ref, final)


def _combine_ple(n_run, lo_run, g_run, h1, meta, gs, ltri, upper, p, pg, pwg, pwp, fg, ye, layer, seq, tm, final):
    t = h1.shape[0]
    full = lambda a: pl.BlockSpec(a.shape, lambda i, *_: (0,) * a.ndim)
    grid_spec = pltpu.PrefetchScalarGridSpec(
        num_scalar_prefetch=3,
        grid=(t // tm,),
        in_specs=[pl.BlockSpec((tm, D_MODEL), lambda i, *_: (i, 0)),
                  pl.BlockSpec((tm, LANES), lambda i, *_: (i, 0)),
                  pl.BlockSpec((2 * tm, LANES), lambda i, *_: (i, 0)),
                  full(ltri), full(upper),
                  _p_spec(tm, seq, layer), full(pg), full(pwg), full(pwp), full(fg),
                  pl.BlockSpec(memory_space=pl.ANY)],
        out_specs=pl.BlockSpec((tm, D_MODEL), lambda i, *_: (i, 0)),
        scratch_shapes=[pltpu.VMEM((2, 2 * tm, ROW_TILES, LANES), F32), pltpu.SemaphoreType.DMA((2,))],
    )
    return pl.pallas_call(
        functools.partial(_combine_kernel, tm=tm, final=final),
        grid_spec=grid_spec,
        out_shape=jax.ShapeDtypeStruct((t, D_MODEL), F32),
        compiler_params=_cparams(("arbitrary",)),
        name="moe_combine_ple",
    )(n_run, lo_run, g_run, h1, meta, gs, ltri, upper, p, pg, pwg, pwp, fg, ye)


def _swap_halves(w):
    half = w.shape[-1] // 2
    return jnp.concatenate([w[..., half:], w[..., :half]], axis=-1)


def _pad_cols(w, left, total):
    return jnp.pad(w, ((0, 0),) * (w.ndim - 1) + ((left, total - left - w.shape[-1]),))


def _arrange_w_in(w):
    sizes = (512, 512, SSD_XBC, SSD_HEADS, MLA_Q_RANK, MLA_KV_RANK, MLA_ROPE)
    pts = np.cumsum(sizes)[:-1].tolist()
    w = w.astype(BF16)
    w_conv, w_z, w_xbc, w_dt, w_cq, w_ckv, w_kr = jnp.split(w, pts, axis=-1)
    seg_kr = _pad_cols(w_kr, ROPE_LO, LANES)
    seg_dtr = _pad_cols(w_dt, 0, LANES) + _pad_cols(_swap_halves(w_kr), ROPE_LO, LANES)
    return jnp.concatenate([w_conv, w_z, w_xbc, w_cq, w_ckv, seg_kr, seg_dtr], axis=-1)


def _arrange_w_uq(w):
    main, rot = [], []
    for h in range(MLA_HEADS):
        wh = w[:, h * (MLA_NOPE + MLA_ROPE):(h + 1) * (MLA_NOPE + MLA_ROPE)]
        main.append(_pad_cols(wh, 0, HEAD_PAD))
        rot.append(_pad_cols(_swap_halves(wh[:, MLA_NOPE:]), ROPE_LO, HEAD_PAD))
    return jnp.concatenate(main + rot, axis=1).astype(BF16)


def _arrange_w_ukv(w):
    ks, vs = [], []
    for h in range(MLA_HEADS):
        wh = w[:, h * (MLA_NOPE + MLA_V):(h + 1) * (MLA_NOPE + MLA_V)]
        ks.append(_pad_cols(wh[:, :MLA_NOPE], 0, HEAD_PAD))
        vs.append(wh[:, MLA_NOPE:])
    return jnp.concatenate(ks, axis=1).astype(BF16), jnp.concatenate(vs, axis=1).T.astype(BF16)


def _row(v, width=None):
    v = v.reshape(1, -1).astype(F32)
    if width is not None:
        v = jnp.pad(v, ((0, 0), (0, width - v.shape[1])))
    return v


def _pick(n, prefs):
    for c in prefs:
        if n % c == 0:
            return c
    return n


def kernel(x, p, positions, attn_norm_g, w_in, conv_dw_w, conv_dw_b, conv_ln_g, conv_ln_b, ssd_conv_w, ssd_conv_b, ssd_dt_bias, ssd_a_log, ssd_d, ssd_norm_g, mla_q_norm_g, mla_w_uq, mla_kv_norm_g, mla_w_ukv, w_out, ffn_norm_g, dense_w_gate, dense_w_up, dense_w_down, moe_router, moe_w_gate, moe_w_up, moe_w_down, ple_norm_g, ple_w_gate, ple_w_proj, final_norm_g):
    bsz, s, _ = x.shape
    t = bsz * s
    tm_row = _pick(t, (512, 256, 128))
    tm_ffn = _pick(s, (1024, 512, 256, 128))
    fc = 512
    tc = _pick(s, (256, 128))
    ts = _pick(s, (512, 256, 128))
    tm_moe = _pick(t, (1024, 512, 256, 128))
    tm_tok = _pick(s, (256, 128))
    n_slots = 2 * t + N_EXPERTS * tm_moe

    inv = ROPE_BASE ** (-jnp.arange(0, MLA_ROPE, 2, dtype=F32) / MLA_ROPE)
    inv128 = _pad_cols(jnp.concatenate([inv, inv])[None, :], ROPE_LO, LANES)
    pos128 = jnp.broadcast_to(positions.astype(F32).reshape(t, 1), (t, LANES))
    ctab, stab = _rope_tables(pos128, inv128, tm_row)
    grp = np.arange(CONV_WIDTH) // (CONV_WIDTH // CONV_GROUPS)
    gmean = jnp.asarray((grp[:, None] == grp[None, :]) / (CONV_WIDTH // CONV_GROUPS), BF16)
    tril = jnp.asarray(np.tril(np.ones((SSD_CHUNK, SSD_CHUNK))), BF16)
    hd = np.arange(SSD_WIDTH) // SSD_HEADDIM
    emat = jnp.asarray(np.arange(LANES)[:, None] == hd[None, :], BF16)
    ltri = jnp.asarray(np.tril(np.ones((tm_tok, tm_tok)), -1), BF16)
    upper = jnp.asarray(np.triu(np.ones((LANES, LANES)), 1), BF16)

    w_in_all = _arrange_w_in(w_in)
    dense_wg, dense_wu, dense_wd = dense_w_gate, dense_w_up, dense_w_down
    moe_wg, moe_wu, moe_wd = moe_w_gate, moe_w_up, moe_w_down

    h = x.reshape(t, D_MODEL)
    for i in range(DEPTH):
        u_conv, z, xbc, dtm, q, k, vt = _inproj(
            h, _row(attn_norm_g[i]), w_in_all, i, ctab, stab,
            _row(mla_q_norm_g[i]), _arrange_w_uq(mla_w_uq[i]),
            _row(mla_kv_norm_g[i]), *_arrange_w_ukv(mla_w_ukv[i]), tm_row)
        y_conv = _conformer_conv(
            u_conv.reshape(bsz, s, -1), jnp.pad(conv_dw_w[i], ((0, 1), (0, 0))), _row(conv_dw_b[i]),
            _row(conv_ln_g[i]), _row(conv_ln_b[i]), gmean, ts)
        y_ssd = _ssd(
            xbc.reshape(bsz, s, -1), z.reshape(bsz, s, -1), dtm.reshape(bsz, s, -1),
            jnp.pad(ssd_conv_w[i], ((0, 8 - SSD_CONV), (0, 0))), _row(ssd_conv_b[i]),
            _row(ssd_dt_bias[i], LANES), _row(ssd_a_log[i], LANES),
            _row(jnp.repeat(ssd_d[i], SSD_HEADDIM)), _row(ssd_norm_g[i]), tril, emat, ts)
        ym = _attention(q, k, vt, bsz, tc)
        yc, ys = y_conv.reshape(t, -1), y_ssd.reshape(t, -1)
        wo = w_out[i].astype(BF16)
        j = i // 2
        final = i == DEPTH - 1
        ple = (p, _row(ple_norm_g[i]), ple_w_gate[i].astype(BF16), ple_w_proj[i].astype(BF16), _row(final_norm_g))
        if i % 2 == 0:
            h = _outproj_ffn_ple_dense(yc, ys, ym, h, wo, _row(ffn_norm_g[i]), dense_wg, dense_wu, dense_wd, *ple,
                                       i, j, s, tm_ffn, fc, final)
        else:
            router = _pad_cols(moe_router[j], 0, LANES).astype(BF16)
            h1, hn, meta, tcnt, cnt = _outproj_moe(yc, ys, ym, h, wo, _row(ffn_norm_g[i]), tm_tok, router)
            counts = cnt[0, :N_EXPERTS].astype(jnp.int32)
            padded = ((counts + tm_moe - 1) // tm_moe) * tm_moe
            pends = jnp.cumsum(padded)
            pstarts = pends - padded
            before = tcnt[:, 0, :N_EXPERTS].astype(jnp.int32)
            n_run = jnp.concatenate([before[1:], counts[None, :]], axis=0) - before
            lo_run = jnp.cumsum(n_run, axis=1) - n_run
            g_run = pstarts[None, :] + before
            runs = (n_run.reshape(-1), lo_run.reshape(-1), g_run.reshape(-1))
            n_tiles = n_slots // tm_moe
            n_used = (pends[-1] // tm_moe).astype(jnp.int32)
            tile_start = jnp.arange(n_tiles, dtype=jnp.int32) * tm_moe
            tile_exp = jnp.minimum(jnp.sum(pends[None, :] <= tile_start[:, None], axis=1), N_EXPERTS - 1)
            tile_exp = tile_exp.astype(jnp.int32)
            n_valid = jnp.clip((pstarts + counts)[tile_exp] - tile_start, 0, tm_moe)
            tile_exp = jnp.where(jnp.arange(n_tiles) < n_used, tile_exp, tile_exp[jnp.maximum(n_used - 1, 0)])
            pad_start = jnp.concatenate([pstarts + counts, pends[-1:]])
            pad_len = jnp.concatenate([padded - counts, n_slots - pends[-1:]])
            xs, gs = _dispatch(*runs, pad_start, pad_len, hn, meta, ltri, upper, n_slots, tm_tok, tm_moe)
            ye = _moe_ffn(tile_exp, n_used.reshape(1), n_valid, xs, moe_wg, moe_wu, moe_wd, j, tm_moe, fc)
            h = _combine_ple(*runs, h1, meta, gs, ltri, upper, *ple, ye, i, s, tm_tok, final)
    return h.reshape(bsz, s, D_MODEL)
```

```python
import functools

import numpy as np
import jax
import jax.numpy as jnp
from jax import lax
from jax.experimental import pallas as pl
from jax.experimental.pallas import tpu as pltpu

F32 = jnp.float32
BF16 = jnp.bfloat16

D_MODEL = 1024
DEPTH = 4
PLE_DIM = 256
CONV_WIDTH = 256
CONV_GROUPS = 4
CONV_KERNEL = 31
SSD_WIDTH = 512
SSD_HEADDIM = 64
SSD_HEADS = 8
SSD_NGROUPS = 2
SSD_STATE = 128
SSD_CONV = 4
SSD_CHUNK = 128
MLA_HEADS = 4
MLA_NOPE = 64
MLA_ROPE = 32
MLA_V = 64
MLA_Q_RANK = 256
MLA_KV_RANK = 128
ROPE_BASE = 10000.0
D_FF = 3584
N_EXPERTS = 8
RMS_EPS = 1e-6
LN_EPS = 1e-5

LANES = 128
HEAD_PAD = 128
ROPE_LO = MLA_NOPE
VMEM_LIMIT = 48 * 1024 * 1024

C_CONV = 0
C_Z = 512
C_XBC = 1024
C_CQ = 2048
C_CKV = 2304
C_KR = 2432
C_DTR = 2560
IN_COLS_PAD = 2688


def _cparams(sem):
    return pltpu.CompilerParams(dimension_semantics=sem, vmem_limit_bytes=VMEM_LIMIT)


def _dot(a, b):
    return jnp.dot(a, b, preferred_element_type=F32)


def _dot_nt(a, b):
    return lax.dot_general(a, b, (((1,), (1,)), ((), ())), preferred_element_type=F32)


def _split3(a):
    a1 = a.astype(BF16)
    r1 = a - a1.astype(F32)
    a2 = r1.astype(BF16)
    a3 = (r1 - a2.astype(F32)).astype(BF16)
    return a1, a2, a3


def _dot_f32_lhs(a, m):
    a1, a2, a3 = _split3(a)
    return _dot(a1, m) + _dot(a2, m) + _dot(a3, m)


def _dot_f32_rhs(m, b):
    b1, b2, b3 = _split3(b)
    return _dot(m, b1) + _dot(m, b2) + _dot(m, b3)


def _rms(x, g, eps=RMS_EPS):
    return x * lax.rsqrt(jnp.mean(x * x, axis=-1, keepdims=True) + eps) * g


def _sigmoid(x):
    return 1.0 / (1.0 + jnp.exp(-x))


def _silu(x):
    return x * _sigmoid(x)


def _rope_kernel(pos_ref, inv_ref, c_ref, s_ref):
    ang = pos_ref[...] * inv_ref[...]
    lane = lax.broadcasted_iota(jnp.int32, ang.shape, 1)
    in_rope = (lane >= ROPE_LO) & (lane < ROPE_LO + MLA_ROPE)
    first_half = lane < ROPE_LO + MLA_ROPE // 2
    cos = jnp.cos(ang)
    sin = jnp.sin(ang)
    c_ref[...] = jnp.where(in_rope, cos, jnp.where(lane < ROPE_LO, 1.0, 0.0))
    s_ref[...] = jnp.where(in_rope, jnp.where(first_half, -sin, sin), 0.0)


def _rope_tables(pos128, inv128, tm):
    t = pos128.shape[0]
    return pl.pallas_call(
        _rope_kernel,
        grid=(t // tm,),
        in_specs=[pl.BlockSpec((tm, LANES), lambda i: (i, 0)),
                  pl.BlockSpec((1, LANES), lambda i: (0, 0))],
        out_specs=[pl.BlockSpec((tm, LANES), lambda i: (i, 0))] * 2,
        out_shape=[jax.ShapeDtypeStruct((t, LANES), F32)] * 2,
        compiler_params=_cparams(("parallel",)),
        name="rope_tables",
    )(pos128, inv128)


def _inproj_kernel(h_ref, g_ref, w_ref, c_ref, s_ref, gq_ref, wq_ref, gkv_ref, wk_ref, wvt_ref,
                   oconv_ref, oz_ref, oxbc_ref, odt_ref, oq_ref, ok_ref, ovt_ref):
    xn = _rms(h_ref[...], g_ref[...]).astype(BF16)
    oconv_ref[...] = _dot(xn, w_ref[0, :, C_CONV:C_Z])
    oz_ref[...] = _dot(xn, w_ref[0, :, C_Z:C_XBC])
    oxbc_ref[...] = _dot(xn, w_ref[0, :, C_XBC:C_CQ])
    cq = _dot(xn, w_ref[0, :, C_CQ:C_CKV])
    ckv = _dot(xn, w_ref[0, :, C_CKV:C_KR])
    kr = _dot(xn, w_ref[0, :, C_KR:C_DTR])
    dtr = _dot(xn, w_ref[0, :, C_DTR:IN_COLS_PAD])
    odt_ref[...] = dtr
    c = c_ref[...]
    s = s_ref[...]
    c4 = jnp.concatenate([c] * MLA_HEADS, axis=1)
    s4 = jnp.concatenate([s] * MLA_HEADS, axis=1)
    qq = _dot(_rms(cq, gq_ref[...]).astype(BF16), wq_ref[...])
    nq = MLA_HEADS * HEAD_PAD
    scale = (MLA_NOPE + MLA_ROPE) ** -0.5 * np.log2(np.e)
    oq_ref[...] = ((qq[:, :nq] * c4 + qq[:, nq:] * s4) * scale).astype(BF16)
    ckvn = _rms(ckv, gkv_ref[...]).astype(BF16)
    kpe = kr * c + dtr * s
    ok_ref[...] = (_dot(ckvn, wk_ref[...]) + jnp.concatenate([kpe] * MLA_HEADS, axis=1)).astype(BF16)
    ovt_ref[...] = _dot_nt(wvt_ref[...], ckvn).astype(BF16)


def _inproj(h, g, w, layer, ctab, stab, gq, wq, gkv, wk, wvt, tm):
    t = h.shape[0]
    row = lambda n: pl.BlockSpec((tm, n), lambda i: (i, 0))
    full = lambda a: pl.BlockSpec(a.shape, lambda i: (0,) * a.ndim)
    nq = MLA_HEADS * HEAD_PAD
    nv = MLA_HEADS * MLA_V
    widths = (512, 512, 1024, LANES, nq, nq)
    dtypes = (F32, F32, F32, F32, BF16, BF16)
    return pl.pallas_call(
        _inproj_kernel,
        grid=(t // tm,),
        in_specs=[row(D_MODEL), full(g), pl.BlockSpec((1,) + w.shape[1:], lambda i: (layer, 0, 0)),
                  row(LANES), row(LANES), full(gq), full(wq), full(gkv), full(wk), full(wvt)],
        out_specs=[row(n) for n in widths] + [pl.BlockSpec((nv, tm), lambda i: (0, i))],
        out_shape=[jax.ShapeDtypeStruct((t, n), d) for n, d in zip(widths, dtypes)]
        + [jax.ShapeDtypeStruct((nv, t), BF16)],
        compiler_params=_cparams(("parallel",)),
        name="inproj",
    )(h, g, w, ctab, stab, gq, wq, gkv, wk, wvt)


CONV_HALO = 32
CONV_SUB = 64


def _conv_kernel(u_ref, w_ref, b_ref, lg_ref, lb_ref, gm_ref, o_ref, gbuf, shifted, *, tc):
    @pl.when(pl.program_id(1) == 0)
    def _():
        gbuf[0:CONV_HALO, :] = jnp.zeros((CONV_HALO, CONV_WIDTH), F32)

    u = u_ref[0]
    gbuf[CONV_HALO:CONV_HALO + tc, :] = u[:, :CONV_WIDTH] * _sigmoid(u[:, CONV_WIDTH:])
    gm = gm_ref[...]
    first = CONV_HALO - (CONV_KERNEL - 1)
    span = CONV_HALO + tc - 8
    for s in range(1, 8):
        shifted[s - 1, 0:span, :] = gbuf[s:s + span, :]
    for r0 in range(0, tc, CONV_SUB):
        acc = jnp.broadcast_to(b_ref[...], (CONV_SUB, CONV_WIDTH))
        for j in range(CONV_KERNEL):
            start = first + j + r0
            s, a = start % 8, start - start % 8
            assert a + CONV_SUB <= span or s == 0
            win = gbuf[a:a + CONV_SUB, :] if s == 0 else shifted[s - 1, a:a + CONV_SUB, :]
            acc = acc + w_ref[j:j + 1, :] * win
        mu = _dot_f32_lhs(acc, gm)
        d = acc - mu
        var = _dot_f32_lhs(d * d, gm)
        hn = d * lax.rsqrt(var + LN_EPS) * lg_ref[...] + lb_ref[...]
        o_ref[0, r0:r0 + CONV_SUB, :] = _silu(hn).astype(BF16)
    gbuf[0:CONV_HALO, :] = gbuf[tc:tc + CONV_HALO, :]


def _conformer_conv(u, w, b, lg, lb, gm, tc):
    bsz, s, _ = u.shape
    full = lambda a: pl.BlockSpec(a.shape, lambda i, j: (0,) * a.ndim)
    return pl.pallas_call(
        functools.partial(_conv_kernel, tc=tc),
        grid=(bsz, s // tc),
        in_specs=[pl.BlockSpec((1, tc, 2 * CONV_WIDTH), lambda i, j: (i, j, 0)),
                  full(w), full(b), full(lg), full(lb), full(gm)],
        out_specs=pl.BlockSpec((1, tc, CONV_WIDTH), lambda i, j: (i, j, 0)),
        out_shape=jax.ShapeDtypeStruct((bsz, s, CONV_WIDTH), BF16),
        scratch_shapes=[pltpu.VMEM((CONV_HALO + tc, CONV_WIDTH), F32),
                        pltpu.VMEM((7, CONV_HALO + tc, CONV_WIDTH), F32)],
        compiler_params=_cparams(("parallel", "arbitrary")),
        name="conformer_conv",
    )(u, w, b, lg, lb, gm)


SSD_HALO = 8
SSD_XBC = SSD_WIDTH + 2 * SSD_NGROUPS * SSD_STATE
GROUP_W = SSD_WIDTH // SSD_NGROUPS
HEADS_PER_GROUP = SSD_HEADS // SSD_NGROUPS


def _ssd_chunk(r0, cbuf, z_ref, dt_ref, cw_ref, cb_ref, dtb_ref, alog_ref, dsk_ref, ng_ref, tril_ref, exp_ref,
               o_ref, state):
    L = SSD_CHUNK
    first = SSD_HALO - (SSD_CONV - 1) + r0
    acc = jnp.broadcast_to(cb_ref[...], (L, SSD_XBC))
    for j in range(SSD_CONV):
        acc = acc + cw_ref[j:j + 1, :] * cbuf[first + j:first + j + L, :]
    xc = _silu(acc)
    xs = xc[:, :SSD_WIDTH]
    bm = xc[:, SSD_WIDTH:SSD_WIDTH + SSD_NGROUPS * SSD_STATE]
    cm = xc[:, SSD_WIDTH + SSD_NGROUPS * SSD_STATE:]

    lane = lax.broadcasted_iota(jnp.int32, (1, LANES), 1)
    v = dt_ref[0, r0:r0 + L, :] + dtb_ref[...]
    dt = jnp.maximum(v, 0.0) + jnp.log1p(jnp.exp(-jnp.abs(v)))
    a = jnp.where(lane < SSD_HEADS, -jnp.exp(alog_ref[...]), 0.0)
    cs = _dot_f32_rhs(tril_ref[...], dt * a)
    cs_t = cs.T
    cs_last = cs[L - 1:L, :]
    emat = exp_ref[...]
    dt_e = _dot_f32_lhs(dt, emat)
    ecs_e = _dot_f32_lhs(jnp.exp(cs), emat)
    ds_e = _dot_f32_lhs(jnp.exp(cs_last - cs), emat)
    cd_e = _dot_f32_lhs(jnp.broadcast_to(jnp.exp(cs_last), (8, LANES)), emat)[0:1, :]

    xd = xs * dt_e
    xdb = xd.astype(BF16)
    xds = (xd * ds_e).astype(BF16)
    rows = lax.broadcasted_iota(jnp.int32, (L, L), 0)
    cols = lax.broadcasted_iota(jnp.int32, (L, L), 1)
    causal = rows >= cols
    ys = []
    for g in range(SSD_NGROUPS):
        cmg = cm[:, g * SSD_STATE:(g + 1) * SSD_STATE].astype(BF16)
        bmg = bm[:, g * SSD_STATE:(g + 1) * SSD_STATE]
        cbm = _dot_nt(cmg, bmg.astype(BF16))
        yd = []
        for r in range(HEADS_PER_GROUP):
            h = g * HEADS_PER_GROUP + r
            seg = cs[:, h:h + 1] - cs_t[h:h + 1, :]
            dec = jnp.exp(jnp.where(causal, seg, -jnp.inf))
            mix = (cbm * dec).astype(BF16)
            yd.append(_dot(mix, xdb[:, h * SSD_HEADDIM:(h + 1) * SSD_HEADDIM]))
        gs = slice(g * GROUP_W, (g + 1) * GROUP_W)
        prev = state[g]
        y_off = _dot(cmg, prev.astype(BF16)) * ecs_e[:, gs]
        st_new = _dot(bmg.T.astype(BF16), xds[:, gs])
        state[g] = prev * cd_e[:, gs] + st_new
        ys.append(jnp.concatenate(yd, axis=1) + y_off)
    y = jnp.concatenate(ys, axis=1) + dsk_ref[...] * xs
    yg = y * _silu(z_ref[0, r0:r0 + L, :])
    outs = []
    for g in range(SSD_NGROUPS):
        ygg = yg[:, g * GROUP_W:(g + 1) * GROUP_W]
        outs.append(ygg * lax.rsqrt(jnp.mean(ygg * ygg, axis=-1, keepdims=True) + RMS_EPS))
    o_ref[0, r0:r0 + L, :] = (jnp.concatenate(outs, axis=1) * ng_ref[...]).astype(BF16)


def _ssd_kernel(xbc_ref, z_ref, dt_ref, cw_ref, cb_ref, dtb_ref, alog_ref, dsk_ref, ng_ref, tril_ref, exp_ref,
                o_ref, cbuf, state, *, rows):
    @pl.when(pl.program_id(1) == 0)
    def _():
        cbuf[0:SSD_HALO, :] = jnp.zeros((SSD_HALO, SSD_XBC), F32)
        state[...] = jnp.zeros(state.shape, F32)

    cbuf[SSD_HALO:SSD_HALO + rows, :] = xbc_ref[0]
    for r0 in range(0, rows, SSD_CHUNK):
        _ssd_chunk(r0, cbuf, z_ref, dt_ref, cw_ref, cb_ref, dtb_ref, alog_ref, dsk_ref, ng_ref, tril_ref, exp_ref,
                   o_ref, state)
    cbuf[0:SSD_HALO, :] = cbuf[rows:rows + SSD_HALO, :]


def _ssd(xbc, z, dtm, cw, cb, dtb, alog, dsk, ng, tril, emat, rows):
    bsz, s, _ = xbc.shape
    full = lambda a: pl.BlockSpec(a.shape, lambda i, j: (0,) * a.ndim)
    blk = lambda n: pl.BlockSpec((1, rows, n), lambda i, j: (i, j, 0))
    return pl.pallas_call(
        functools.partial(_ssd_kernel, rows=rows),
        grid=(bsz, s // rows),
        in_specs=[blk(SSD_XBC), blk(SSD_WIDTH), blk(LANES),
                  full(cw), full(cb), full(dtb), full(alog), full(dsk), full(ng), full(tril), full(emat)],
        out_specs=blk(SSD_WIDTH),
        out_shape=jax.ShapeDtypeStruct((bsz, s, SSD_WIDTH), BF16),
        scratch_shapes=[pltpu.VMEM((SSD_HALO + rows, SSD_XBC), F32),
                        pltpu.VMEM((SSD_NGROUPS, SSD_STATE, GROUP_W), F32)],
        compiler_params=_cparams(("parallel", "arbitrary")),
        name="ssd",
    )(xbc, z, dtm, cw, cb, dtb, alog, dsk, ng, tril, emat)


def _attn_kernel(q_ref, k_ref, vt_ref, o_ref, m_sc, l_sc, acc_sc, st0, st1, *, tq, tk):
    i = pl.program_id(1)
    krow = lax.broadcasted_iota(jnp.int32, (tk, tq), 0)
    qcol = lax.broadcasted_iota(jnp.int32, (tk, tq), 1)
    ones = jnp.ones((16, tk), BF16)
    m_sc[...] = jnp.full(m_sc.shape, -jnp.inf, F32)
    l_sc[...] = jnp.zeros(l_sc.shape, F32)
    acc_sc[...] = jnp.zeros(acc_sc.shape, F32)

    def scores(j, st_ref):
        start = pl.multiple_of(j * tk, tk)
        for h in range(MLA_HEADS):
            hs = slice(h * HEAD_PAD, (h + 1) * HEAD_PAD)
            st_ref[h] = _dot_nt(k_ref[pl.ds(start, tk), hs], q_ref[:, hs])

    def update(j, st_ref, key0):
        start = pl.multiple_of(j * tk, tk)
        for h in range(MLA_HEADS):
            vs = slice(h * MLA_V, (h + 1) * MLA_V)
            st = st_ref[h]
            if key0 is not None:
                st = jnp.where(krow + key0 <= qcol, st, -jnp.inf)
            m = m_sc[h:h + 1, :]
            m_new = jnp.maximum(m, jnp.max(st, axis=0, keepdims=True))
            p = jnp.exp2(st - m_new).astype(BF16)
            alpha = jnp.exp2(m - m_new)
            m_sc[h:h + 1, :] = m_new
            lhs = jnp.concatenate([vt_ref[vs, pl.ds(start, tk)], ones], axis=0)
            pv = _dot(lhs, p)
            l_sc[h:h + 1, :] = alpha * l_sc[h:h + 1, :] + pv[MLA_V:MLA_V + 1, :]
            acc_sc[vs, :] = alpha * acc_sc[vs, :] + pv[:MLA_V, :]

    def pair(jp, c):
        j0 = 2 * jp
        scores(j0 + 1, st1)
        update(j0, st0, None)
        scores(j0 + 2, st0)
        update(j0 + 1, st1, None)
        return c

    scores(0, st0)
    lax.fori_loop(0, i, pair, 0)
    scores(2 * i + 1, st1)
    update(2 * i, st0, 0)
    update(2 * i + 1, st1, tk)
    outs = [acc_sc[h * MLA_V:(h + 1) * MLA_V, :] / l_sc[h:h + 1, :] for h in range(MLA_HEADS)]
    o_ref[...] = jnp.concatenate(outs, axis=0).T.astype(BF16)


def _attention(q, k, vt, bsz, tk):
    t = q.shape[0]
    s = t // bsz
    tq = 2 * tk
    nblk = s // tq
    nq = MLA_HEADS * HEAD_PAD
    nv = MLA_HEADS * MLA_V
    return pl.pallas_call(
        functools.partial(_attn_kernel, tq=tq, tk=tk),
        grid=(bsz, nblk),
        in_specs=[pl.BlockSpec((tq, nq), lambda b, i: (b * nblk + i, 0)),
                  pl.BlockSpec((s, nq), lambda b, i: (b, 0)),
                  pl.BlockSpec((nv, s), lambda b, i: (0, b))],
        out_specs=pl.BlockSpec((tq, nv), lambda b, i: (b * nblk + i, 0)),
        out_shape=jax.ShapeDtypeStruct((t, nv), BF16),
        scratch_shapes=[pltpu.VMEM((8, tq), F32), pltpu.VMEM((8, tq), F32), pltpu.VMEM((nv, tq), F32),
                        pltpu.VMEM((MLA_HEADS, tk, tq), F32), pltpu.VMEM((MLA_HEADS, tk, tq), F32)],
        compiler_params=_cparams(("parallel", "parallel")),
        name="mla_attention",
    )(q, k, vt)


M_IDX0, M_IDX1, M_GATE0, M_GATE1 = range(4)


def _outproj_body(yc, ys, ym, h, w_ref, g_ref):
    acc = _dot(yc, w_ref[0:CONV_WIDTH, :])
    acc = acc + _dot(ys, w_ref[CONV_WIDTH:CONV_WIDTH + SSD_WIDTH, :])
    acc = acc + _dot(ym, w_ref[CONV_WIDTH + SSD_WIDTH:, :])
    h1 = h + acc
    return h1, _rms(h1, g_ref[...])


def _outproj_moe_kernel(yc_ref, ys_ref, ym_ref, h_ref, w_ref, g_ref, r_ref,
                        h1_ref, hn_ref, meta_ref, tcnt_ref, cnt_ref, *, sub):
    @pl.when(pl.program_id(0) == 0)
    def _():
        cnt_ref[...] = jnp.zeros(cnt_ref.shape, F32)

    cnt = cnt_ref[...]
    for k, r0 in enumerate(range(0, h_ref.shape[0], sub)):
        rs = slice(r0, r0 + sub)
        h1, hn = _outproj_body(yc_ref[rs, :], ys_ref[rs, :], ym_ref[rs, :], h_ref[rs, :], w_ref, g_ref)
        h1_ref[rs, :] = h1
        hnb = hn.astype(BF16)
        hn_ref[rs, :] = hnb
        logits = _dot(hnb, r_ref[...])
        lane = lax.broadcasted_iota(jnp.int32, logits.shape, 1)
        lm = jnp.where(lane < N_EXPERTS, logits, -jnp.inf)
        m1 = jnp.max(lm, axis=-1, keepdims=True)
        i1 = jnp.min(jnp.where(lm == m1, lane, LANES), axis=-1, keepdims=True)
        lm2 = jnp.where(lane == i1, -jnp.inf, lm)
        m2 = jnp.max(lm2, axis=-1, keepdims=True)
        i2 = jnp.min(jnp.where(lm2 == m2, lane, LANES), axis=-1, keepdims=True)
        e = jnp.exp(m2 - m1)
        g1 = 1.0 / (1.0 + e)
        g2 = e / (1.0 + e)
        onehot = jnp.where((lane == i1) | (lane == i2), 1.0, 0.0)
        tcnt_ref[k] = cnt
        cnt = cnt + jnp.sum(onehot, axis=0, keepdims=True)
        meta = jnp.where(lane == M_IDX0, i1.astype(F32), 0.0)
        meta = jnp.where(lane == M_IDX1, i2.astype(F32), meta)
        meta = jnp.where(lane == M_GATE0, g1, meta)
        meta = jnp.where(lane == M_GATE1, g2, meta)
        meta_ref[rs, :] = meta
    cnt_ref[...] = cnt


def _outproj_moe(yc, ys, ym, h, w, g, tm, sub, router):
    t = h.shape[0]
    row = lambda n: pl.BlockSpec((tm, n), lambda i: (i, 0))
    full = lambda a: pl.BlockSpec(a.shape, lambda i: (0,) * a.ndim)
    return pl.pallas_call(
        functools.partial(_outproj_moe_kernel, sub=sub),
        grid=(t // tm,),
        in_specs=[row(CONV_WIDTH), row(SSD_WIDTH), row(MLA_HEADS * MLA_V), row(D_MODEL), full(w), full(g),
                  full(router)],
        out_specs=[row(D_MODEL), row(D_MODEL), row(LANES),
                   pl.BlockSpec((tm // sub, 8, LANES), lambda i: (i, 0, 0)), pl.BlockSpec((8, LANES), lambda i: (0, 0))],
        out_shape=[jax.ShapeDtypeStruct((t, D_MODEL), F32), jax.ShapeDtypeStruct((t, D_MODEL), BF16),
                   jax.ShapeDtypeStruct((t, LANES), F32), jax.ShapeDtypeStruct((t // sub, 8, LANES), F32),
                   jax.ShapeDtypeStruct((8, LANES), F32)],
        compiler_params=_cparams(("arbitrary",)),
        name="outproj_moe",
    )(yc, ys, ym, h, w, g, router)


def _ple_update(h, p, pg_ref, pwg_ref, pwp_ref, fg_ref, final):
    gate = _sigmoid(_dot(_rms(h, pg_ref[...]).astype(BF16), pwg_ref[...]))
    out = h + _dot(p.astype(BF16), pwp_ref[...]) * gate
    if final:
        out = _rms(out, fg_ref[...])
    return out


def _p_spec(tm, seq, layer):
    tps = seq // tm
    return pl.BlockSpec((1, 1, tm, PLE_DIM), lambda i, *_: (layer, i // tps, i % tps, 0))


def _ffn_kernel(yc_ref, ys_ref, ym_ref, h_ref, wo_ref, g_ref, wg_ref, wu_ref, wd_ref,
                p_ref, pg_ref, pwg_ref, pwp_ref, fg_ref, o_ref, hn_sc, *, final):
    f = pl.program_id(1)
    half = hn_sc.shape[0] // 2
    halves = [slice(r0, r0 + half) for r0 in (0, half)]

    @pl.when(f == 0)
    def _():
        for rs in halves:
            h1, hn = _outproj_body(yc_ref[rs, :], ys_ref[rs, :], ym_ref[rs, :], h_ref[rs, :], wo_ref, g_ref)
            o_ref[rs, :] = h1
            hn_sc[rs, :] = hn.astype(BF16)

    wg, wu, wd = wg_ref[0].astype(BF16), wu_ref[0].astype(BF16), wd_ref[0].astype(BF16)
    for rs in halves:
        x = hn_sc[rs, :]
        mid = (_silu(_dot(x, wg)) * _dot(x, wu)).astype(BF16)
        o_ref[rs, :] += _dot(mid, wd)

    @pl.when(f == pl.num_programs(1) - 1)
    def _():
        for rs in halves:
            o_ref[rs, :] = _ple_update(o_ref[rs, :], p_ref[0, 0, rs, :], pg_ref, pwg_ref, pwp_ref, fg_ref, final)


def _outproj_ffn_ple_dense(yc, ys, ym, h, wo, g, wg, wu, wd, p, pg, pwg, pwp, fg, layer, ffn_layer, seq, tm, fc, final):
    t = h.shape[0]
    row = lambda n: pl.BlockSpec((tm, n), lambda i, f: (i, 0))
    full = lambda a: pl.BlockSpec(a.shape, lambda i, f: (0,) * a.ndim)
    return pl.pallas_call(
        functools.partial(_ffn_kernel, final=final),
        grid=(t // tm, D_FF // fc),
        in_specs=[row(CONV_WIDTH), row(SSD_WIDTH), row(MLA_HEADS * MLA_V), row(D_MODEL), full(wo), full(g),
                  pl.BlockSpec((1, D_MODEL, fc), lambda i, f: (ffn_layer, 0, f)),
                  pl.BlockSpec((1, D_MODEL, fc), lambda i, f: (ffn_layer, 0, f)),
                  pl.BlockSpec((1, fc, D_MODEL), lambda i, f: (ffn_layer, f, 0)),
                  _p_spec(tm, seq, layer), full(pg), full(pwg), full(pwp), full(fg)],
        out_specs=row(D_MODEL),
        out_shape=jax.ShapeDtypeStruct((t, D_MODEL), F32),
        scratch_shapes=[pltpu.VMEM((tm, D_MODEL), BF16)],
        compiler_params=_cparams(("parallel", "arbitrary")),
        name="outproj_ffn_ple_dense",
    )(yc, ys, ym, h, wo, g, wg, wu, wd, p, pg, pwg, pwp, fg)


ROW_TILES = D_MODEL // LANES


def _rows_to_tiles(val):
    blocks = jnp.stack([val[:, c * LANES:(c + 1) * LANES] for c in range(ROW_TILES)], axis=0)
    return jnp.transpose(blocks, (1, 0, 2))


def _tiles_to_rows(val):
    blocks = jnp.transpose(val, (1, 0, 2))
    return jnp.concatenate([blocks[c] for c in range(ROW_TILES)], axis=1)


def _run_bits(tm):
    return [1 << b for b in range(tm.bit_length() - 1, -1, -1)]


def _run_copies(n, src_ref, src0, dst_ref, dst0, sem, tm, wait, src_step=1):
    off = 0
    for b in _run_bits(tm):
        part = n & b

        @pl.when(part != 0)
        def _(off=off, b=b):
            cp = pltpu.make_async_copy(src_ref.at[pl.ds(src0 + off * src_step, b)],
                                       dst_ref.at[pl.ds(dst0 + off, b)], sem)
            if wait:
                cp.wait()
            else:
                cp.start()

        off = off + part


def _sorted_positions(meta, ltri, upper):
    lane = lax.broadcasted_iota(jnp.int32, meta.shape, 1)
    i1 = meta[:, M_IDX0:M_IDX0 + 1].astype(jnp.int32)
    i2 = meta[:, M_IDX1:M_IDX1 + 1].astype(jnp.int32)
    onehot = jnp.where((lane == i1) | (lane == i2), 1.0, 0.0)
    before = _dot(ltri, onehot.astype(BF16))
    n = jnp.broadcast_to(jnp.sum(onehot, axis=0, keepdims=True), (8, LANES))
    loff = _dot_f32_lhs(n, upper)[0:1, :]
    pos = before + loff
    q1 = jnp.sum(jnp.where(lane == i1, pos, 0.0), axis=-1, keepdims=True)
    q2 = jnp.sum(jnp.where(lane == i2, pos, 0.0), axis=-1, keepdims=True)
    return q1, q2


def _dispatch_kernel(n_ref, lo_ref, g_ref, ps_ref, pn_ref, x_ref, meta_ref, ltri_ref, up_ref, xs_ref, gs_ref,
                     xbuf, zbuf, sems, zsem, *, tm, tiles):
    i = pl.program_id(0)
    nt = pl.num_programs(0)
    slot = i % 2
    zrows = zbuf.shape[0]

    def pads(wait):
        for e in range(N_EXPERTS):
            _run_copies(pn_ref[e], zbuf, 0, xs_ref, ps_ref[e], zsem, zrows, wait, src_step=0)
        tail0, tail_len = ps_ref[N_EXPERTS], pn_ref[N_EXPERTS]
        for c in range(2 * N_EXPERTS):

            @pl.when(c * zrows < tail_len)
            def _(c=c):
                cp = pltpu.make_async_copy(zbuf, xs_ref.at[pl.ds(tail0 + c * zrows, zrows)], zsem)
                if wait:
                    cp.wait()
                else:
                    cp.start()

    @pl.when(i == 0)
    def _():
        zbuf[...] = jnp.zeros(zbuf.shape, F32)
        pads(False)

    srow = lax.broadcasted_iota(jnp.int32, (2 * tm, tm), 0).astype(F32)
    for k in range(tiles):
        rs = slice(k * tm, (k + 1) * tm)
        ss = slice(2 * k * tm, 2 * (k + 1) * tm)
        meta = meta_ref[rs, :]
        q1, q2 = _sorted_positions(meta, ltri_ref[...], up_ref[...])
        lane = lax.broadcasted_iota(jnp.int32, meta.shape, 1)
        qmat = jnp.where(lane == 0, q1, jnp.where(lane == 1, q2, 0.0))
        qt = qmat.T
        p1 = jnp.where(srow == qt[0:1, :], 1.0, 0.0).astype(BF16)
        p2 = jnp.where(srow == qt[1:2, :], 1.0, 0.0).astype(BF16)
        g1 = jnp.broadcast_to(meta[:, M_GATE0:M_GATE0 + 1], (tm, LANES))
        g2 = jnp.broadcast_to(meta[:, M_GATE1:M_GATE1 + 1], (tm, LANES))
        gs_ref[ss, :] = _dot_f32_rhs(p1, g1) + _dot_f32_rhs(p2, g2)
        xbuf[slot, ss] = _rows_to_tiles(_dot(p1 + p2, x_ref[rs, :]))
        for e in range(N_EXPERTS):
            r = (i * tiles + k) * N_EXPERTS + e
            _run_copies(n_ref[r], xbuf.at[slot], 2 * k * tm + lo_ref[r], xs_ref, g_ref[r], sems.at[slot], tm, False)

    def wait_step(sl):
        pltpu.make_async_copy(xbuf.at[sl], xs_ref.at[pl.ds(0, 2 * tm * tiles)], sems.at[sl]).wait()

    @pl.when(i > 0)
    def _():
        wait_step(1 - slot)

    @pl.when(i == nt - 1)
    def _():
        wait_step(slot)
        pads(True)


def _dispatch(n_run, lo_run, g_run, pad_start, pad_len, hn, meta, ltri, upper, n_slots, tm, tiles, tm_moe):
    t = hn.shape[0]
    rows = tm * tiles
    full = lambda a: pl.BlockSpec(a.shape, lambda i, *_: (0,) * a.ndim)
    grid_spec = pltpu.PrefetchScalarGridSpec(
        num_scalar_prefetch=5,
        grid=(t // rows,),
        in_specs=[pl.BlockSpec((rows, D_MODEL), lambda i, *_: (i, 0)),
                  pl.BlockSpec((rows, LANES), lambda i, *_: (i, 0)),
                  full(ltri), full(upper)],
        out_specs=[pl.BlockSpec(memory_space=pl.ANY), pl.BlockSpec((2 * rows, LANES), lambda i, *_: (i, 0))],
        scratch_shapes=[pltpu.VMEM((2, 2 * rows, ROW_TILES, LANES), F32),
                        pltpu.VMEM((tm_moe // 2, ROW_TILES, LANES), F32),
                        pltpu.SemaphoreType.DMA((2,)), pltpu.SemaphoreType.DMA],
    )
    return pl.pallas_call(
        functools.partial(_dispatch_kernel, tm=tm, tiles=tiles),
        grid_spec=grid_spec,
        out_shape=[jax.ShapeDtypeStruct((n_slots, ROW_TILES, LANES), F32),
                   jax.ShapeDtypeStruct((2 * t, LANES), F32)],
        compiler_params=_cparams(("arbitrary",)),
        name="moe_dispatch",
    )(n_run, lo_run, g_run, pad_start, pad_len, hn, meta, ltri, upper)


def _moe_ffn_kernel(texp_ref, nused_ref, nvalid_ref, x_ref, wg_ref, wu_ref, wd_ref, o_ref, xb, acc, *, tm):
    del texp_ref, nused_ref
    i = pl.program_id(0)
    f = pl.program_id(1)
    half = tm // 2

    @pl.when(f == 0)
    def _():
        xb[...] = _tiles_to_rows(x_ref[...]).astype(BF16)
        acc[...] = jnp.zeros(acc.shape, F32)

    def rows(n):
        wg, wu, wd = wg_ref[0, 0].astype(BF16), wu_ref[0, 0].astype(BF16), wd_ref[0, 0].astype(BF16)
        step = min(n, half)
        for r0 in range(0, n, step):
            x = xb[r0:r0 + step, :]
            mid = (_silu(_dot(x, wg)) * _dot(x, wu)).astype(BF16)
            acc[r0:r0 + step, :] += _dot(mid, wd)

    nvalid = nvalid_ref[i]
    quarter = half // 2

    @pl.when(nvalid > half)
    def _():
        rows(tm)

    @pl.when((nvalid > quarter) & (nvalid <= half))
    def _():
        rows(half)

    @pl.when((nvalid > 0) & (nvalid <= quarter))
    def _():
        rows(quarter)

    @pl.when(f == pl.num_programs(1) - 1)
    def _():
        o_ref[...] = _rows_to_tiles(acc[...])


def _moe_ffn(tile_exp, n_used, n_valid, xs, wg, wu, wd, layer, tm, fc):
    n_slots = xs.shape[0]
    nf = D_FF // fc

    def fsel(i, f, nu):
        return jnp.where(i < nu[0], f, nf - 1)

    grid_spec = pltpu.PrefetchScalarGridSpec(
        num_scalar_prefetch=3,
        grid=(n_slots // tm, nf),
        in_specs=[pl.BlockSpec((tm, ROW_TILES, LANES), lambda i, f, te, nu, nv: (jnp.minimum(i, nu[0] - 1), 0, 0)),
                  pl.BlockSpec((1, 1, D_MODEL, fc), lambda i, f, te, nu, nv: (layer, te[i], 0, fsel(i, f, nu))),
                  pl.BlockSpec((1, 1, D_MODEL, fc), lambda i, f, te, nu, nv: (layer, te[i], 0, fsel(i, f, nu))),
                  pl.BlockSpec((1, 1, fc, D_MODEL), lambda i, f, te, nu, nv: (layer, te[i], fsel(i, f, nu), 0))],
        out_specs=pl.BlockSpec((tm, ROW_TILES, LANES), lambda i, f, te, nu, nv: (i, 0, 0)),
        scratch_shapes=[pltpu.VMEM((tm, D_MODEL), BF16), pltpu.VMEM((tm, D_MODEL), F32)],
    )
    return pl.pallas_call(
        functools.partial(_moe_ffn_kernel, tm=tm),
        grid_spec=grid_spec,
        out_shape=jax.ShapeDtypeStruct((n_slots, ROW_TILES, LANES), F32),
        compiler_params=_cparams(("parallel", "arbitrary")),
        name="moe_ffn",
    )(tile_exp, n_used, n_valid, xs, wg, wu, wd)


def _combine_kernel(n_ref, lo_ref, g_ref, h1_ref, meta_ref, gs_ref, ltri_ref, up_ref,
                    p_ref, pg_ref, pwg_ref, pwp_ref, fg_ref, ye_ref, o_ref, ybuf, sems, *, tm, tiles, final):
    i = pl.program_id(0)
    nt = pl.num_programs(0)
    slot = i % 2

    def fetch(step, sl):
        for k in range(tiles):
            for e in range(N_EXPERTS):
                r = (step * tiles + k) * N_EXPERTS + e
                _run_copies(n_ref[r], ye_ref, g_ref[r], ybuf.at[sl], 2 * k * tm + lo_ref[r], sems.at[sl], tm, False)

    @pl.when(i == 0)
    def _():
        fetch(i, slot)

    @pl.when(i + 1 < nt)
    def _():
        fetch(i + 1, 1 - slot)

    scol = lax.broadcasted_iota(jnp.int32, (tm, 2 * tm), 1).astype(F32)
    sels = []
    for k in range(tiles):
        q1, q2 = _sorted_positions(meta_ref[k * tm:(k + 1) * tm, :], ltri_ref[...], up_ref[...])
        sels.append(jnp.where((scol == q1) | (scol == q2), 1.0, 0.0).astype(BF16))
    pltpu.make_async_copy(ye_ref.at[pl.ds(0, 2 * tm * tiles)], ybuf.at[slot], sems.at[slot]).wait()
    for k in range(tiles):
        rs = slice(k * tm, (k + 1) * tm)
        ss = slice(2 * k * tm, 2 * (k + 1) * tm)
        y = _tiles_to_rows(ybuf[slot, ss]) * gs_ref[ss, 0:1]
        hi = y.astype(BF16)
        lo = (y - hi.astype(F32)).astype(BF16)
        h2 = h1_ref[rs, :] + _dot(sels[k], hi) + _dot(sels[k], lo)
        o_ref[rs, :] = _ple_update(h2, p_ref[0, 0, rs, :], pg_ref, pwg_ref, pwp_ref, fg_ref, final)


def _combine_ple(n_run, lo_run, g_run, h1, meta, gs, ltri, upper, p, pg, pwg, pwp, fg, ye, layer, seq, tm, tiles,
                 final):
    t = h1.shape[0]
    rows = tm * tiles
    full = lambda a: pl.BlockSpec(a.shape, lambda i, *_: (0,) * a.ndim)
    grid_spec = pltpu.PrefetchScalarGridSpec(
        num_scalar_prefetch=3,
        grid=(t // rows,),
        in_specs=[pl.BlockSpec((rows, D_MODEL), lambda i, *_: (i, 0)),
                  pl.BlockSpec((rows, LANES), lambda i, *_: (i, 0)),
                  pl.BlockSpec((2 * rows, LANES), lambda i, *_: (i, 0)),
                  full(ltri), full(upper),
                  _p_spec(rows, seq, layer), full(pg), full(pwg), full(pwp), full(fg),
                  pl.BlockSpec(memory_space=pl.ANY)],
        out_specs=pl.BlockSpec((rows, D_MODEL), lambda i, *_: (i, 0)),
        scratch_shapes=[pltpu.VMEM((2, 2 * rows, ROW_TILES, LANES), F32), pltpu.SemaphoreType.DMA((2,))],
    )
    return pl.pallas_call(
        functools.partial(_combine_kernel, tm=tm, tiles=tiles, final=final),
        grid_spec=grid_spec,
        out_shape=jax.ShapeDtypeStruct((t, D_MODEL), F32),
        compiler_params=_cparams(("arbitrary",)),
        name="moe_combine_ple",
    )(n_run, lo_run, g_run, h1, meta, gs, ltri, upper, p, pg, pwg, pwp, fg, ye)


def _swap_halves(w):
    half = w.shape[-1] // 2
    return jnp.concatenate([w[..., half:], w[..., :half]], axis=-1)


def _pad_cols(w, left, total):
    return jnp.pad(w, ((0, 0),) * (w.ndim - 1) + ((left, total - left - w.shape[-1]),))


def _arrange_w_in(w):
    sizes = (512, 512, SSD_XBC, SSD_HEADS, MLA_Q_RANK, MLA_KV_RANK, MLA_ROPE)
    pts = np.cumsum(sizes)[:-1].tolist()
    w = w.astype(BF16)
    w_conv, w_z, w_xbc, w_dt, w_cq, w_ckv, w_kr = jnp.split(w, pts, axis=-1)
    seg_kr = _pad_cols(w_kr, ROPE_LO, LANES)
    seg_dtr = _pad_cols(w_dt, 0, LANES) + _pad_cols(_swap_halves(w_kr), ROPE_LO, LANES)
    return jnp.concatenate([w_conv, w_z, w_xbc, w_cq, w_ckv, seg_kr, seg_dtr], axis=-1)


def _arrange_w_uq(w):
    main, rot = [], []
    for h in range(MLA_HEADS):
        wh = w[:, h * (MLA_NOPE + MLA_ROPE):(h + 1) * (MLA_NOPE + MLA_ROPE)]
        main.append(_pad_cols(wh, 0, HEAD_PAD))
        rot.append(_pad_cols(_swap_halves(wh[:, MLA_NOPE:]), ROPE_LO, HEAD_PAD))
    return jnp.concatenate(main + rot, axis=1).astype(BF16)


def _arrange_w_ukv(w):
    ks, vs = [], []
    for h in range(MLA_HEADS):
        wh = w[:, h * (MLA_NOPE + MLA_V):(h + 1) * (MLA_NOPE + MLA_V)]
        ks.append(_pad_cols(wh[:, :MLA_NOPE], 0, HEAD_PAD))
        vs.append(wh[:, MLA_NOPE:])
    return jnp.concatenate(ks, axis=1).astype(BF16), jnp.concatenate(vs, axis=1).T.astype(BF16)


def _row(v, width=None):
    v = v.reshape(1, -1).astype(F32)
    if width is not None:
        v = jnp.pad(v, ((0, 0), (0, width - v.shape[1])))
    return v


def _pick(n, prefs):
    for c in prefs:
        if n % c == 0:
            return c
    return n


def kernel(x, p, positions, attn_norm_g, w_in, conv_dw_w, conv_dw_b, conv_ln_g, conv_ln_b, ssd_conv_w, ssd_conv_b, ssd_dt_bias, ssd_a_log, ssd_d, ssd_norm_g, mla_q_norm_g, mla_w_uq, mla_kv_norm_g, mla_w_ukv, w_out, ffn_norm_g, dense_w_gate, dense_w_up, dense_w_down, moe_router, moe_w_gate, moe_w_up, moe_w_down, ple_norm_g, ple_w_gate, ple_w_proj, final_norm_g):
    bsz, s, _ = x.shape
    t = bsz * s
    tm_row = _pick(t, (512, 256, 128))
    tm_ffn = _pick(s, (1024, 512, 256, 128))
    fc = 512
    tc = _pick(s, (256, 128))
    ts = _pick(s, (512, 256, 128))
    tm_moe = _pick(t, (1024, 512, 256, 128))
    tm_tok = _pick(s, (256, 128))
    disp_tiles = _pick(s // tm_tok, (4, 2))
    comb_tiles = _pick(s // tm_tok, (2,))
    n_slots = 2 * t + N_EXPERTS * tm_moe

    inv = ROPE_BASE ** (-jnp.arange(0, MLA_ROPE, 2, dtype=F32) / MLA_ROPE)
    inv128 = _pad_cols(jnp.concatenate([inv, inv])[None, :], ROPE_LO, LANES)
    pos128 = jnp.broadcast_to(positions.astype(F32).reshape(t, 1), (t, LANES))
    ctab, stab = _rope_tables(pos128, inv128, tm_row)
    grp = np.arange(CONV_WIDTH) // (CONV_WIDTH // CONV_GROUPS)
    gmean = jnp.asarray((grp[:, None] == grp[None, :]) / (CONV_WIDTH // CONV_GROUPS), BF16)
    tril = jnp.asarray(np.tril(np.ones((SSD_CHUNK, SSD_CHUNK))), BF16)
    hd = np.arange(SSD_WIDTH) // SSD_HEADDIM
    emat = jnp.asarray(np.arange(LANES)[:, None] == hd[None, :], BF16)
    ltri = jnp.asarray(np.tril(np.ones((tm_tok, tm_tok)), -1), BF16)
    upper = jnp.asarray(np.triu(np.ones((LANES, LANES)), 1), BF16)

    w_in_all = _arrange_w_in(w_in)
    dense_wg, dense_wu, dense_wd = dense_w_gate, dense_w_up, dense_w_down
    moe_wg, moe_wu, moe_wd = moe_w_gate, moe_w_up, moe_w_down

    h = x.reshape(t, D_MODEL)
    for i in range(DEPTH):
        u_conv, z, xbc, dtm, q, k, vt = _inproj(
            h, _row(attn_norm_g[i]), w_in_all, i, ctab, stab,
            _row(mla_q_norm_g[i]), _arrange_w_uq(mla_w_uq[i]),
            _row(mla_kv_norm_g[i]), *_arrange_w_ukv(mla_w_ukv[i]), tm_row)
        y_conv = _conformer_conv(
            u_conv.reshape(bsz, s, -1), jnp.pad(conv_dw_w[i], ((0, 1), (0, 0))), _row(conv_dw_b[i]),
            _row(conv_ln_g[i]), _row(conv_ln_b[i]), gmean, ts)
        y_ssd = _ssd(
            xbc.reshape(bsz, s, -1), z.reshape(bsz, s, -1), dtm.reshape(bsz, s, -1),
            jnp.pad(ssd_conv_w[i], ((0, 8 - SSD_CONV), (0, 0))), _row(ssd_conv_b[i]),
            _row(ssd_dt_bias[i], LANES), _row(ssd_a_log[i], LANES),
            _row(jnp.repeat(ssd_d[i], SSD_HEADDIM)), _row(ssd_norm_g[i]), tril, emat, ts)
        ym = _attention(q, k, vt, bsz, tc)
        yc, ys = y_conv.reshape(t, -1), y_ssd.reshape(t, -1)
        wo = w_out[i].astype(BF16)
        j = i // 2
        final = i == DEPTH - 1
        ple = (p, _row(ple_norm_g[i]), ple_w_gate[i].astype(BF16), ple_w_proj[i].astype(BF16), _row(final_norm_g))
        if i % 2 == 0:
            h = _outproj_ffn_ple_dense(yc, ys, ym, h, wo, _row(ffn_norm_g[i]), dense_wg, dense_wu, dense_wd, *ple,
                                       i, j, s, tm_ffn, fc, final)
        else:
            router = _pad_cols(moe_router[j], 0, LANES).astype(BF16)
            h1, hn, meta, tcnt, cnt = _outproj_moe(yc, ys, ym, h, wo, _row(ffn_norm_g[i]), tm_ffn, tm_tok,
                                                   router)
            counts = cnt[0, :N_EXPERTS].astype(jnp.int32)
            padded = ((counts + tm_moe - 1) // tm_moe) * tm_moe
            pends = jnp.cumsum(padded)
            pstarts = pends - padded
            before = tcnt[:, 0, :N_EXPERTS].astype(jnp.int32)
            n_run = jnp.concatenate([before[1:], counts[None, :]], axis=0) - before
            lo_run = jnp.cumsum(n_run, axis=1) - n_run
            g_run = pstarts[None, :] + before
            runs = (n_run.reshape(-1), lo_run.reshape(-1), g_run.reshape(-1))
            n_tiles = n_slots // tm_moe
            n_used = (pends[-1] // tm_moe).astype(jnp.int32)
            tile_start = jnp.arange(n_tiles, dtype=jnp.int32) * tm_moe
            tile_exp = jnp.minimum(jnp.sum(pends[None, :] <= tile_start[:, None], axis=1), N_EXPERTS - 1)
            tile_exp = tile_exp.astype(jnp.int32)
            n_valid = jnp.clip((pstarts + counts)[tile_exp] - tile_start, 0, tm_moe)
            tile_exp = jnp.where(jnp.arange(n_tiles) < n_used, tile_exp, tile_exp[jnp.maximum(n_used - 1, 0)])
            pad_start = jnp.concatenate([pstarts + counts, pends[-1:]])
            pad_len = jnp.concatenate([padded - counts, n_slots - pends[-1:]])
            xs, gs = _dispatch(*runs, pad_start, pad_len, hn, meta, ltri, upper, n_slots, tm_tok, disp_tiles, tm_moe)
            ye = _moe_ffn(tile_exp, n_used.reshape(1), n_valid, xs, moe_wg, moe_wu, moe_wd, j, tm_moe, fc)
            h = _combine_ple(*runs, h1, meta, gs, ltri, upper, *ple, ye, i, s, tm_tok, comb_tiles, final)
    return h.reshape(bsz, s, D_MODEL)
```

```python
import functools

import numpy as np
import jax
import jax.numpy as jnp
from jax import lax
from jax.experimental import pallas as pl
from jax.experimental.pallas import tpu as pltpu

F32 = jnp.float32
BF16 = jnp.bfloat16

D_MODEL = 1024
DEPTH = 4
PLE_DIM = 256
CONV_WIDTH = 256
CONV_GROUPS = 4
CONV_KERNEL = 31
SSD_WIDTH = 512
SSD_HEADDIM = 64
SSD_HEADS = 8
SSD_NGROUPS = 2
SSD_STATE = 128
SSD_CONV = 4
SSD_CHUNK = 128
MLA_HEADS = 4
MLA_NOPE = 64
MLA_ROPE = 32
MLA_V = 64
MLA_Q_RANK = 256
MLA_KV_RANK = 128
ROPE_BASE = 10000.0
D_FF = 3584
N_EXPERTS = 8
RMS_EPS = 1e-6
LN_EPS = 1e-5

LANES = 128
HEAD_PAD = 128
ROPE_LO = MLA_NOPE
VMEM_LIMIT = 48 * 1024 * 1024

C_CONV = 0
C_Z = 512
C_XBC = 1024
C_CQ = 2048
C_CKV = 2304
C_KR = 2432
C_DTR = 2560
IN_COLS_PAD = 2688


def _cparams(sem):
    return pltpu.CompilerParams(dimension_semantics=sem, vmem_limit_bytes=VMEM_LIMIT)


def _dot(a, b):
    return jnp.dot(a, b, preferred_element_type=F32)


def _dot_nt(a, b):
    return lax.dot_general(a, b, (((1,), (1,)), ((), ())), preferred_element_type=F32)


def _split3(a):
    a1 = a.astype(BF16)
    r1 = a - a1.astype(F32)
    a2 = r1.astype(BF16)
    a3 = (r1 - a2.astype(F32)).astype(BF16)
    return a1, a2, a3


def _dot_f32_lhs(a, m):
    a1, a2, a3 = _split3(a)
    return _dot(a1, m) + _dot(a2, m) + _dot(a3, m)


def _dot_f32_rhs(m, b):
    b1, b2, b3 = _split3(b)
    return _dot(m, b1) + _dot(m, b2) + _dot(m, b3)


def _rms(x, g, eps=RMS_EPS):
    return x * lax.rsqrt(jnp.mean(x * x, axis=-1, keepdims=True) + eps) * g


def _sigmoid(x):
    return 1.0 / (1.0 + jnp.exp(-x))


def _silu(x):
    return x * _sigmoid(x)


def _rope_kernel(pos_ref, inv_ref, c_ref, s_ref):
    ang = pos_ref[...] * inv_ref[...]
    lane = lax.broadcasted_iota(jnp.int32, ang.shape, 1)
    in_rope = (lane >= ROPE_LO) & (lane < ROPE_LO + MLA_ROPE)
    first_half = lane < ROPE_LO + MLA_ROPE // 2
    cos = jnp.cos(ang)
    sin = jnp.sin(ang)
    c_ref[...] = jnp.where(in_rope, cos, jnp.where(lane < ROPE_LO, 1.0, 0.0))
    s_ref[...] = jnp.where(in_rope, jnp.where(first_half, -sin, sin), 0.0)


def _rope_tables(pos128, inv128, tm):
    t = pos128.shape[0]
    return pl.pallas_call(
        _rope_kernel,
        grid=(t // tm,),
        in_specs=[pl.BlockSpec((tm, LANES), lambda i: (i, 0)),
                  pl.BlockSpec((1, LANES), lambda i: (0, 0))],
        out_specs=[pl.BlockSpec((tm, LANES), lambda i: (i, 0))] * 2,
        out_shape=[jax.ShapeDtypeStruct((t, LANES), F32)] * 2,
        compiler_params=_cparams(("parallel",)),
        name="rope_tables",
    )(pos128, inv128)


def _inproj_kernel(h_ref, g_ref, w_ref, c_ref, s_ref, gq_ref, wq_ref, gkv_ref, wk_ref, wvt_ref,
                   oconv_ref, oz_ref, oxbc_ref, odt_ref, oq_ref, ok_ref, ovt_ref):
    xn = _rms(h_ref[...], g_ref[...]).astype(BF16)
    oconv_ref[...] = _dot(xn, w_ref[0, :, C_CONV:C_Z])
    oz_ref[...] = _dot(xn, w_ref[0, :, C_Z:C_XBC])
    oxbc_ref[...] = _dot(xn, w_ref[0, :, C_XBC:C_CQ])
    cq = _dot(xn, w_ref[0, :, C_CQ:C_CKV])
    ckv = _dot(xn, w_ref[0, :, C_CKV:C_KR])
    kr = _dot(xn, w_ref[0, :, C_KR:C_DTR])
    dtr = _dot(xn, w_ref[0, :, C_DTR:IN_COLS_PAD])
    odt_ref[...] = dtr
    c = c_ref[...]
    s = s_ref[...]
    c4 = jnp.concatenate([c] * MLA_HEADS, axis=1)
    s4 = jnp.concatenate([s] * MLA_HEADS, axis=1)
    qq = _dot(_rms(cq, gq_ref[...]).astype(BF16), wq_ref[...])
    nq = MLA_HEADS * HEAD_PAD
    scale = (MLA_NOPE + MLA_ROPE) ** -0.5 * np.log2(np.e)
    oq_ref[...] = ((qq[:, :nq] * c4 + qq[:, nq:] * s4) * scale).astype(BF16)
    ckvn = _rms(ckv, gkv_ref[...]).astype(BF16)
    kpe = kr * c + dtr * s
    ok_ref[...] = (_dot(ckvn, wk_ref[...]) + jnp.concatenate([kpe] * MLA_HEADS, axis=1)).astype(BF16)
    ovt_ref[...] = _dot_nt(wvt_ref[...], ckvn).astype(BF16)


def _inproj(h, g, w, layer, ctab, stab, gq, wq, gkv, wk, wvt, tm):
    t = h.shape[0]
    row = lambda n: pl.BlockSpec((tm, n), lambda i: (i, 0))
    full = lambda a: pl.BlockSpec(a.shape, lambda i: (0,) * a.ndim)
    nq = MLA_HEADS * HEAD_PAD
    nv = MLA_HEADS * MLA_V
    widths = (512, 512, 1024, LANES, nq, nq)
    dtypes = (F32, F32, F32, F32, BF16, BF16)
    return pl.pallas_call(
        _inproj_kernel,
        grid=(t // tm,),
        in_specs=[row(D_MODEL), full(g), pl.BlockSpec((1,) + w.shape[1:], lambda i: (layer, 0, 0)),
                  row(LANES), row(LANES), full(gq), full(wq), full(gkv), full(wk), full(wvt)],
        out_specs=[row(n) for n in widths] + [pl.BlockSpec((nv, tm), lambda i: (0, i))],
        out_shape=[jax.ShapeDtypeStruct((t, n), d) for n, d in zip(widths, dtypes)]
        + [jax.ShapeDtypeStruct((nv, t), BF16)],
        compiler_params=_cparams(("parallel",)),
        name="inproj",
    )(h, g, w, ctab, stab, gq, wq, gkv, wk, wvt)


CONV_HALO = 32
CONV_SUB = 64


def _conv_kernel(u_ref, w_ref, b_ref, lg_ref, lb_ref, gm_ref, o_ref, gbuf, shifted, *, tc):
    @pl.when(pl.program_id(1) == 0)
    def _():
        gbuf[0:CONV_HALO, :] = jnp.zeros((CONV_HALO, CONV_WIDTH), F32)

    u = u_ref[0]
    gbuf[CONV_HALO:CONV_HALO + tc, :] = u[:, :CONV_WIDTH] * _sigmoid(u[:, CONV_WIDTH:])
    gm = gm_ref[...]
    first = CONV_HALO - (CONV_KERNEL - 1)
    span = CONV_HALO + tc - 8
    for s in range(1, 8):
        shifted[s - 1, 0:span, :] = gbuf[s:s + span, :]
    for r0 in range(0, tc, CONV_SUB):
        acc = jnp.broadcast_to(b_ref[...], (CONV_SUB, CONV_WIDTH))
        for j in range(CONV_KERNEL):
            start = first + j + r0
            s, a = start % 8, start - start % 8
            assert a + CONV_SUB <= span or s == 0
            win = gbuf[a:a + CONV_SUB, :] if s == 0 else shifted[s - 1, a:a + CONV_SUB, :]
            acc = acc + w_ref[j:j + 1, :] * win
        mu = _dot_f32_lhs(acc, gm)
        d = acc - mu
        var = _dot_f32_lhs(d * d, gm)
        hn = d * lax.rsqrt(var + LN_EPS) * lg_ref[...] + lb_ref[...]
        o_ref[0, r0:r0 + CONV_SUB, :] = _silu(hn).astype(BF16)
    gbuf[0:CONV_HALO, :] = gbuf[tc:tc + CONV_HALO, :]


def _conformer_conv(u, w, b, lg, lb, gm, tc):
    bsz, s, _ = u.shape
    full = lambda a: pl.BlockSpec(a.shape, lambda i, j: (0,) * a.ndim)
    return pl.pallas_call(
        functools.partial(_conv_kernel, tc=tc),
        grid=(bsz, s // tc),
        in_specs=[pl.BlockSpec((1, tc, 2 * CONV_WIDTH), lambda i, j: (i, j, 0)),
                  full(w), full(b), full(lg), full(lb), full(gm)],
        out_specs=pl.BlockSpec((1, tc, CONV_WIDTH), lambda i, j: (i, j, 0)),
        out_shape=jax.ShapeDtypeStruct((bsz, s, CONV_WIDTH), BF16),
        scratch_shapes=[pltpu.VMEM((CONV_HALO + tc, CONV_WIDTH), F32),
                        pltpu.VMEM((7, CONV_HALO + tc, CONV_WIDTH), F32)],
        compiler_params=_cparams(("parallel", "arbitrary")),
        name="conformer_conv",
    )(u, w, b, lg, lb, gm)


SSD_HALO = 8
SSD_XBC = SSD_WIDTH + 2 * SSD_NGROUPS * SSD_STATE
GROUP_W = SSD_WIDTH // SSD_NGROUPS
HEADS_PER_GROUP = SSD_HEADS // SSD_NGROUPS


def _ssd_chunk(r0, cbuf, z_ref, dt_ref, cw_ref, cb_ref, dtb_ref, alog_ref, dsk_ref, ng_ref, tril_ref, exp_ref,
               o_ref, state):
    L = SSD_CHUNK
    first = SSD_HALO - (SSD_CONV - 1) + r0
    acc = jnp.broadcast_to(cb_ref[...], (L, SSD_XBC))
    for j in range(SSD_CONV):
        acc = acc + cw_ref[j:j + 1, :] * cbuf[first + j:first + j + L, :]
    xc = _silu(acc)
    xs = xc[:, :SSD_WIDTH]
    bm = xc[:, SSD_WIDTH:SSD_WIDTH + SSD_NGROUPS * SSD_STATE]
    cm = xc[:, SSD_WIDTH + SSD_NGROUPS * SSD_STATE:]

    lane = lax.broadcasted_iota(jnp.int32, (1, LANES), 1)
    v = dt_ref[0, r0:r0 + L, :] + dtb_ref[...]
    dt = jnp.maximum(v, 0.0) + jnp.log1p(jnp.exp(-jnp.abs(v)))
    a = jnp.where(lane < SSD_HEADS, -jnp.exp(alog_ref[...]), 0.0)
    cs = _dot_f32_rhs(tril_ref[...], dt * a)
    cs_t = cs.T
    cs_last = cs[L - 1:L, :]
    emat = exp_ref[...]
    dt_e = _dot_f32_lhs(dt, emat)
    ecs_e = _dot_f32_lhs(jnp.exp(cs), emat)
    ds_e = _dot_f32_lhs(jnp.exp(cs_last - cs), emat)
    cd_e = _dot_f32_lhs(jnp.broadcast_to(jnp.exp(cs_last), (8, LANES)), emat)[0:1, :]

    xd = xs * dt_e
    xdb = xd.astype(BF16)
    xds = (xd * ds_e).astype(BF16)
    rows = lax.broadcasted_iota(jnp.int32, (L, L), 0)
    cols = lax.broadcasted_iota(jnp.int32, (L, L), 1)
    causal = rows >= cols
    ys = []
    for g in range(SSD_NGROUPS):
        cmg = cm[:, g * SSD_STATE:(g + 1) * SSD_STATE].astype(BF16)
        bmg = bm[:, g * SSD_STATE:(g + 1) * SSD_STATE]
        cbm = _dot_nt(cmg, bmg.astype(BF16))
        yd = []
        for r in range(HEADS_PER_GROUP):
            h = g * HEADS_PER_GROUP + r
            seg = cs[:, h:h + 1] - cs_t[h:h + 1, :]
            dec = jnp.exp(jnp.where(causal, seg, -jnp.inf))
            mix = (cbm * dec).astype(BF16)
            yd.append(_dot(mix, xdb[:, h * SSD_HEADDIM:(h + 1) * SSD_HEADDIM]))
        gs = slice(g * GROUP_W, (g + 1) * GROUP_W)
        prev = state[g]
        y_off = _dot(cmg, prev.astype(BF16)) * ecs_e[:, gs]
        st_new = _dot(bmg.T.astype(BF16), xds[:, gs])
        state[g] = prev * cd_e[:, gs] + st_new
        ys.append(jnp.concatenate(yd, axis=1) + y_off)
    y = jnp.concatenate(ys, axis=1) + dsk_ref[...] * xs
    yg = y * _silu(z_ref[0, r0:r0 + L, :])
    outs = []
    for g in range(SSD_NGROUPS):
        ygg = yg[:, g * GROUP_W:(g + 1) * GROUP_W]
        outs.append(ygg * lax.rsqrt(jnp.mean(ygg * ygg, axis=-1, keepdims=True) + RMS_EPS))
    o_ref[0, r0:r0 + L, :] = (jnp.concatenate(outs, axis=1) * ng_ref[...]).astype(BF16)


def _ssd_kernel(xbc_ref, z_ref, dt_ref, cw_ref, cb_ref, dtb_ref, alog_ref, dsk_ref, ng_ref, tril_ref, exp_ref,
                o_ref, cbuf, state, *, rows):
    @pl.when(pl.program_id(1) == 0)
    def _():
        cbuf[0:SSD_HALO, :] = jnp.zeros((SSD_HALO, SSD_XBC), F32)
        state[...] = jnp.zeros(state.shape, F32)

    cbuf[SSD_HALO:SSD_HALO + rows, :] = xbc_ref[0]
    for r0 in range(0, rows, SSD_CHUNK):
        _ssd_chunk(r0, cbuf, z_ref, dt_ref, cw_ref, cb_ref, dtb_ref, alog_ref, dsk_ref, ng_ref, tril_ref, exp_ref,
                   o_ref, state)
    cbuf[0:SSD_HALO, :] = cbuf[rows:rows + SSD_HALO, :]


def _ssd(xbc, z, dtm, cw, cb, dtb, alog, dsk, ng, tril, emat, rows):
    bsz, s, _ = xbc.shape
    full = lambda a: pl.BlockSpec(a.shape, lambda i, j: (0,) * a.ndim)
    blk = lambda n: pl.BlockSpec((1, rows, n), lambda i, j: (i, j, 0))
    return pl.pallas_call(
        functools.partial(_ssd_kernel, rows=rows),
        grid=(bsz, s // rows),
        in_specs=[blk(SSD_XBC), blk(SSD_WIDTH), blk(LANES),
                  full(cw), full(cb), full(dtb), full(alog), full(dsk), full(ng), full(tril), full(emat)],
        out_specs=blk(SSD_WIDTH),
        out_shape=jax.ShapeDtypeStruct((bsz, s, SSD_WIDTH), BF16),
        scratch_shapes=[pltpu.VMEM((SSD_HALO + rows, SSD_XBC), F32),
                        pltpu.VMEM((SSD_NGROUPS, SSD_STATE, GROUP_W), F32)],
        compiler_params=_cparams(("parallel", "arbitrary")),
        name="ssd",
    )(xbc, z, dtm, cw, cb, dtb, alog, dsk, ng, tril, emat)


def _attn_kernel(q_ref, k_ref, vt_ref, o_ref, m_sc, l_sc, acc_sc, st0, st1, *, tq, tk):
    i = pl.program_id(1)
    krow = lax.broadcasted_iota(jnp.int32, (tk, tq), 0)
    qcol = lax.broadcasted_iota(jnp.int32, (tk, tq), 1)
    ones = jnp.ones((16, tk), BF16)
    m_sc[...] = jnp.full(m_sc.shape, -jnp.inf, F32)
    l_sc[...] = jnp.zeros(l_sc.shape, F32)
    acc_sc[...] = jnp.zeros(acc_sc.shape, F32)

    def scores(j, st_ref):
        start = pl.multiple_of(j * tk, tk)
        for h in range(MLA_HEADS):
            hs = slice(h * HEAD_PAD, (h + 1) * HEAD_PAD)
            st_ref[h] = _dot_nt(k_ref[pl.ds(start, tk), hs], q_ref[:, hs])

    def update(j, st_ref, key0):
        start = pl.multiple_of(j * tk, tk)
        for h in range(MLA_HEADS):
            vs = slice(h * MLA_V, (h + 1) * MLA_V)
            st = st_ref[h]
            if key0 is not None:
                st = jnp.where(krow + key0 <= qcol, st, -jnp.inf)
            m = m_sc[h:h + 1, :]
            m_new = jnp.maximum(m, jnp.max(st, axis=0, keepdims=True))
            p = jnp.exp2(st - m_new).astype(BF16)
            alpha = jnp.exp2(m - m_new)
            m_sc[h:h + 1, :] = m_new
            lhs = jnp.concatenate([vt_ref[vs, pl.ds(start, tk)], ones], axis=0)
            pv = _dot(lhs, p)
            l_sc[h:h + 1, :] = alpha * l_sc[h:h + 1, :] + pv[MLA_V:MLA_V + 1, :]
            acc_sc[vs, :] = alpha * acc_sc[vs, :] + pv[:MLA_V, :]

    def pair(jp, c):
        j0 = 2 * jp
        scores(j0 + 1, st1)
        update(j0, st0, None)
        scores(j0 + 2, st0)
        update(j0 + 1, st1, None)
        return c

    scores(0, st0)
    lax.fori_loop(0, i, pair, 0)
    scores(2 * i + 1, st1)
    update(2 * i, st0, 0)
    update(2 * i + 1, st1, tk)
    outs = [acc_sc[h * MLA_V:(h + 1) * MLA_V, :] / l_sc[h:h + 1, :] for h in range(MLA_HEADS)]
    o_ref[...] = jnp.concatenate(outs, axis=0).T.astype(BF16)


def _attention(q, k, vt, bsz, tk):
    t = q.shape[0]
    s = t // bsz
    tq = 2 * tk
    nblk = s // tq
    nq = MLA_HEADS * HEAD_PAD
    nv = MLA_HEADS * MLA_V
    return pl.pallas_call(
        functools.partial(_attn_kernel, tq=tq, tk=tk),
        grid=(bsz, nblk),
        in_specs=[pl.BlockSpec((tq, nq), lambda b, i: (b * nblk + i, 0)),
                  pl.BlockSpec((s, nq), lambda b, i: (b, 0)),
                  pl.BlockSpec((nv, s), lambda b, i: (0, b))],
        out_specs=pl.BlockSpec((tq, nv), lambda b, i: (b * nblk + i, 0)),
        out_shape=jax.ShapeDtypeStruct((t, nv), BF16),
        scratch_shapes=[pltpu.VMEM((8, tq), F32), pltpu.VMEM((8, tq), F32), pltpu.VMEM((nv, tq), F32),
                        pltpu.VMEM((MLA_HEADS, tk, tq), F32), pltpu.VMEM((MLA_HEADS, tk, tq), F32)],
        compiler_params=_cparams(("parallel", "parallel")),
        name="mla_attention",
    )(q, k, vt)


M_IDX0, M_IDX1, M_GATE0, M_GATE1 = range(4)


def _outproj_body(yc, ys, ym, h, w_ref, g_ref):
    acc = _dot(yc, w_ref[0:CONV_WIDTH, :])
    acc = acc + _dot(ys, w_ref[CONV_WIDTH:CONV_WIDTH + SSD_WIDTH, :])
    acc = acc + _dot(ym, w_ref[CONV_WIDTH + SSD_WIDTH:, :])
    h1 = h + acc
    return h1, _rms(h1, g_ref[...])


def _outproj_moe_kernel(yc_ref, ys_ref, ym_ref, h_ref, w_ref, g_ref, r_ref,
                        h1_ref, hn_ref, meta_ref, tcnt_ref, cnt_ref, *, sub):
    @pl.when(pl.program_id(0) == 0)
    def _():
        cnt_ref[...] = jnp.zeros(cnt_ref.shape, F32)

    cnt = cnt_ref[...]
    for k, r0 in enumerate(range(0, h_ref.shape[0], sub)):
        rs = slice(r0, r0 + sub)
        h1, hn = _outproj_body(yc_ref[rs, :], ys_ref[rs, :], ym_ref[rs, :], h_ref[rs, :], w_ref, g_ref)
        h1_ref[rs, :] = h1
        hnb = hn.astype(BF16)
        hn_ref[rs, :] = hnb
        logits = _dot(hnb, r_ref[...])
        lane = lax.broadcasted_iota(jnp.int32, logits.shape, 1)
        lm = jnp.where(lane < N_EXPERTS, logits, -jnp.inf)
        m1 = jnp.max(lm, axis=-1, keepdims=True)
        i1 = jnp.min(jnp.where(lm == m1, lane, LANES), axis=-1, keepdims=True)
        lm2 = jnp.where(lane == i1, -jnp.inf, lm)
        m2 = jnp.max(lm2, axis=-1, keepdims=True)
        i2 = jnp.min(jnp.where(lm2 == m2, lane, LANES), axis=-1, keepdims=True)
        e = jnp.exp(m2 - m1)
        g1 = 1.0 / (1.0 + e)
        g2 = e / (1.0 + e)
        onehot = jnp.where((lane == i1) | (lane == i2), 1.0, 0.0)
        tcnt_ref[k] = cnt
        cnt = cnt + jnp.sum(onehot, axis=0, keepdims=True)
        meta = jnp.where(lane == M_IDX0, i1.astype(F32), 0.0)
        meta = jnp.where(lane == M_IDX1, i2.astype(F32), meta)
        meta = jnp.where(lane == M_GATE0, g1, meta)
        meta = jnp.where(lane == M_GATE1, g2, meta)
        meta_ref[rs, :] = meta
    cnt_ref[...] = cnt


def _outproj_moe(yc, ys, ym, h, w, g, tm, sub, router):
    t = h.shape[0]
    row = lambda n: pl.BlockSpec((tm, n), lambda i: (i, 0))
    full = lambda a: pl.BlockSpec(a.shape, lambda i: (0,) * a.ndim)
    return pl.pallas_call(
        functools.partial(_outproj_moe_kernel, sub=sub),
        grid=(t // tm,),
        in_specs=[row(CONV_WIDTH), row(SSD_WIDTH), row(MLA_HEADS * MLA_V), row(D_MODEL), full(w), full(g),
                  full(router)],
        out_specs=[row(D_MODEL), row(D_MODEL), row(LANES),
                   pl.BlockSpec((tm // sub, 8, LANES), lambda i: (i, 0, 0)), pl.BlockSpec((8, LANES), lambda i: (0, 0))],
        out_shape=[jax.ShapeDtypeStruct((t, D_MODEL), F32), jax.ShapeDtypeStruct((t, D_MODEL), BF16),
                   jax.ShapeDtypeStruct((t, LANES), F32), jax.ShapeDtypeStruct((t // sub, 8, LANES), F32),
                   jax.ShapeDtypeStruct((8, LANES), F32)],
        compiler_params=_cparams(("arbitrary",)),
        name="outproj_moe",
    )(yc, ys, ym, h, w, g, router)


def _ple_update(h, p, pg_ref, pwg_ref, pwp_ref, fg_ref, final):
    gate = _sigmoid(_dot(_rms(h, pg_ref[...]).astype(BF16), pwg_ref[...]))
    out = h + _dot(p.astype(BF16), pwp_ref[...]) * gate
    if final:
        out = _rms(out, fg_ref[...])
    return out


def _p_spec(tm, seq, layer):
    tps = seq // tm
    return pl.BlockSpec((1, 1, tm, PLE_DIM), lambda i, *_: (layer, i // tps, i % tps, 0))


def _ffn_kernel(yc_ref, ys_ref, ym_ref, h_ref, wo_ref, g_ref, wg_ref, wu_ref, wd_ref,
                p_ref, pg_ref, pwg_ref, pwp_ref, fg_ref, o_ref, hn_sc, *, final):
    f = pl.program_id(1)
    half = hn_sc.shape[0] // 2
    halves = [slice(r0, r0 + half) for r0 in (0, half)]

    @pl.when(f == 0)
    def _():
        for rs in halves:
            h1, hn = _outproj_body(yc_ref[rs, :], ys_ref[rs, :], ym_ref[rs, :], h_ref[rs, :], wo_ref, g_ref)
            o_ref[rs, :] = h1
            hn_sc[rs, :] = hn.astype(BF16)

    wg, wu, wd = wg_ref[0].astype(BF16), wu_ref[0].astype(BF16), wd_ref[0].astype(BF16)
    for rs in halves:
        x = hn_sc[rs, :]
        mid = (_silu(_dot(x, wg)) * _dot(x, wu)).astype(BF16)
        o_ref[rs, :] += _dot(mid, wd)

    @pl.when(f == pl.num_programs(1) - 1)
    def _():
        for rs in halves:
            o_ref[rs, :] = _ple_update(o_ref[rs, :], p_ref[0, 0, rs, :], pg_ref, pwg_ref, pwp_ref, fg_ref, final)


def _outproj_ffn_ple_dense(yc, ys, ym, h, wo, g, wg, wu, wd, p, pg, pwg, pwp, fg, layer, ffn_layer, seq, tm, fc, final):
    t = h.shape[0]
    row = lambda n: pl.BlockSpec((tm, n), lambda i, f: (i, 0))
    full = lambda a: pl.BlockSpec(a.shape, lambda i, f: (0,) * a.ndim)
    return pl.pallas_call(
        functools.partial(_ffn_kernel, final=final),
        grid=(t // tm, D_FF // fc),
        in_specs=[row(CONV_WIDTH), row(SSD_WIDTH), row(MLA_HEADS * MLA_V), row(D_MODEL), full(wo), full(g),
                  pl.BlockSpec((1, D_MODEL, fc), lambda i, f: (ffn_layer, 0, f)),
                  pl.BlockSpec((1, D_MODEL, fc), lambda i, f: (ffn_layer, 0, f)),
                  pl.BlockSpec((1, fc, D_MODEL), lambda i, f: (ffn_layer, f, 0)),
                  _p_spec(tm, seq, layer), full(pg), full(pwg), full(pwp), full(fg)],
        out_specs=row(D_MODEL),
        out_shape=jax.ShapeDtypeStruct((t, D_MODEL), F32),
        scratch_shapes=[pltpu.VMEM((tm, D_MODEL), BF16)],
        compiler_params=_cparams(("parallel", "arbitrary")),
        name="outproj_ffn_ple_dense",
    )(yc, ys, ym, h, wo, g, wg, wu, wd, p, pg, pwg, pwp, fg)


ROW_TILES = D_MODEL // LANES


def _rows_to_tiles(val):
    blocks = jnp.stack([val[:, c * LANES:(c + 1) * LANES] for c in range(ROW_TILES)], axis=0)
    return jnp.transpose(blocks, (1, 0, 2))


def _tiles_to_rows(val):
    blocks = jnp.transpose(val, (1, 0, 2))
    return jnp.concatenate([blocks[c] for c in range(ROW_TILES)], axis=1)


def _run_bits(tm):
    return [1 << b for b in range(tm.bit_length() - 1, -1, -1)]


def _run_copies(n, src_ref, src0, dst_ref, dst0, sem, tm, wait, src_step=1):
    off = 0
    for b in _run_bits(tm):
        part = n & b

        @pl.when(part != 0)
        def _(off=off, b=b):
            cp = pltpu.make_async_copy(src_ref.at[pl.ds(src0 + off * src_step, b)],
                                       dst_ref.at[pl.ds(dst0 + off, b)], sem)
            if wait:
                cp.wait()
            else:
                cp.start()

        off = off + part


def _sorted_positions(meta, ltri, upper):
    lane = lax.broadcasted_iota(jnp.int32, meta.shape, 1)
    i1 = meta[:, M_IDX0:M_IDX0 + 1].astype(jnp.int32)
    i2 = meta[:, M_IDX1:M_IDX1 + 1].astype(jnp.int32)
    onehot = jnp.where((lane == i1) | (lane == i2), 1.0, 0.0)
    before = _dot(ltri, onehot.astype(BF16))
    n = jnp.broadcast_to(jnp.sum(onehot, axis=0, keepdims=True), (8, LANES))
    loff = _dot_f32_lhs(n, upper)[0:1, :]
    pos = before + loff
    q1 = jnp.sum(jnp.where(lane == i1, pos, 0.0), axis=-1, keepdims=True)
    q2 = jnp.sum(jnp.where(lane == i2, pos, 0.0), axis=-1, keepdims=True)
    return q1, q2


def _dispatch_kernel(n_ref, lo_ref, g_ref, ps_ref, pn_ref, x_ref, meta_ref, ltri_ref, up_ref, xs_ref, gs_ref,
                     xbuf, zbuf, sems, zsem, *, tm, tiles):
    i = pl.program_id(0)
    nt = pl.num_programs(0)
    slot = i % 2
    zrows = zbuf.shape[0]

    def pads(wait):
        for e in range(N_EXPERTS):
            _run_copies(pn_ref[e], zbuf, 0, xs_ref, ps_ref[e], zsem, zrows, wait, src_step=0)
        tail0, tail_len = ps_ref[N_EXPERTS], pn_ref[N_EXPERTS]
        for c in range(2 * N_EXPERTS):

            @pl.when(c * zrows < tail_len)
            def _(c=c):
                cp = pltpu.make_async_copy(zbuf, xs_ref.at[pl.ds(tail0 + c * zrows, zrows)], zsem)
                if wait:
                    cp.wait()
                else:
                    cp.start()

    @pl.when(i == 0)
    def _():
        zbuf[...] = jnp.zeros(zbuf.shape, F32)
        pads(False)

    srow = lax.broadcasted_iota(jnp.int32, (2 * tm, tm), 0).astype(F32)
    for k in range(tiles):
        rs = slice(k * tm, (k + 1) * tm)
        ss = slice(2 * k * tm, 2 * (k + 1) * tm)
        meta = meta_ref[rs, :]
        q1, q2 = _sorted_positions(meta, ltri_ref[...], up_ref[...])
        lane = lax.broadcasted_iota(jnp.int32, meta.shape, 1)
        qmat = jnp.where(lane == 0, q1, jnp.where(lane == 1, q2, 0.0))
        qt = qmat.T
        p1 = jnp.where(srow == qt[0:1, :], 1.0, 0.0).astype(BF16)
        p2 = jnp.where(srow == qt[1:2, :], 1.0, 0.0).astype(BF16)
        g1 = jnp.broadcast_to(meta[:, M_GATE0:M_GATE0 + 1], (tm, LANES))
        g2 = jnp.broadcast_to(meta[:, M_GATE1:M_GATE1 + 1], (tm, LANES))
        gs_ref[ss, :] = _dot_f32_rhs(p1, g1) + _dot_f32_rhs(p2, g2)
        xbuf[slot, ss] = _rows_to_tiles(_dot(p1 + p2, x_ref[rs, :]))
        for e in range(N_EXPERTS):
            r = (i * tiles + k) * N_EXPERTS + e
            _run_copies(n_ref[r], xbuf.at[slot], 2 * k * tm + lo_ref[r], xs_ref, g_ref[r], sems.at[slot], tm, False)

    def wait_step(sl):
        pltpu.make_async_copy(xbuf.at[sl], xs_ref.at[pl.ds(0, 2 * tm * tiles)], sems.at[sl]).wait()

    @pl.when(i > 0)
    def _():
        wait_step(1 - slot)

    @pl.when(i == nt - 1)
    def _():
        wait_step(slot)
        pads(True)


def _dispatch(n_run, lo_run, g_run, pad_start, pad_len, hn, meta, ltri, upper, n_slots, tm, tiles, tm_moe):
    t = hn.shape[0]
    rows = tm * tiles
    full = lambda a: pl.BlockSpec(a.shape, lambda i, *_: (0,) * a.ndim)
    grid_spec = pltpu.PrefetchScalarGridSpec(
        num_scalar_prefetch=5,
        grid=(t // rows,),
        in_specs=[pl.BlockSpec((rows, D_MODEL), lambda i, *_: (i, 0)),
                  pl.BlockSpec((rows, LANES), lambda i, *_: (i, 0)),
                  full(ltri), full(upper)],
        out_specs=[pl.BlockSpec(memory_space=pl.ANY), pl.BlockSpec((2 * rows, LANES), lambda i, *_: (i, 0))],
        scratch_shapes=[pltpu.VMEM((2, 2 * rows, ROW_TILES, LANES), F32),
                        pltpu.VMEM((tm_moe // 2, ROW_TILES, LANES), F32),
                        pltpu.SemaphoreType.DMA((2,)), pltpu.SemaphoreType.DMA],
    )
    return pl.pallas_call(
        functools.partial(_dispatch_kernel, tm=tm, tiles=tiles),
        grid_spec=grid_spec,
        out_shape=[jax.ShapeDtypeStruct((n_slots, ROW_TILES, LANES), F32),
                   jax.ShapeDtypeStruct((2 * t, LANES), F32)],
        compiler_params=_cparams(("arbitrary",)),
        name="moe_dispatch",
    )(n_run, lo_run, g_run, pad_start, pad_len, hn, meta, ltri, upper)


def _moe_ffn_kernel(texp_ref, nused_ref, nvalid_ref, x_ref, wg_ref, wu_ref, wd_ref, o_ref, xb, acc, *, tm):
    del texp_ref, nused_ref
    i = pl.program_id(0)
    f = pl.program_id(1)
    half = tm // 2

    @pl.when(f == 0)
    def _():
        xb[...] = _tiles_to_rows(x_ref[...]).astype(BF16)
        acc[...] = jnp.zeros(acc.shape, F32)

    def rows(n):
        wg, wu, wd = wg_ref[0, 0].astype(BF16), wu_ref[0, 0].astype(BF16), wd_ref[0, 0].astype(BF16)
        step = min(n, half)
        for r0 in range(0, n, step):
            x = xb[r0:r0 + step, :]
            mid = (_silu(_dot(x, wg)) * _dot(x, wu)).astype(BF16)
            acc[r0:r0 + step, :] += _dot(mid, wd)

    nvalid = nvalid_ref[i]
    quarter = half // 2

    @pl.when(nvalid > half)
    def _():
        rows(tm)

    @pl.when((nvalid > quarter) & (nvalid <= half))
    def _():
        rows(half)

    @pl.when((nvalid > 0) & (nvalid <= quarter))
    def _():
        rows(quarter)

    @pl.when(f == pl.num_programs(1) - 1)
    def _():
        o_ref[...] = _rows_to_tiles(acc[...])


def _moe_ffn(tile_exp, n_used, n_valid, xs, wg, wu, wd, layer, tm, fc):
    n_slots = xs.shape[0]
    nf = D_FF // fc

    def fsel(i, f, nu):
        return jnp.where(i < nu[0], f, nf - 1)

    grid_spec = pltpu.PrefetchScalarGridSpec(
        num_scalar_prefetch=3,
        grid=(n_slots // tm, nf),
        in_specs=[pl.BlockSpec((tm, ROW_TILES, LANES), lambda i, f, te, nu, nv: (jnp.minimum(i, nu[0] - 1), 0, 0)),
                  pl.BlockSpec((1, 1, D_MODEL, fc), lambda i, f, te, nu, nv: (layer, te[i], 0, fsel(i, f, nu))),
                  pl.BlockSpec((1, 1, D_MODEL, fc), lambda i, f, te, nu, nv: (layer, te[i], 0, fsel(i, f, nu))),
                  pl.BlockSpec((1, 1, fc, D_MODEL), lambda i, f, te, nu, nv: (layer, te[i], fsel(i, f, nu), 0))],
        out_specs=pl.BlockSpec((tm, ROW_TILES, LANES), lambda i, f, te, nu, nv: (i, 0, 0)),
        scratch_shapes=[pltpu.VMEM((tm, D_MODEL), BF16), pltpu.VMEM((tm, D_MODEL), F32)],
    )
    return pl.pallas_call(
        functools.partial(_moe_ffn_kernel, tm=tm),
        grid_spec=grid_spec,
        out_shape=jax.ShapeDtypeStruct((n_slots, ROW_TILES, LANES), F32),
        compiler_params=_cparams(("parallel", "arbitrary")),
        name="moe_ffn",
    )(tile_exp, n_used, n_valid, xs, wg, wu, wd)


def _combine_kernel(n_ref, lo_ref, g_ref, h1_ref, meta_ref, gs_ref, ltri_ref, up_ref,
                    p_ref, pg_ref, pwg_ref, pwp_ref, fg_ref, ye_ref, o_ref, ybuf, sems, *, tm, tiles, final):
    i = pl.program_id(0)
    nt = pl.num_programs(0)
    slot = i % 2

    def fetch(step, sl):
        for k in range(tiles):
            for e in range(N_EXPERTS):
                r = (step * tiles + k) * N_EXPERTS + e
                _run_copies(n_ref[r], ye_ref, g_ref[r], ybuf.at[sl], 2 * k * tm + lo_ref[r], sems.at[sl], tm, False)

    @pl.when(i == 0)
    def _():
        fetch(i, slot)

    @pl.when(i + 1 < nt)
    def _():
        fetch(i + 1, 1 - slot)

    scol = lax.broadcasted_iota(jnp.int32, (tm, 2 * tm), 1).astype(F32)
    sels = []
    for k in range(tiles):
        q1, q2 = _sorted_positions(meta_ref[k * tm:(k + 1) * tm, :], ltri_ref[...], up_ref[...])
        sels.append(jnp.where((scol == q1) | (scol == q2), 1.0, 0.0).astype(BF16))
    pltpu.make_async_copy(ye_ref.at[pl.ds(0, 2 * tm * tiles)], ybuf.at[slot], sems.at[slot]).wait()
    for k in range(tiles):
        rs = slice(k * tm, (k + 1) * tm)
        ss = slice(2 * k * tm, 2 * (k + 1) * tm)
        y = _tiles_to_rows(ybuf[slot, ss]) * gs_ref[ss, 0:1]
        hi = y.astype(BF16)
        lo = (y - hi.astype(F32)).astype(BF16)
        h2 = h1_ref[rs, :] + _dot(sels[k], hi) + _dot(sels[k], lo)
        o_ref[rs, :] = _ple_update(h2, p_ref[0, 0, rs, :], pg_ref, pwg_ref, pwp_ref, fg_ref, final)


def _combine_ple(n_run, lo_run, g_run, h1, meta, gs, ltri, upper, p, pg, pwg, pwp, fg, ye, layer, seq, tm, tiles,
                 final):
    t = h1.shape[0]
    rows = tm * tiles
    full = lambda a: pl.BlockSpec(a.shape, lambda i, *_: (0,) * a.ndim)
    grid_spec = pltpu.PrefetchScalarGridSpec(
        num_scalar_prefetch=3,
        grid=(t // rows,),
        in_specs=[pl.BlockSpec((rows, D_MODEL), lambda i, *_: (i, 0)),
                  pl.BlockSpec((rows, LANES), lambda i, *_: (i, 0)),
                  pl.BlockSpec((2 * rows, LANES), lambda i, *_: (i, 0)),
                  full(ltri), full(upper),
                  _p_spec(rows, seq, layer), full(pg), full(pwg), full(pwp), full(fg),
                  pl.BlockSpec(memory_space=pl.ANY)],
        out_specs=pl.BlockSpec((rows, D_MODEL), lambda i, *_: (i, 0)),
        scratch_shapes=[pltpu.VMEM((2, 2 * rows, ROW_TILES, LANES), F32), pltpu.SemaphoreType.DMA((2,))],
    )
    return pl.pallas_call(
        functools.partial(_combine_kernel, tm=tm, tiles=tiles, final=final),
        grid_spec=grid_spec,
        out_shape=jax.ShapeDtypeStruct((t, D_MODEL), F32),
        compiler_params=_cparams(("arbitrary",)),
        name="moe_combine_ple",
    )(n_run, lo_run, g_run, h1, meta, gs, ltri, upper, p, pg, pwg, pwp, fg, ye)


def _swap_halves(w):
    half = w.shape[-1] // 2
    return jnp.concatenate([w[..., half:], w[..., :half]], axis=-1)


def _pad_cols(w, left, total):
    return jnp.pad(w, ((0, 0),) * (w.ndim - 1) + ((left, total - left - w.shape[-1]),))


def _arrange_w_in(w):
    sizes = (512, 512, SSD_XBC, SSD_HEADS, MLA_Q_RANK, MLA_KV_RANK, MLA_ROPE)
    pts = np.cumsum(sizes)[:-1].tolist()
    w = w.astype(BF16)
    w_conv, w_z, w_xbc, w_dt, w_cq, w_ckv, w_kr = jnp.split(w, pts, axis=-1)
    seg_kr = _pad_cols(w_kr, ROPE_LO, LANES)
    seg_dtr = _pad_cols(w_dt, 0, LANES) + _pad_cols(_swap_halves(w_kr), ROPE_LO, LANES)
    return jnp.concatenate([w_conv, w_z, w_xbc, w_cq, w_ckv, seg_kr, seg_dtr], axis=-1)


def _arrange_w_uq(w):
    main, rot = [], []
    for h in range(MLA_HEADS):
        wh = w[:, h * (MLA_NOPE + MLA_ROPE):(h + 1) * (MLA_NOPE + MLA_ROPE)]
        main.append(_pad_cols(wh, 0, HEAD_PAD))
        rot.append(_pad_cols(_swap_halves(wh[:, MLA_NOPE:]), ROPE_LO, HEAD_PAD))
    return jnp.concatenate(main + rot, axis=1).astype(BF16)


def _arrange_w_ukv(w):
    ks, vs = [], []
    for h in range(MLA_HEADS):
        wh = w[:, h * (MLA_NOPE + MLA_V):(h + 1) * (MLA_NOPE + MLA_V)]
        ks.append(_pad_cols(wh[:, :MLA_NOPE], 0, HEAD_PAD))
        vs.append(wh[:, MLA_NOPE:])
    return jnp.concatenate(ks, axis=1).astype(BF16), jnp.concatenate(vs, axis=1).T.astype(BF16)


def _row(v, width=None):
    v = v.reshape(1, -1).astype(F32)
    if width is not None:
        v = jnp.pad(v, ((0, 0), (0, width - v.shape[1])))
    return v


def _pick(n, prefs):
    for c in prefs:
        if n % c == 0:
            return c
    return n


def kernel(x, p, positions, attn_norm_g, w_in, conv_dw_w, conv_dw_b, conv_ln_g, conv_ln_b, ssd_conv_w, ssd_conv_b, ssd_dt_bias, ssd_a_log, ssd_d, ssd_norm_g, mla_q_norm_g, mla_w_uq, mla_kv_norm_g, mla_w_ukv, w_out, ffn_norm_g, dense_w_gate, dense_w_up, dense_w_down, moe_router, moe_w_gate, moe_w_up, moe_w_down, ple_norm_g, ple_w_gate, ple_w_proj, final_norm_g):
    bsz, s, _ = x.shape
    t = bsz * s
    tm_row = _pick(t, (1024, 512, 256, 128))
    tm_ffn = _pick(s, (1024, 512, 256, 128))
    fc = 512
    tc = _pick(s, (256, 128))
    ts = _pick(s, (512, 256, 128))
    tm_moe = _pick(t, (1024, 512, 256, 128))
    tm_tok = _pick(s, (256, 128))
    disp_tiles = _pick(s // tm_tok, (4, 2))
    comb_tiles = _pick(s // tm_tok, (2,))
    n_slots = 2 * t + N_EXPERTS * tm_moe

    inv = ROPE_BASE ** (-jnp.arange(0, MLA_ROPE, 2, dtype=F32) / MLA_ROPE)
    inv128 = _pad_cols(jnp.concatenate([inv, inv])[None, :], ROPE_LO, LANES)
    pos128 = jnp.broadcast_to(positions.astype(F32).reshape(t, 1), (t, LANES))
    ctab, stab = _rope_tables(pos128, inv128, tm_row)
    grp = np.arange(CONV_WIDTH) // (CONV_WIDTH // CONV_GROUPS)
    gmean = jnp.asarray((grp[:, None] == grp[None, :]) / (CONV_WIDTH // CONV_GROUPS), BF16)
    tril = jnp.asarray(np.tril(np.ones((SSD_CHUNK, SSD_CHUNK))), BF16)
    hd = np.arange(SSD_WIDTH) // SSD_HEADDIM
    emat = jnp.asarray(np.arange(LANES)[:, None] == hd[None, :], BF16)
    ltri = jnp.asarray(np.tril(np.ones((tm_tok, tm_tok)), -1), BF16)
    upper = jnp.asarray(np.triu(np.ones((LANES, LANES)), 1), BF16)

    w_in_all = _arrange_w_in(w_in)
    dense_wg, dense_wu, dense_wd = dense_w_gate, dense_w_up, dense_w_down
    moe_wg, moe_wu, moe_wd = moe_w_gate, moe_w_up, moe_w_down

    h = x.reshape(t, D_MODEL)
    for i in range(DEPTH):
        u_conv, z, xbc, dtm, q, k, vt = _inproj(
            h, _row(attn_norm_g[i]), w_in_all, i, ctab, stab,
            _row(mla_q_norm_g[i]), _arrange_w_uq(mla_w_uq[i]),
            _row(mla_kv_norm_g[i]), *_arrange_w_ukv(mla_w_ukv[i]), tm_row)
        y_conv = _conformer_conv(
            u_conv.reshape(bsz, s, -1), jnp.pad(conv_dw_w[i], ((0, 1), (0, 0))), _row(conv_dw_b[i]),
            _row(conv_ln_g[i]), _row(conv_ln_b[i]), gmean, ts)
        y_ssd = _ssd(
            xbc.reshape(bsz, s, -1), z.reshape(bsz, s, -1), dtm.reshape(bsz, s, -1),
            jnp.pad(ssd_conv_w[i], ((0, 8 - SSD_CONV), (0, 0))), _row(ssd_conv_b[i]),
            _row(ssd_dt_bias[i], LANES), _row(ssd_a_log[i], LANES),
            _row(jnp.repeat(ssd_d[i], SSD_HEADDIM)), _row(ssd_norm_g[i]), tril, emat, ts)
        ym = _attention(q, k, vt, bsz, tc)
        yc, ys = y_conv.reshape(t, -1), y_ssd.reshape(t, -1)
        wo = w_out[i].astype(BF16)
        j = i // 2
        final = i == DEPTH - 1
        ple = (p, _row(ple_norm_g[i]), ple_w_gate[i].astype(BF16), ple_w_proj[i].astype(BF16), _row(final_norm_g))
        if i % 2 == 0:
            h = _outproj_ffn_ple_dense(yc, ys, ym, h, wo, _row(ffn_norm_g[i]), dense_wg, dense_wu, dense_wd, *ple,
                                       i, j, s, tm_ffn, fc, final)
        else:
            router = _pad_cols(moe_router[j], 0, LANES).astype(BF16)
            h1, hn, meta, tcnt, cnt = _outproj_moe(yc, ys, ym, h, wo, _row(ffn_norm_g[i]), tm_ffn, tm_tok,
                                                   router)
            counts = cnt[0, :N_EXPERTS].astype(jnp.int32)
            padded = ((counts + tm_moe - 1) // tm_moe) * tm_moe
            pends = jnp.cumsum(padded)
            pstarts = pends - padded
            before = tcnt[:, 0, :N_EXPERTS].astype(jnp.int32)
            n_run = jnp.concatenate([before[1:], counts[None, :]], axis=0) - before
            lo_run = jnp.cumsum(n_run, axis=1) - n_run
            g_run = pstarts[None, :] + before
            runs = (n_run.reshape(-1), lo_run.reshape(-1), g_run.reshape(-1))
            n_tiles = n_slots // tm_moe
            n_used = (pends[-1] // tm_moe).astype(jnp.int32)
            tile_start = jnp.arange(n_tiles, dtype=jnp.int32) * tm_moe
            tile_exp = jnp.minimum(jnp.sum(pends[None, :] <= tile_start[:, None], axis=1), N_EXPERTS - 1)
            tile_exp = tile_exp.astype(jnp.int32)
            n_valid = jnp.clip((pstarts + counts)[tile_exp] - tile_start, 0, tm_moe)
            tile_exp = jnp.where(jnp.arange(n_tiles) < n_used, tile_exp, tile_exp[jnp.maximum(n_used - 1, 0)])
            pad_start = jnp.concatenate([pstarts + counts, pends[-1:]])
            pad_len = jnp.concatenate([padded - counts, n_slots - pends[-1:]])
            xs, gs = _dispatch(*runs, pad_start, pad_len, hn, meta, ltri, upper, n_slots, tm_tok, disp_tiles, tm_moe)
            ye = _moe_ffn(tile_exp, n_used.reshape(1), n_valid, xs, moe_wg, moe_wu, moe_wd, j, tm_moe, fc)
            h = _combine_ple(*runs, h1, meta, gs, ltri, upper, *ple, ye, i, s, tm_tok, comb_tiles, final)
    return h.reshape(bsz, s, D_MODEL)
```
